```python
import math
import jax, jax.numpy as jnp
from jax import lax
import numpy as np

D_MODEL = 2048
BATCH = 4
SEQ = 8192
DEPTH = 1

N_META = 16
LEAD = 128
N_PAD = LEAD - N_META

MIX_WIDTH = D_MODEL
ATTN_HEADS = 8
ATTN_QK_DIM = 64
ATTN_V_DIM = 2 * ATTN_QK_DIM
ATTN_WIDTH = ATTN_HEADS * ATTN_V_DIM
DN_HEADS = 8
DN_DK = 128
DN_DV = 128
DN_WIDTH = DN_HEADS * DN_DV
CONV_K = 4
CHUNK = 64
Q_BLOCK = 128
FFN_HIDDEN = -(-8 * D_MODEL // (3 * 256)) * 256
EPS = 1e-6
NEG = -1e30

A_Q = ATTN_HEADS * 2 * ATTN_QK_DIM
A_K = ATTN_HEADS * 2 * ATTN_QK_DIM
A_V = ATTN_WIDTH
D_Q = DN_HEADS * DN_DK
D_K = DN_HEADS * DN_DK
D_V = DN_WIDTH
D_Z = DN_WIDTH
D_B = DN_HEADS
D_A = DN_HEADS
COL_SIZES = (A_Q, A_K, A_V, D_Q, D_K, D_V, D_Z, D_B, D_A)
SPLITS = tuple(int(s) for s in np.cumsum(COL_SIZES)[:-1])
IN_COLS = int(sum(COL_SIZES))
CONV_CH = D_Q + D_K + D_V

kernel_name = "hymba_diffattn_gdn_alibi_meta"


def rmsnorm(x, w):
    xf = x.astype(jnp.float32)
    y = xf * lax.rsqrt(jnp.mean(xf * xf, axis=-1, keepdims=True) + EPS)
    return (y * w.astype(jnp.float32)).astype(x.dtype)


def l2norm(x):
    return x * lax.rsqrt(jnp.sum(x * x, axis=-1, keepdims=True) + EPS)


def diff_attention(q, k, v, q_norm_w, k_norm_w, lq1, lk1, lq2, lk2, subln_w, lambda_init):
    dtype = q.dtype
    B, L = q.shape[0], q.shape[1]
    f32 = jnp.float32
    q = rmsnorm(q.reshape(B, L, ATTN_HEADS, 2, ATTN_QK_DIM), q_norm_w).astype(f32)
    k = rmsnorm(k.reshape(B, L, ATTN_HEADS, 2, ATTN_QK_DIM), k_norm_w).astype(f32)
    v = v.reshape(B, L, ATTN_HEADS, ATTN_V_DIM).astype(f32)
    lam = (jnp.exp(jnp.sum(lq1.astype(f32) * lk1.astype(f32)))
           - jnp.exp(jnp.sum(lq2.astype(f32) * lk2.astype(f32))) + lambda_init)
    slopes = 2.0 ** (-8.0 * jnp.arange(1, ATTN_HEADS + 1, dtype=f32) / ATTN_HEADS)
    kpos = jnp.arange(L)
    key_ok = kpos >= N_PAD
    key_real = kpos >= LEAD
    n_blocks = L // Q_BLOCK
    qb = jnp.moveaxis(q.reshape(B, n_blocks, Q_BLOCK, ATTN_HEADS, 2, ATTN_QK_DIM), 1, 0)
    scale = ATTN_QK_DIM ** -0.5

    def one_block(args):
        q_blk, start = args
        qpos = start + jnp.arange(Q_BLOCK)
        s = jnp.einsum('bqhmd,bkhmd->bhmqk', q_blk, k) * scale
        dist = (qpos[:, None] - kpos[None, :]).astype(f32)
        bias = jnp.where(key_real[None, :], -slopes[:, None, None] * dist, 0.0)
        allowed = (kpos[None, :] <= qpos[:, None]) & key_ok[None, :]
        s = jnp.where(allowed, s + bias[None, :, None], NEG)
        p = jax.nn.softmax(s, axis=-1)
        p = p[:, :, 0] - lam * p[:, :, 1]
        return jnp.einsum('bhqk,bkhd->bqhd', p, v)

    o = lax.map(one_block, (qb, jnp.arange(n_blocks) * Q_BLOCK))
    o = jnp.moveaxis(o, 0, 1).reshape(B, L, ATTN_HEADS, ATTN_V_DIM)
    o = rmsnorm(o, subln_w) * (1.0 - lambda_init)
    return o.reshape(B, L, ATTN_WIDTH).astype(dtype)


def causal_conv(x, w):
    L = x.shape[1]
    xp = jnp.pad(x, ((0, 0), (CONV_K - 1, 0), (0, 0)))
    return sum(xp[:, j:j + L, :] * w[:, j].astype(x.dtype) for j in range(CONV_K))


def gated_deltanet(q, k, v, z, b, a, conv_w, a_log, dt_bias, o_norm_w):
    dtype = q.dtype
    B, L = q.shape[0], q.shape[1]
    f32 = jnp.float32
    valid = jnp.arange(L) >= N_PAD
    qkv = jnp.concatenate([q, k, v], axis=-1) * valid[None, :, None].astype(dtype)
    qkv = jax.nn.silu(causal_conv(qkv, conv_w)).astype(f32)
    q, k, v = jnp.split(qkv, [D_Q, D_Q + D_K], axis=-1)
    q = l2norm(q.reshape(B, L, DN_HEADS, DN_DK)) * (DN_DK ** -0.5)
    k = l2norm(k.reshape(B, L, DN_HEADS, DN_DK))
    v = v.reshape(B, L, DN_HEADS, DN_DV)
    vmask = valid.astype(f32)[None, :, None]
    beta = jax.nn.sigmoid(b.astype(f32)) * vmask
    g = -jnp.exp(a_log.astype(f32)) * jax.nn.softplus(a.astype(f32) + dt_bias.astype(f32)) * vmask
    n = L // CHUNK

    def chunks(t):
        t = t.reshape((B, n, CHUNK) + t.shape[2:])
        return jnp.moveaxis(t, 3, 1)

    qc, kc, vc = chunks(q), chunks(k), chunks(v)
    bc = chunks(beta)
    gc = jnp.cumsum(chunks(g), axis=-1)
    idx = jnp.arange(CHUNK)
    incl = idx[:, None] >= idx[None, :]
    strict = idx[:, None] > idx[None, :]
    diff = gc[..., :, None] - gc[..., None, :]
    decay = jnp.where(incl, jnp.exp(jnp.where(incl, diff, 0.0)), 0.0)
    kb = kc * bc[..., None]
    lmat = jnp.where(strict, jnp.einsum('bhnid,bhnjd->bhnij', kb, kc) * decay, 0.0)
    tmat = lmat + jnp.eye(CHUNK, dtype=f32)
    u = lax.linalg.triangular_solve(tmat, vc * bc[..., None], left_side=True, lower=True,
                                    unit_diagonal=True)
    w = lax.linalg.triangular_solve(tmat, kb * jnp.exp(gc)[..., None], left_side=True,
                                    lower=True, unit_diagonal=True)
    qk = jnp.einsum('bhnid,bhnjd->bhnij', qc, kc) * decay

    def step(S, xs):
        q_c, k_c, u_c, w_c, qk_c, g_c = xs
        v_new = u_c - jnp.einsum('bhck,bhkv->bhcv', w_c, S)
        o = (jnp.einsum('bhck,bhkv->bhcv', q_c * jnp.exp(g_c)[..., None], S)
             + jnp.einsum('bhij,bhjv->bhiv', qk_c, v_new))
        g_last = g_c[..., -1:]
        S = (S * jnp.exp(g_last)[..., None]
             + jnp.einsum('bhck,bhcv->bhkv', k_c * jnp.exp(g_last - g_c)[..., None], v_new))
        return S, o

    xs = (jnp.moveaxis(qc, 2, 0), jnp.moveaxis(kc, 2, 0), jnp.moveaxis(u, 2, 0),
          jnp.moveaxis(w, 2, 0), jnp.moveaxis(qk, 2, 0), jnp.moveaxis(gc, 2, 0))
    S0 = jnp.zeros((B, DN_HEADS, DN_DK, DN_DV), f32)
    _, o = lax.scan(step, S0, xs)
    o = jnp.moveaxis(o, 0, 2).reshape(B, DN_HEADS, L, DN_DV).transpose(0, 2, 1, 3)
    o = rmsnorm(o, o_norm_w) * jax.nn.silu(z.reshape(B, L, DN_HEADS, DN_DV).astype(f32))
    return o.reshape(B, L, DN_WIDTH).astype(dtype)


def setup_inputs(seed: int = 0) -> dict:
    key = jax.random.key(seed)
    ks = jax.random.split(key, 20)
    f32 = jnp.float32

    def normal(k, shape, scale):
        return jax.random.normal(k, shape, f32) * scale

    def gain(k, shape):
        return 1.0 + 0.02 * jax.random.normal(k, shape, f32)

    dt = jnp.exp(jax.random.uniform(ks[13], (DEPTH, DN_HEADS), f32,
                                    math.log(1e-3), math.log(1e-1)))
    return {
        "x": normal(ks[0], (BATCH, SEQ, D_MODEL), 1.0),
        "meta_tokens": normal(ks[1], (N_META, D_MODEL), 1.0),
        "attn_norm_w": gain(ks[2], (DEPTH, D_MODEL)),
        "w_in": normal(ks[3], (DEPTH, D_MODEL, IN_COLS), D_MODEL ** -0.5),
        "q_norm_w": gain(ks[4], (DEPTH, ATTN_QK_DIM)),
        "k_norm_w": gain(ks[5], (DEPTH, ATTN_QK_DIM)),
        "lambda_q1": normal(ks[6], (DEPTH, ATTN_QK_DIM), 0.1),
        "lambda_k1": normal(ks[7], (DEPTH, ATTN_QK_DIM), 0.1),
        "lambda_q2": normal(ks[8], (DEPTH, ATTN_QK_DIM), 0.1),
        "lambda_k2": normal(ks[9], (DEPTH, ATTN_QK_DIM), 0.1),
        "subln_w": gain(ks[10], (DEPTH, ATTN_V_DIM)),
        "conv_w": normal(ks[11], (DEPTH, CONV_CH, CONV_K), CONV_K ** -0.5),
        "a_log": jnp.log(jax.random.uniform(ks[12], (DEPTH, DN_HEADS), f32, 1.0, 16.0)),
        "dt_bias": dt + jnp.log(-jnp.expm1(-dt)),
        "o_norm_w": gain(ks[14], (DEPTH, DN_DV)),
        "w_out": normal(ks[15], (DEPTH, MIX_WIDTH, D_MODEL), MIX_WIDTH ** -0.5),
        "ffn_norm_w": gain(ks[16], (DEPTH, D_MODEL)),
        "w_gate": normal(ks[17], (DEPTH, D_MODEL, FFN_HIDDEN), D_MODEL ** -0.5),
        "w_up": normal(ks[18], (DEPTH, D_MODEL, FFN_HIDDEN), D_MODEL ** -0.5),
        "w_down": normal(ks[19], (DEPTH, FFN_HIDDEN, D_MODEL), FFN_HIDDEN ** -0.5),
    }


def reference(x, meta_tokens, attn_norm_w, w_in, q_norm_w, k_norm_w, lambda_q1, lambda_k1,
              lambda_q2, lambda_k2, subln_w, conv_w, a_log, dt_bias, o_norm_w, w_out,
              ffn_norm_w, w_gate, w_up, w_down):
    B = x.shape[0]
    lead = jnp.concatenate(
        [jnp.zeros((B, N_PAD, D_MODEL), x.dtype),
         jnp.broadcast_to(meta_tokens[None].astype(x.dtype), (B, N_META, D_MODEL))], axis=1)
    h = jnp.concatenate([lead, x], axis=1)
    for l in range(DEPTH):
        lambda_init = 0.8 - 0.6 * math.exp(-0.3 * l)
        u = rmsnorm(h, attn_norm_w[l])
        proj = u @ w_in[l]
        aq, ak, av, dq, dk, dv, dz, db, da = jnp.split(proj, SPLITS, axis=-1)
        o_a = diff_attention(aq, ak, av, q_norm_w[l], k_norm_w[l], lambda_q1[l], lambda_k1[l],
                             lambda_q2[l], lambda_k2[l], subln_w[l], lambda_init)
        o_d = gated_deltanet(dq, dk, dv, dz, db, da, conv_w[l], a_log[l], dt_bias[l], o_norm_w[l])
        h = h + jnp.concatenate([o_a, o_d], axis=-1) @ w_out[l]
        u = rmsnorm(h, ffn_norm_w[l])
        h = h + (jax.nn.silu(u @ w_gate[l]) * (u @ w_up[l])) @ w_down[l]
    return h[:, LEAD:]
```

```python
import functools

import jax
import jax.numpy as jnp
from jax import lax
from jax.experimental import pallas as pl
from jax.experimental.pallas import tpu as pltpu

F32 = jnp.float32
BF16 = jnp.bfloat16
HIGHEST = lax.Precision.HIGHEST

D_MODEL = 2048
N_META = 16
LEAD = 128
N_PAD = LEAD - N_META
HEADS = 8
HEAD_W = 128
QK_DIM = 64
GROUP_W = HEADS * HEAD_W
MAIN_COLS = 7 * GROUP_W
GATE_COLS = 2 * HEADS
CONV_K = 4
FFN_HIDDEN = 5632
EPS = 1e-6
NEG = -1e30
LAMBDA_INIT = 0.2
CHUNK = 128
VMEM_LIMIT = 56 * 1024 * 1024


def _dot(a, b, precision=None):
    return jnp.dot(a, b, preferred_element_type=F32, precision=precision)


def _dot_nt(a, b):
    return lax.dot_general(a, b, (((1,), (1,)), ((), ())), preferred_element_type=F32)


def _rms_rows(x, w_row):
    return x * lax.rsqrt(jnp.mean(x * x, axis=-1, keepdims=True) + EPS) * w_row


def _inproj_kernel(x_ref, nw_ref, w_ref, wg_ref, o_ref, g_ref, u_ref, *, row_chunk):
    j = pl.program_id(1)

    @pl.when(j == 0)
    def _():
        def body(c, carry):
            r = pl.multiple_of(c * row_chunk, row_chunk)
            u = _rms_rows(x_ref[pl.ds(r, row_chunk), :], nw_ref[...]).astype(BF16)
            u_ref[pl.ds(r, row_chunk), :] = u
            g_ref[pl.ds(r, row_chunk), :] = _dot(u, wg_ref[...])
            return carry

        lax.fori_loop(0, x_ref.shape[0] // row_chunk, body, 0)

    o_ref[...] = _dot(u_ref[...], w_ref[...]).astype(o_ref.dtype)


def _inproj(x2d, norm_w, w_main, w_gate, tm, tn):
    m = x2d.shape[0]
    row_chunk = min(256, tm)
    return pl.pallas_call(
        functools.partial(_inproj_kernel, row_chunk=row_chunk),
        grid=(m // tm, MAIN_COLS // tn),
        in_specs=[
            pl.BlockSpec((tm, D_MODEL), lambda i, j: (i, 0)),
            pl.BlockSpec((1, D_MODEL), lambda i, j: (0, 0)),
            pl.BlockSpec((D_MODEL, tn), lambda i, j: (0, j)),
            pl.BlockSpec((D_MODEL, HEAD_W), lambda i, j: (0, 0)),
        ],
        out_specs=[
            pl.BlockSpec((tm, tn), lambda i, j: (i, j)),
            pl.BlockSpec((tm, HEAD_W), lambda i, j: (i, 0)),
        ],
        out_shape=[
            jax.ShapeDtypeStruct((m, MAIN_COLS), BF16),
            jax.ShapeDtypeStruct((m, HEAD_W), F32),
        ],
        scratch_shapes=[pltpu.VMEM((tm, D_MODEL), BF16)],
        compiler_params=pltpu.CompilerParams(
            dimension_semantics=("parallel", "arbitrary"), vmem_limit_bytes=VMEM_LIMIT),
        name="inproj",
    )(x2d, norm_w, w_main, w_gate)


def _halfnorm(x, w_row):
    lo = lax.broadcasted_iota(jnp.int32, x.shape, 1) < QK_DIM
    x2 = x * x
    s_lo = jnp.sum(jnp.where(lo, x2, 0.0), axis=-1, keepdims=True)
    s_hi = jnp.sum(jnp.where(lo, 0.0, x2), axis=-1, keepdims=True)
    ms = jnp.where(lo, s_lo, s_hi) * (1.0 / QK_DIM)
    return x * lax.rsqrt(ms + EPS) * w_row


def _attn_kernel(slopes_ref, lvec_ref, q_ref, k_ref, v_ref, lk_ref, lv_ref, qw_ref, kw_ref,
                 sw_ref, o_ref, kn_ref, lkn_ref, acc_ref, *, tq, seq):
    h = pl.program_id(1)
    i = pl.program_id(2)
    slope = slopes_ref[h]

    @pl.when(i == 0)
    def _():
        def body(c, carry):
            r = pl.multiple_of(c * 256, 256)
            kn_ref[pl.ds(r, 256), :] = _halfnorm(
                k_ref[pl.ds(r, 256), :].astype(F32), kw_ref[...]).astype(BF16)
            return carry

        lax.fori_loop(0, seq // 256, body, 0)
        lkn_ref[...] = _halfnorm(lk_ref[...].astype(F32), kw_ref[...]).astype(BF16)

    lane = lax.broadcasted_iota(jnp.int32, (tq, HEAD_W), 1)
    lo = lane < QK_DIM
    qn = _halfnorm(q_ref[...].astype(F32), qw_ref[...]) * (QK_DIM ** -0.5)
    qs = (jnp.where(lo, qn, 0.0).astype(BF16), jnp.where(lo, 0.0, qn).astype(BF16))

    row = lax.broadcasted_iota(jnp.int32, (tq, tq), 0)
    col = lax.broadcasted_iota(jnp.int32, (tq, tq), 1)
    row_bias = slope * lax.broadcasted_iota(jnp.int32, (tq, HEAD_W), 0).astype(F32)
    lead_ok = lane >= N_PAD
    lv = lv_ref[...]
    ms, ls = [], []
    for mp in range(2):
        s = jnp.where(lead_ok, _dot_nt(qs[mp], lkn_ref[...]) + row_bias, NEG)
        m = jnp.max(s, axis=-1, keepdims=True)
        p = jnp.exp(s - m)
        ms.append(m)
        ls.append(jnp.sum(p, axis=-1, keepdims=True))
        acc_ref[mp] = _dot(p.astype(BF16), lv)

    col_f = lax.broadcasted_iota(jnp.int32, (1, tq), 1).astype(F32)

    def block(j, carry, diag):
        r = pl.multiple_of(j * tq, tq)
        kb = kn_ref[pl.ds(r, tq), :]
        vb = v_ref[pl.ds(r, tq), :]
        bias = slope * (col_f + ((j - i) * tq).astype(F32))
        out = []
        for mp in range(2):
            m, l = carry[2 * mp], carry[2 * mp + 1]
            s = _dot_nt(qs[mp], kb) + bias
            if diag:
                s = jnp.where(col <= row, s, NEG)
            m_new = jnp.maximum(m, jnp.max(s, axis=-1, keepdims=True))
            alpha = jnp.exp(m - m_new)
            p = jnp.exp(s - m_new)
            l_new = alpha * l + jnp.sum(p, axis=-1, keepdims=True)
            acc_ref[mp] = alpha * acc_ref[mp] + _dot(p.astype(BF16), vb)
            out += [m_new, l_new]
        return tuple(out)

    carry = (ms[0], ls[0], ms[1], ls[1])
    carry = lax.fori_loop(0, i, lambda j, c: block(j, c, False), carry)
    carry = block(i, carry, True)

    lv4 = lvec_ref[...]
    lam = (jnp.exp(jnp.sum(lv4[0:1] * lv4[1:2], axis=-1, keepdims=True))
           - jnp.exp(jnp.sum(lv4[2:3] * lv4[3:4], axis=-1, keepdims=True)) + LAMBDA_INIT)
    o = acc_ref[0] / carry[1] - lam * (acc_ref[1] / carry[3])
    o_ref[...] = (_rms_rows(o, sw_ref[...]) * (1.0 - LAMBDA_INIT)).astype(o_ref.dtype)


def _attn(proj3, lead_proj, slopes, lvec, qw, kw, sw, tq):
    b, seq, _ = proj3.shape
    return pl.pallas_call(
        functools.partial(_attn_kernel, tq=tq, seq=seq),
        grid=(b, HEADS, seq // tq),
        in_specs=[
            pl.BlockSpec(memory_space=pltpu.SMEM),
            pl.BlockSpec((4, QK_DIM), lambda b_, h, i: (0, 0)),
            pl.BlockSpec((None, tq, HEAD_W), lambda b_, h, i: (b_, i, h)),
            pl.BlockSpec((None, seq, HEAD_W), lambda b_, h, i: (b_, 0, HEADS + h)),
            pl.BlockSpec((None, seq, HEAD_W), lambda b_, h, i: (b_, 0, 2 * HEADS + h)),
            pl.BlockSpec((LEAD, HEAD_W), lambda b_, h, i: (0, HEADS + h)),
            pl.BlockSpec((LEAD, HEAD_W), lambda b_, h, i: (0, 2 * HEADS + h)),
            pl.BlockSpec((1, HEAD_W), lambda b_, h, i: (0, 0)),
            pl.BlockSpec((1, HEAD_W), lambda b_, h, i: (0, 0)),
            pl.BlockSpec((1, HEAD_W), lambda b_, h, i: (0, 0)),
        ],
        out_specs=pl.BlockSpec((None, tq, HEAD_W), lambda b_, h, i: (b_, i, h)),
        out_shape=jax.ShapeDtypeStruct((b, seq, GROUP_W), BF16),
        scratch_shapes=[
            pltpu.VMEM((seq, HEAD_W), BF16),
            pltpu.VMEM((LEAD, HEAD_W), BF16),
            pltpu.VMEM((2, tq, HEAD_W), F32),
        ],
        compiler_params=pltpu.CompilerParams(
            dimension_semantics=("parallel", "parallel", "arbitrary"),
            vmem_limit_bytes=VMEM_LIMIT),
        name="diff_attn",
    )(slopes, lvec, proj3, proj3, proj3, lead_proj, lead_proj, qw, kw, sw)


def _gdn_prep_kernel(tq_ref, tk_ref, tv_ref, hq_ref, hk_ref, hv_ref, lq_ref, lk_ref, lv_ref,
                     tg_ref, lg_ref, cwq_ref, cwk_ref, cwv_ref, alog_ref, dtb_ref,
                     u_ref, w_ref, qg_ref, kdt_ref, qk_ref, dec_ref, xs_ref, gs_ref, *, hpb):
    s = pl.program_id(1)
    hp = pl.program_id(2)
    is_lead = s == 0
    width = hpb * HEAD_W
    srcs = ((tq_ref, hq_ref, lq_ref), (tk_ref, hk_ref, lk_ref), (tv_ref, hv_ref, lv_ref))

    @pl.when(is_lead)
    def _():
        rowid = lax.broadcasted_iota(jnp.int32, (CHUNK, width), 0)
        for idx, (_, _, l_ref) in enumerate(srcs):
            cs = slice(idx * width, (idx + 1) * width)
            xs_ref[0:8, cs] = jnp.zeros((8, width), F32)
            xs_ref[8:8 + CHUNK, cs] = jnp.where(rowid >= N_PAD, l_ref[...].astype(F32), 0.0)
        gs_ref[...] = lg_ref[...]

    @pl.when(s == 1)
    def _():
        for idx, (t_ref, _, l_ref) in enumerate(srcs):
            cs = slice(idx * width, (idx + 1) * width)
            xs_ref[0:8, cs] = l_ref[LEAD - 16:LEAD, :].astype(F32)[8:16]
            xs_ref[8:8 + CHUNK, cs] = t_ref[...].astype(F32)
        gs_ref[...] = tg_ref[...]

    @pl.when(s > 1)
    def _():
        for idx, (t_ref, h_ref, _) in enumerate(srcs):
            cs = slice(idx * width, (idx + 1) * width)
            xs_ref[0:8, cs] = h_ref[...].astype(F32)[8:16]
            xs_ref[8:8 + CHUNK, cs] = t_ref[...].astype(F32)
        gs_ref[...] = tg_ref[...]

    rowi = lax.broadcasted_iota(jnp.int32, (CHUNK, CHUNK), 0)
    lanei = lax.broadcasted_iota(jnp.int32, (CHUNK, CHUNK), 1)
    vmask = (rowi >= jnp.where(is_lead, N_PAD, 0)).astype(F32)
    graw = gs_ref[...]
    beta_all = jax.nn.sigmoid(graw) * vmask
    t = graw + dtb_ref[...]
    softplus = jnp.maximum(t, 0.0) + jnp.log1p(jnp.exp(-jnp.abs(t)))
    g_all = -jnp.exp(alog_ref[...]) * softplus * vmask
    incl = rowi >= lanei
    gc_all = _dot(incl.astype(F32), g_all, HIGHEST)
    gc_t = gc_all.T
    eye = (rowi == lanei).astype(F32)

    for hh in range(hpb):
        h = hp * hpb + hh
        hs = slice(hh * HEAD_W, (hh + 1) * HEAD_W)
        beta = jnp.sum(jnp.where(lanei == h, beta_all, 0.0), axis=1, keepdims=True)
        gc = jnp.sum(jnp.where(lanei == HEADS + h, gc_all, 0.0), axis=1, keepdims=True)
        gc_row = jnp.sum(jnp.where(rowi == HEADS + h, gc_t, 0.0), axis=0, keepdims=True)
        g_last = gc_row[:, CHUNK - 1:CHUNK]

        def conv_silu(idx, cw_ref):
            c0 = idx * width + hh * HEAD_W
            y = xs_ref[5:5 + CHUNK, c0:c0 + HEAD_W] * cw_ref[0:1, hs]
            for j in range(1, CONV_K):
                y = y + xs_ref[5 + j:5 + j + CHUNK, c0:c0 + HEAD_W] * cw_ref[j:j + 1, hs]
            return y * jax.nn.sigmoid(y)

        q = conv_silu(0, cwq_ref)
        k = conv_silu(1, cwk_ref)
        v = conv_silu(2, cwv_ref)
        q = q * lax.rsqrt(jnp.sum(q * q, axis=-1, keepdims=True) + EPS) * (HEAD_W ** -0.5)
        k = k * lax.rsqrt(jnp.sum(k * k, axis=-1, keepdims=True) + EPS)

        decay = jnp.where(incl, jnp.exp(jnp.where(incl, gc - gc_row, 0.0)), 0.0)
        kb = k * beta
        k16 = k.astype(BF16)
        lmat = jnp.where(rowi > lanei, _dot_nt(kb.astype(BF16), k16) * decay, 0.0)
        qk = _dot_nt(q.astype(BF16), k16) * decay

        mk = -lmat
        pk = eye + mk
        mk = _dot(mk, mk, HIGHEST)
        for _ in range(5):
            r = _dot(mk, jnp.concatenate([pk, mk], axis=1), HIGHEST)
            pk = pk + r[:, :CHUNK]
            mk = r[:, CHUNK:]
        pk = pk + _dot(mk, pk, HIGHEST)

        uw = _dot(pk, jnp.concatenate([v * beta, kb * jnp.exp(gc)], axis=1), HIGHEST)
        u_ref[:, hs] = uw[:, :HEAD_W].astype(BF16)
        w_ref[:, hs] = uw[:, HEAD_W:].astype(BF16)
        qg_ref[:, hs] = (q * jnp.exp(gc)).astype(BF16)
        kdt_ref[:, hs] = (k * jnp.exp(g_last - gc)).T.astype(BF16)
        qk_ref[:, hs] = qk.astype(BF16)
        dec_ref[hh * 8:(hh + 1) * 8, :] = jnp.broadcast_to(jnp.exp(g_last), (8, HEAD_W))


def _gdn_prep(proj3, lead_proj, gates3, lead_gates, conv_wt, alog_row, dtb_row, hpb):
    b, seq, _ = proj3.shape
    nb = 1 + seq // CHUNK
    width = hpb * HEAD_W
    per_group = GROUP_W // width
    tok = lambda g: pl.BlockSpec(
        (None, CHUNK, width), lambda b_, s, hp: (b_, jnp.maximum(s - 1, 0), g * per_group + hp))
    halo = lambda g: pl.BlockSpec(
        (None, 16, width),
        lambda b_, s, hp: (b_, jnp.maximum((s - 1) * (CHUNK // 16) - 1, 0), g * per_group + hp))
    lead = lambda g: pl.BlockSpec((LEAD, width), lambda b_, s, hp: (0, g * per_group + hp))
    cw = lambda g: pl.BlockSpec((CONV_K, width), lambda b_, s, hp: (0, g * per_group + hp))
    row = pl.BlockSpec((1, HEAD_W), lambda b_, s, hp: (0, 0))
    out = pl.BlockSpec((None, CHUNK, width), lambda b_, s, hp: (b_, s, hp))
    big = jax.ShapeDtypeStruct((b, nb * CHUNK, GROUP_W), BF16)
    return pl.pallas_call(
        functools.partial(_gdn_prep_kernel, hpb=hpb),
        grid=(b, nb, per_group),
        in_specs=[
            tok(3), tok(4), tok(5), halo(3), halo(4), halo(5), lead(3), lead(4), lead(5),
            pl.BlockSpec((None, CHUNK, HEAD_W), lambda b_, s, hp: (b_, jnp.maximum(s - 1, 0), 0)),
            pl.BlockSpec((LEAD, HEAD_W), lambda b_, s, hp: (0, 0)),
            cw(0), cw(1), cw(2), row, row,
        ],
        out_specs=[out, out, out, out, out,
                   pl.BlockSpec((None, None, 8 * hpb, HEAD_W), lambda b_, s, hp: (b_, s, hp, 0))],
        out_shape=[big, big, big, big, big,
                   jax.ShapeDtypeStruct((b, nb, 8 * HEADS, HEAD_W), F32)],
        scratch_shapes=[pltpu.VMEM((8 + CHUNK, 3 * width), F32), pltpu.VMEM((CHUNK, HEAD_W), F32)],
        compiler_params=pltpu.CompilerParams(
            dimension_semantics=("parallel", "parallel", "arbitrary"),
            vmem_limit_bytes=VMEM_LIMIT),
        name="gdn_prep",
    )(proj3, proj3, proj3, proj3, proj3, proj3, lead_proj, lead_proj, lead_proj,
      gates3, lead_gates, conv_wt, conv_wt, conv_wt, alog_row, dtb_row)


def _gdn_seq_kernel(u_ref, w_ref, qg_ref, kdt_ref, qk_ref, dec_ref, z_ref, nw_ref, o_ref, s_ref):
    s = pl.program_id(1)

    @pl.when(s == 0)
    def _():
        s_ref[...] = jnp.zeros_like(s_ref)

    for h in range(HEADS):
        hs = slice(h * HEAD_W, (h + 1) * HEAD_W)
        state = s_ref[h]
        s16 = state.astype(BF16)
        v_new = u_ref[:, hs].astype(F32) - _dot(w_ref[:, hs], s16)
        v16 = v_new.astype(BF16)
        o = _dot(qg_ref[:, hs], s16) + _dot(qk_ref[:, hs], v16)
        s_ref[h] = dec_ref[h * 8:h * 8 + 1, :] * state + _dot(kdt_ref[:, hs], v16)

        @pl.when(s > 0)
        def _():
            z = z_ref[:, hs].astype(F32)
            o_ref[:, hs] = (_rms_rows(o, nw_ref[...]) * (z * jax.nn.sigmoid(z))).astype(o_ref.dtype)


def _gdn_seq(u, w, qg, kdt, qk, dec, proj3, onw):
    b, seq, _ = proj3.shape
    nb = 1 + seq // CHUNK
    blk = pl.BlockSpec((None, CHUNK, GROUP_W), lambda b_, s: (b_, s, 0))
    return pl.pallas_call(
        _gdn_seq_kernel,
        grid=(b, nb),
        in_specs=[
            blk, blk, blk, blk, blk,
            pl.BlockSpec((None, None, 8 * HEADS, HEAD_W), lambda b_, s: (b_, s, 0, 0)),
            pl.BlockSpec((None, CHUNK, GROUP_W), lambda b_, s: (b_, jnp.maximum(s - 1, 0), 6)),
            pl.BlockSpec((1, HEAD_W), lambda b_, s: (0, 0)),
        ],
        out_specs=pl.BlockSpec((None, CHUNK, GROUP_W), lambda b_, s: (b_, jnp.maximum(s - 1, 0), 0)),
        out_shape=jax.ShapeDtypeStruct((b, seq, GROUP_W), BF16),
        scratch_shapes=[pltpu.VMEM((HEADS, HEAD_W, HEAD_W), F32)],
        compiler_params=pltpu.CompilerParams(
            dimension_semantics=("parallel", "arbitrary"), vmem_limit_bytes=VMEM_LIMIT),
        name="gdn_seq",
    )(u, w, qg, kdt, qk, dec, proj3, onw)


def _outproj_kernel(x_ref, oa_ref, od_ref, wa_ref, wd_ref, o_ref):
    o_ref[...] = x_ref[...] + _dot(oa_ref[...], wa_ref[...]) + _dot(od_ref[...], wd_ref[...])


def _outproj(x2d, oa, od, w_out16, tm):
    m = x2d.shape[0]
    return pl.pallas_call(
        _outproj_kernel,
        grid=(m // tm,),
        in_specs=[
            pl.BlockSpec((tm, D_MODEL), lambda i: (i, 0)),
            pl.BlockSpec((tm, GROUP_W), lambda i: (i, 0)),
            pl.BlockSpec((tm, GROUP_W), lambda i: (i, 0)),
            pl.BlockSpec((GROUP_W, D_MODEL), lambda i: (0, 0)),
            pl.BlockSpec((GROUP_W, D_MODEL), lambda i: (1, 0)),
        ],
        out_specs=pl.BlockSpec((tm, D_MODEL), lambda i: (i, 0)),
        out_shape=jax.ShapeDtypeStruct((m, D_MODEL), F32),
        compiler_params=pltpu.CompilerParams(
            dimension_semantics=("parallel",), vmem_limit_bytes=VMEM_LIMIT),
        name="outproj",
    )(x2d, oa, od, w_out16, w_out16)


def _ffn_kernel(h_ref, nw_ref, wg_ref, wu_ref, wd_ref, o_ref, u_ref, *, row_chunk):
    j = pl.program_id(1)

    @pl.when(j == 0)
    def _():
        def body(c, carry):
            r = pl.multiple_of(c * row_chunk, row_chunk)
            x = h_ref[pl.ds(r, row_chunk), :]
            u_ref[pl.ds(r, row_chunk), :] = _rms_rows(x, nw_ref[...]).astype(BF16)
            o_ref[pl.ds(r, row_chunk), :] = x
            return carry

        lax.fori_loop(0, h_ref.shape[0] // row_chunk, body, 0)

    u = u_ref[...]
    g = _dot(u, wg_ref[...])
    a = (g * jax.nn.sigmoid(g) * _dot(u, wu_ref[...])).astype(BF16)
    o_ref[...] += _dot(a, wd_ref[...])


def _ffn(h2d, norm_w, wg, wu, wd, tm, th):
    m = h2d.shape[0]
    return pl.pallas_call(
        functools.partial(_ffn_kernel, row_chunk=min(256, tm)),
        grid=(m // tm, FFN_HIDDEN // th),
        in_specs=[
            pl.BlockSpec((tm, D_MODEL), lambda i, j: (i, 0)),
            pl.BlockSpec((1, D_MODEL), lambda i, j: (0, 0)),
            pl.BlockSpec((D_MODEL, th), lambda i, j: (0, j)),
            pl.BlockSpec((D_MODEL, th), lambda i, j: (0, j)),
            pl.BlockSpec((th, D_MODEL), lambda i, j: (j, 0)),
        ],
        out_specs=pl.BlockSpec((tm, D_MODEL), lambda i, j: (i, 0)),
        out_shape=jax.ShapeDtypeStruct((m, D_MODEL), F32),
        scratch_shapes=[pltpu.VMEM((tm, D_MODEL), BF16)],
        compiler_params=pltpu.CompilerParams(
            dimension_semantics=("parallel", "arbitrary"), vmem_limit_bytes=VMEM_LIMIT),
        name="ffn",
    )(h2d, norm_w, wg, wu, wd)


def kernel(x, meta_tokens, attn_norm_w, w_in, q_norm_w, k_norm_w, lambda_q1, lambda_k1, lambda_q2,
           lambda_k2, subln_w, conv_w, a_log, dt_bias, o_norm_w, w_out, ffn_norm_w, w_gate, w_up,
           w_down):
    b, seq, _ = x.shape
    m = b * seq
    x2d = x.reshape(m, D_MODEL)
    lead = jnp.concatenate([jnp.zeros((N_PAD, D_MODEL), x.dtype), meta_tokens.astype(x.dtype)], 0)

    w_main = w_in[0, :, :MAIN_COLS].astype(BF16)
    w_gates = jnp.pad(w_in[0, :, MAIN_COLS:], ((0, 0), (0, HEAD_W - GATE_COLS))).astype(BF16)
    tm = min(1024, m)
    proj, gates = _inproj(x2d, attn_norm_w, w_main, w_gates, tm, 1024)
    lead_proj, lead_gates = _inproj(lead, attn_norm_w, w_main, w_gates, LEAD, 1024)
    proj3 = proj.reshape(b, seq, MAIN_COLS)

    slopes = 2.0 ** (-8.0 * jnp.arange(1, HEADS + 1, dtype=F32) / HEADS)
    lvec = jnp.concatenate([lambda_q1, lambda_k1, lambda_q2, lambda_k2], 0).astype(F32)
    o_a = _attn(proj3, lead_proj, slopes, lvec, jnp.tile(q_norm_w, (1, 2)),
                jnp.tile(k_norm_w, (1, 2)), subln_w, min(256, seq))

    gate_row = lambda p: jnp.pad(p.astype(F32), ((0, 0), (HEADS, HEAD_W - 2 * HEADS)))
    u, w, qg, kdt, qk, dec = _gdn_prep(
        proj3, lead_proj, gates.reshape(b, seq, HEAD_W), lead_gates, conv_w[0].T,
        gate_row(a_log), gate_row(dt_bias), 2)
    o_d = _gdn_seq(u, w, qg, kdt, qk, dec, proj3, o_norm_w)

    h1 = _outproj(x2d, o_a.reshape(m, GROUP_W), o_d.reshape(m, GROUP_W), w_out[0].astype(BF16),
                  min(512, m))
    out = _ffn(h1, ffn_norm_w, w_gate[0].astype(BF16), w_up[0].astype(BF16),
               w_down[0].astype(BF16), min(512, m), 512)
    return out.reshape(b, seq, D_MODEL)
```

```python
import functools

import jax
import jax.numpy as jnp
import numpy as np
from jax import lax
from jax.experimental import pallas as pl
from jax.experimental.pallas import tpu as pltpu

F32 = jnp.float32
BF16 = jnp.bfloat16
HIGHEST = lax.Precision.HIGHEST

D_MODEL = 2048
N_META = 16
LEAD = 128
N_PAD = LEAD - N_META
HEADS = 8
HEAD_W = 128
QK_DIM = 64
GROUP_W = HEADS * HEAD_W
MAIN_COLS = 7 * GROUP_W
GATE_COLS = 2 * HEADS
CONV_K = 4
FFN_HIDDEN = 5632
EPS = 1e-6
NEG = -1e30
LAMBDA_INIT = 0.2
CHUNK = 128
VMEM_LIMIT = 56 * 1024 * 1024
LOG2E = 1.4426950408889634


def _bf16_pieces(x, n):
    out = []
    for _ in range(n):
        bits = np.array(x, np.float32).view(np.uint32)
        bits = (bits + 0x7FFF + ((bits >> 16) & 1)) & 0xFFFF0000
        p = float(bits.view(np.float32))
        out.append(p)
        x -= p
    return tuple(out)


LOG2E_BF16_PIECES = _bf16_pieces(LOG2E, 3)


def _dot(a, b, precision=None):
    return jnp.dot(a, b, preferred_element_type=F32, precision=precision)


def _dot_nt(a, b):
    return lax.dot_general(a, b, (((1,), (1,)), ((), ())), preferred_element_type=F32)


def _rms_rows(x, w_row):
    return x * lax.rsqrt(jnp.mean(x * x, axis=-1, keepdims=True) + EPS) * w_row


def _inproj_kernel(x_ref, nw_ref, w_ref, wg_ref, o_ref, g_ref, u_ref, *, row_chunk):
    j = pl.program_id(1)

    @pl.when(j == 0)
    def _():
        def body(c, carry):
            r = pl.multiple_of(c * row_chunk, row_chunk)
            u = _rms_rows(x_ref[pl.ds(r, row_chunk), :], nw_ref[...]).astype(BF16)
            u_ref[pl.ds(r, row_chunk), :] = u
            g_ref[pl.ds(r, row_chunk), :] = _dot(u, wg_ref[...])
            return carry

        lax.fori_loop(0, x_ref.shape[0] // row_chunk, body, 0)

    o_ref[...] = _dot(u_ref[...], w_ref[...]).astype(o_ref.dtype)


def _inproj(x2d, norm_w, w_main, w_gate, tm, tn):
    m = x2d.shape[0]
    assert m % tm == 0 and MAIN_COLS % tn == 0
    row_chunk = min(256, tm)
    return pl.pallas_call(
        functools.partial(_inproj_kernel, row_chunk=row_chunk),
        grid=(m // tm, MAIN_COLS // tn),
        in_specs=[
            pl.BlockSpec((tm, D_MODEL), lambda i, j: (i, 0)),
            pl.BlockSpec((1, D_MODEL), lambda i, j: (0, 0)),
            pl.BlockSpec((D_MODEL, tn), lambda i, j: (0, j)),
            pl.BlockSpec((D_MODEL, HEAD_W), lambda i, j: (0, 0)),
        ],
        out_specs=[
            pl.BlockSpec((tm, tn), lambda i, j: (i, j)),
            pl.BlockSpec((tm, HEAD_W), lambda i, j: (i, 0)),
        ],
        out_shape=[
            jax.ShapeDtypeStruct((m, MAIN_COLS), BF16),
            jax.ShapeDtypeStruct((m, HEAD_W), F32),
        ],
        scratch_shapes=[pltpu.VMEM((tm, D_MODEL), BF16)],
        compiler_params=pltpu.CompilerParams(
            dimension_semantics=("parallel", "arbitrary"), vmem_limit_bytes=VMEM_LIMIT),
        name="inproj",
    )(x2d, norm_w, w_main, w_gate)


def _halfnorm(x, w_row):
    lo = lax.broadcasted_iota(jnp.int32, x.shape, 1) < QK_DIM
    x2 = x * x
    s_lo = jnp.sum(jnp.where(lo, x2, 0.0), axis=-1, keepdims=True)
    s_hi = jnp.sum(jnp.where(lo, 0.0, x2), axis=-1, keepdims=True)
    ms = jnp.where(lo, s_lo, s_hi) * (1.0 / QK_DIM)
    return x * lax.rsqrt(ms + EPS) * w_row


def _attn_kernel(slopes_ref, lvec_ref, q_ref, k_ref, v_ref, lk_ref, lv_ref, qw_ref, kw_ref,
                 swc_ref, o_ref, kn_ref, vt_ref, lkn_ref, lvt_ref, kaug_ref, acc_ref, sa_ref,
                 sb_ref, *, tq, seq):
    h = pl.program_id(1)
    i = pl.program_id(2)
    slope = slopes_ref[h]

    @pl.when(i == 0)
    def _():
        def body(c, carry):
            r = pl.multiple_of(c * 256, 256)
            kn_ref[pl.ds(r, 256), :] = _halfnorm(
                k_ref[pl.ds(r, 256), :].astype(F32), kw_ref[...]).astype(BF16)
            vt_ref[:, pl.ds(r, 256)] = v_ref[pl.ds(r, 256), :].astype(F32).T.astype(BF16)
            return carry

        lax.fori_loop(0, seq // 256, body, 0)
        lkn_ref[...] = _halfnorm(lk_ref[...].astype(F32), kw_ref[...]).astype(BF16)
        lvt_ref[...] = lv_ref[...].astype(F32).T.astype(BF16)
        kk = lax.broadcasted_iota(jnp.int32, (tq, HEAD_W), 0)
        ln = lax.broadcasted_iota(jnp.int32, (tq, HEAD_W), 1)
        hi = ((kk // 16) * 16).astype(F32)
        lo_ = (kk % 16).astype(F32)
        kaug_ref[...] = (slope * jnp.where(ln < 3, hi, jnp.where(ln < 6, lo_, 0.0))).astype(BF16)

    lane = lax.broadcasted_iota(jnp.int32, (tq, HEAD_W), 1)
    lo = lane < QK_DIM
    qn = _halfnorm(q_ref[...].astype(F32), qw_ref[...]) * (QK_DIM ** -0.5 * LOG2E)
    sub = lax.broadcasted_iota(jnp.int32, (HEAD_W, tq), 0)
    aug = jnp.zeros((HEAD_W, tq), F32)
    for n, piece in enumerate(LOG2E_BF16_PIECES):
        aug = jnp.where((sub == n) | (sub == n + 3), piece, aug)
    ws = tuple(jnp.concatenate([x.T, aug], axis=0).astype(BF16)
               for x in (jnp.where(lo, qn, 0.0), jnp.where(lo, 0.0, qn)))

    slope2 = slope * LOG2E
    q_off = slope2 * lax.broadcasted_iota(jnp.int32, (1, tq), 1).astype(F32)
    key_ok = lax.broadcasted_iota(jnp.int32, (LEAD, tq), 0) >= N_PAD
    ms, ls = [], []
    for mp in range(2):
        s = jnp.where(key_ok, _dot(lkn_ref[...], ws[mp][:HEAD_W]) + q_off, NEG)
        m = jnp.max(s, axis=0, keepdims=True)
        p = jnp.exp2(s - m)
        ms.append(m)
        ls.append(jnp.sum(p, axis=0, keepdims=True))
        acc_ref[mp] = _dot(lvt_ref[...], p.astype(BF16))

    key_i = lax.broadcasted_iota(jnp.int32, (tq, tq), 0)
    qry_i = lax.broadcasted_iota(jnp.int32, (tq, tq), 1)

    def scores(j, dst_ref, diag):
        r = pl.multiple_of(j * tq, tq)
        lhs = jnp.concatenate([kn_ref[pl.ds(r, tq), :], kaug_ref[...]], axis=1)
        bms = []
        for mp in range(2):
            raw = _dot(lhs, ws[mp])
            if diag:
                raw = jnp.where(key_i <= qry_i, raw, NEG)
            dst_ref[mp] = raw
            bms.append(jnp.max(raw, axis=0, keepdims=True))
        return tuple(bms)

    def accumulate(j, src_ref, bms, carry):
        r = pl.multiple_of(j * tq, tq)
        vt = vt_ref[:, pl.ds(r, tq)]
        c = slope2 * ((j - i) * tq).astype(F32)
        out = []
        for mp in range(2):
            m, l = carry[2 * mp], carry[2 * mp + 1]
            m_new = jnp.maximum(m, bms[mp] + c)
            alpha = jnp.exp2(m - m_new)
            p = jnp.exp2(src_ref[mp] - (m_new - c))
            out += [m_new, alpha * l + jnp.sum(p, axis=0, keepdims=True)]
            acc_ref[mp] = alpha * acc_ref[mp] + _dot(vt, p.astype(BF16))
        return tuple(out)

    carry = (ms[0], ls[0], ms[1], ls[1])
    bm_a = scores(i, sa_ref, True)

    def pair(t, state):
        j_a, bm_a, carry = state
        bm_b = scores(2 * t, sb_ref, False)
        carry = accumulate(j_a, sa_ref, bm_a, carry)
        bm_a = scores(2 * t + 1, sa_ref, False)
        carry = accumulate(2 * t, sb_ref, bm_b, carry)
        return 2 * t + 1, bm_a, carry

    j_a, bm_a, carry = lax.fori_loop(0, i // 2, pair, (i, bm_a, carry))

    def odd_tail(carry):
        bm_b = scores(i - 1, sb_ref, False)
        carry = accumulate(j_a, sa_ref, bm_a, carry)
        return accumulate(i - 1, sb_ref, bm_b, carry)

    carry = lax.cond(i % 2 == 1, odd_tail, lambda c: accumulate(j_a, sa_ref, bm_a, c), carry)

    lv4 = lvec_ref[...]
    lam = (jnp.exp(jnp.sum(lv4[0:1] * lv4[1:2], axis=-1, keepdims=True))
           - jnp.exp(jnp.sum(lv4[2:3] * lv4[3:4], axis=-1, keepdims=True)) + LAMBDA_INIT)
    o = acc_ref[0] / carry[1] - lam * (acc_ref[1] / carry[3])
    o = o * lax.rsqrt(jnp.mean(o * o, axis=0, keepdims=True) + EPS) * swc_ref[...]
    o_ref[...] = (o * (1.0 - LAMBDA_INIT)).T.astype(o_ref.dtype)


def _attn(proj3, lead_proj, slopes, lvec, qw, kw, sw, tq):
    b, seq, _ = proj3.shape
    assert seq % tq == 0 and seq % 256 == 0 and tq % 16 == 0 and tq <= 512
    return pl.pallas_call(
        functools.partial(_attn_kernel, tq=tq, seq=seq),
        grid=(b, HEADS, seq // tq),
        in_specs=[
            pl.BlockSpec(memory_space=pltpu.SMEM),
            pl.BlockSpec((4, QK_DIM), lambda b_, h, i: (0, 0)),
            pl.BlockSpec((None, tq, HEAD_W), lambda b_, h, i: (b_, i, h)),
            pl.BlockSpec((None, seq, HEAD_W), lambda b_, h, i: (b_, 0, HEADS + h)),
            pl.BlockSpec((None, seq, HEAD_W), lambda b_, h, i: (b_, 0, 2 * HEADS + h)),
            pl.BlockSpec((LEAD, HEAD_W), lambda b_, h, i: (0, HEADS + h)),
            pl.BlockSpec((LEAD, HEAD_W), lambda b_, h, i: (0, 2 * HEADS + h)),
            pl.BlockSpec((1, HEAD_W), lambda b_, h, i: (0, 0)),
            pl.BlockSpec((1, HEAD_W), lambda b_, h, i: (0, 0)),
            pl.BlockSpec((HEAD_W, 1), lambda b_, h, i: (0, 0)),
        ],
        out_specs=pl.BlockSpec((None, tq, HEAD_W), lambda b_, h, i: (b_, i, h)),
        out_shape=jax.ShapeDtypeStruct((b, seq, GROUP_W), BF16),
        scratch_shapes=[
            pltpu.VMEM((seq, HEAD_W), BF16),
            pltpu.VMEM((HEAD_W, seq), BF16),
            pltpu.VMEM((LEAD, HEAD_W), BF16),
            pltpu.VMEM((HEAD_W, LEAD), BF16),
            pltpu.VMEM((tq, HEAD_W), BF16),
            pltpu.VMEM((2, HEAD_W, tq), F32),
            pltpu.VMEM((2, tq, tq), F32),
            pltpu.VMEM((2, tq, tq), F32),
        ],
        compiler_params=pltpu.CompilerParams(
            dimension_semantics=("parallel", "parallel", "arbitrary"),
            vmem_limit_bytes=VMEM_LIMIT),
        name="diff_attn",
    )(slopes, lvec, proj3, proj3, proj3, lead_proj, lead_proj, qw, kw, sw)


def _gdn_prep_kernel(tq_ref, tk_ref, tv_ref, hq_ref, hk_ref, hv_ref, lq_ref, lk_ref, lv_ref,
                     tg_ref, lg_ref, cwq_ref, cwk_ref, cwv_ref, alog_ref, dtb_ref,
                     u_ref, w_ref, qg_ref, kdt_ref, qk_ref, dec_ref, xs_ref, gs_ref, *, hpb):
    s = pl.program_id(1)
    hp = pl.program_id(2)
    is_lead = s == 0
    width = hpb * HEAD_W
    srcs = ((tq_ref, hq_ref, lq_ref), (tk_ref, hk_ref, lk_ref), (tv_ref, hv_ref, lv_ref))

    @pl.when(is_lead)
    def _():
        rowid = lax.broadcasted_iota(jnp.int32, (CHUNK, width), 0)
        for idx, (_, _, l_ref) in enumerate(srcs):
            cs = slice(idx * width, (idx + 1) * width)
            xs_ref[0:8, cs] = jnp.zeros((8, width), F32)
            xs_ref[8:8 + CHUNK, cs] = jnp.where(rowid >= N_PAD, l_ref[...].astype(F32), 0.0)
        gs_ref[...] = lg_ref[...]

    @pl.when(s == 1)
    def _():
        for idx, (t_ref, _, l_ref) in enumerate(srcs):
            cs = slice(idx * width, (idx + 1) * width)
            xs_ref[0:8, cs] = l_ref[LEAD - 16:LEAD, :].astype(F32)[8:16]
            xs_ref[8:8 + CHUNK, cs] = t_ref[...].astype(F32)
        gs_ref[...] = tg_ref[...]

    @pl.when(s > 1)
    def _():
        for idx, (t_ref, h_ref, _) in enumerate(srcs):
            cs = slice(idx * width, (idx + 1) * width)
            xs_ref[0:8, cs] = h_ref[...].astype(F32)[8:16]
            xs_ref[8:8 + CHUNK, cs] = t_ref[...].astype(F32)
        gs_ref[...] = tg_ref[...]

    rowi = lax.broadcasted_iota(jnp.int32, (CHUNK, CHUNK), 0)
    lanei = lax.broadcasted_iota(jnp.int32, (CHUNK, CHUNK), 1)
    vmask = (rowi >= jnp.where(is_lead, N_PAD, 0)).astype(F32)
    graw = gs_ref[...]
    beta_all = jax.nn.sigmoid(graw) * vmask
    t = graw + dtb_ref[...]
    softplus = jnp.maximum(t, 0.0) + jnp.log1p(jnp.exp(-jnp.abs(t)))
    g_all = -jnp.exp(alog_ref[...]) * softplus * vmask
    incl = rowi >= lanei
    gc_all = _dot(incl.astype(F32), g_all, HIGHEST)
    gc_t = gc_all.T
    eye = (rowi == lanei).astype(F32)

    for hh in range(hpb):
        h = hp * hpb + hh
        hs = slice(hh * HEAD_W, (hh + 1) * HEAD_W)
        beta = jnp.sum(jnp.where(lanei == h, beta_all, 0.0), axis=1, keepdims=True)
        gc = jnp.sum(jnp.where(lanei == HEADS + h, gc_all, 0.0), axis=1, keepdims=True)
        gc_row = jnp.sum(jnp.where(rowi == HEADS + h, gc_t, 0.0), axis=0, keepdims=True)
        g_last = gc_row[:, CHUNK - 1:CHUNK]

        def conv_silu(idx, cw_ref):
            c0 = idx * width + hh * HEAD_W
            y = xs_ref[5:5 + CHUNK, c0:c0 + HEAD_W] * cw_ref[0:1, hs]
            for j in range(1, CONV_K):
                y = y + xs_ref[5 + j:5 + j + CHUNK, c0:c0 + HEAD_W] * cw_ref[j:j + 1, hs]
            return y * jax.nn.sigmoid(y)

        q = conv_silu(0, cwq_ref)
        k = conv_silu(1, cwk_ref)
        v = conv_silu(2, cwv_ref)
        q = q * lax.rsqrt(jnp.sum(q * q, axis=-1, keepdims=True) + EPS) * (HEAD_W ** -0.5)
        k = k * lax.rsqrt(jnp.sum(k * k, axis=-1, keepdims=True) + EPS)

        decay = jnp.where(incl, jnp.exp(jnp.where(incl, gc - gc_row, 0.0)), 0.0)
        kb = k * beta
        k16 = k.astype(BF16)
        lmat = jnp.where(rowi > lanei, _dot_nt(kb.astype(BF16), k16) * decay, 0.0)
        qk = _dot_nt(q.astype(BF16), k16) * decay

        mk = -lmat
        pk = eye + mk
        mk = _dot(mk, mk, HIGHEST)
        for _ in range(5):
            r = _dot(mk, jnp.concatenate([pk, mk], axis=1), HIGHEST)
            pk = pk + r[:, :CHUNK]
            mk = r[:, CHUNK:]
        pk = pk + _dot(mk, pk, HIGHEST)

        uw = _dot(pk, jnp.concatenate([v * beta, kb * jnp.exp(gc)], axis=1), HIGHEST)
        u_ref[:, hs] = uw[:, :HEAD_W].astype(BF16)
        w_ref[:, hs] = uw[:, HEAD_W:].astype(BF16)
        qg_ref[:, hs] = (q * jnp.exp(gc)).astype(BF16)
        kdt_ref[:, hs] = (k * jnp.exp(g_last - gc)).T.astype(BF16)
        qk_ref[:, hs] = qk.astype(BF16)
        dec_ref[hh * 8:(hh + 1) * 8, :] = jnp.broadcast_to(jnp.exp(g_last), (8, HEAD_W))


def _gdn_prep(proj3, lead_proj, gates3, lead_gates, conv_wt, alog_row, dtb_row, hpb):
    b, seq, _ = proj3.shape
    nb = 1 + seq // CHUNK
    width = hpb * HEAD_W
    per_group = GROUP_W // width
    tok = lambda g: pl.BlockSpec(
        (None, CHUNK, width), lambda b_, s, hp: (b_, jnp.maximum(s - 1, 0), g * per_group + hp))
    halo = lambda g: pl.BlockSpec(
        (None, 16, width),
        lambda b_, s, hp: (b_, jnp.maximum((s - 1) * (CHUNK // 16) - 1, 0), g * per_group + hp))
    lead = lambda g: pl.BlockSpec((LEAD, width), lambda b_, s, hp: (0, g * per_group + hp))
    cw = lambda g: pl.BlockSpec((CONV_K, width), lambda b_, s, hp: (0, g * per_group + hp))
    row = pl.BlockSpec((1, HEAD_W), lambda b_, s, hp: (0, 0))
    out = pl.BlockSpec((None, CHUNK, width), lambda b_, s, hp: (b_, s, hp))
    big = jax.ShapeDtypeStruct((b, nb * CHUNK, GROUP_W), BF16)
    return pl.pallas_call(
        functools.partial(_gdn_prep_kernel, hpb=hpb),
        grid=(b, nb, per_group),
        in_specs=[
            tok(3), tok(4), tok(5), halo(3), halo(4), halo(5), lead(3), lead(4), lead(5),
            pl.BlockSpec((None, CHUNK, HEAD_W), lambda b_, s, hp: (b_, jnp.maximum(s - 1, 0), 0)),
            pl.BlockSpec((LEAD, HEAD_W), lambda b_, s, hp: (0, 0)),
            cw(0), cw(1), cw(2), row, row,
        ],
        out_specs=[out, out, out, out, out,
                   pl.BlockSpec((None, None, 8 * hpb, HEAD_W), lambda b_, s, hp: (b_, s, hp, 0))],
        out_shape=[big, big, big, big, big,
                   jax.ShapeDtypeStruct((b, nb, 8 * HEADS, HEAD_W), F32)],
        scratch_shapes=[pltpu.VMEM((8 + CHUNK, 3 * width), F32), pltpu.VMEM((CHUNK, HEAD_W), F32)],
        compiler_params=pltpu.CompilerParams(
            dimension_semantics=("parallel", "parallel", "arbitrary"),
            vmem_limit_bytes=VMEM_LIMIT),
        name="gdn_prep",
    )(proj3, proj3, proj3, proj3, proj3, proj3, lead_proj, lead_proj, lead_proj,
      gates3, lead_gates, conv_wt, conv_wt, conv_wt, alog_row, dtb_row)


def _gdn_seq_kernel(u_ref, w_ref, qg_ref, kdt_ref, qk_ref, dec_ref, z_ref, nw_ref, o_ref, s_ref):
    s = pl.program_id(1)

    @pl.when(s == 0)
    def _():
        s_ref[...] = jnp.zeros_like(s_ref)

    for h in range(HEADS):
        hs = slice(h * HEAD_W, (h + 1) * HEAD_W)
        state = s_ref[h]
        s16 = state.astype(BF16)
        v_new = u_ref[:, hs].astype(F32) - _dot(w_ref[:, hs], s16)
        v16 = v_new.astype(BF16)
        o = _dot(qg_ref[:, hs], s16) + _dot(qk_ref[:, hs], v16)
        s_ref[h] = dec_ref[h * 8:h * 8 + 1, :] * state + _dot(kdt_ref[:, hs], v16)

        @pl.when(s > 0)
        def _():
            z = z_ref[:, hs].astype(F32)
            o_ref[:, hs] = (_rms_rows(o, nw_ref[...]) * (z * jax.nn.sigmoid(z))).astype(o_ref.dtype)


def _gdn_seq(u, w, qg, kdt, qk, dec, proj3, onw):
    b, seq, _ = proj3.shape
    nb = 1 + seq // CHUNK
    blk = pl.BlockSpec((None, CHUNK, GROUP_W), lambda b_, s: (b_, s, 0))
    return pl.pallas_call(
        _gdn_seq_kernel,
        grid=(b, nb),
        in_specs=[
            blk, blk, blk, blk, blk,
            pl.BlockSpec((None, None, 8 * HEADS, HEAD_W), lambda b_, s: (b_, s, 0, 0)),
            pl.BlockSpec((None, CHUNK, GROUP_W), lambda b_, s: (b_, jnp.maximum(s - 1, 0), 6)),
            pl.BlockSpec((1, HEAD_W), lambda b_, s: (0, 0)),
        ],
        out_specs=pl.BlockSpec((None, CHUNK, GROUP_W), lambda b_, s: (b_, jnp.maximum(s - 1, 0), 0)),
        out_shape=jax.ShapeDtypeStruct((b, seq, GROUP_W), BF16),
        scratch_shapes=[pltpu.VMEM((HEADS, HEAD_W, HEAD_W), F32)],
        compiler_params=pltpu.CompilerParams(
            dimension_semantics=("parallel", "arbitrary"), vmem_limit_bytes=VMEM_LIMIT),
        name="gdn_seq",
    )(u, w, qg, kdt, qk, dec, proj3, onw)


def _outproj_kernel(x_ref, oa_ref, od_ref, wa_ref, wd_ref, o_ref):
    o_ref[...] = x_ref[...] + _dot(oa_ref[...], wa_ref[...]) + _dot(od_ref[...], wd_ref[...])


def _outproj(x2d, oa, od, w_out16, tm):
    m = x2d.shape[0]
    assert m % tm == 0
    return pl.pallas_call(
        _outproj_kernel,
        grid=(m // tm,),
        in_specs=[
            pl.BlockSpec((tm, D_MODEL), lambda i: (i, 0)),
            pl.BlockSpec((tm, GROUP_W), lambda i: (i, 0)),
            pl.BlockSpec((tm, GROUP_W), lambda i: (i, 0)),
            pl.BlockSpec((GROUP_W, D_MODEL), lambda i: (0, 0)),
            pl.BlockSpec((GROUP_W, D_MODEL), lambda i: (1, 0)),
        ],
        out_specs=pl.BlockSpec((tm, D_MODEL), lambda i: (i, 0)),
        out_shape=jax.ShapeDtypeStruct((m, D_MODEL), F32),
        compiler_params=pltpu.CompilerParams(
            dimension_semantics=("parallel",), vmem_limit_bytes=VMEM_LIMIT),
        name="outproj",
    )(x2d, oa, od, w_out16, w_out16)


def _ffn_kernel(h_ref, nw_ref, wg_ref, wu_ref, wd_ref, o_ref, u_ref, *, row_chunk):
    j = pl.program_id(1)

    @pl.when(j == 0)
    def _():
        def body(c, carry):
            r = pl.multiple_of(c * row_chunk, row_chunk)
            x = h_ref[pl.ds(r, row_chunk), :]
            u_ref[pl.ds(r, row_chunk), :] = _rms_rows(x, nw_ref[...]).astype(BF16)
            o_ref[pl.ds(r, row_chunk), :] = x
            return carry

        lax.fori_loop(0, h_ref.shape[0] // row_chunk, body, 0)

    u = u_ref[...]
    g = _dot(u, wg_ref[...])
    a = (g * jax.nn.sigmoid(g) * _dot(u, wu_ref[...])).astype(BF16)
    o_ref[...] += _dot(a, wd_ref[...])


def _ffn(h2d, norm_w, wg, wu, wd, tm, th):
    m = h2d.shape[0]
    assert m % tm == 0 and FFN_HIDDEN % th == 0
    return pl.pallas_call(
        functools.partial(_ffn_kernel, row_chunk=min(256, tm)),
        grid=(m // tm, FFN_HIDDEN // th),
        in_specs=[
            pl.BlockSpec((tm, D_MODEL), lambda i, j: (i, 0)),
            pl.BlockSpec((1, D_MODEL), lambda i, j: (0, 0)),
            pl.BlockSpec((D_MODEL, th), lambda i, j: (0, j)),
            pl.BlockSpec((D_MODEL, th), lambda i, j: (0, j)),
            pl.BlockSpec((th, D_MODEL), lambda i, j: (j, 0)),
        ],
        out_specs=pl.BlockSpec((tm, D_MODEL), lambda i, j: (i, 0)),
        out_shape=jax.ShapeDtypeStruct((m, D_MODEL), F32),
        scratch_shapes=[pltpu.VMEM((tm, D_MODEL), BF16)],
        compiler_params=pltpu.CompilerParams(
            dimension_semantics=("parallel", "arbitrary"), vmem_limit_bytes=VMEM_LIMIT),
        name="ffn",
    )(h2d, norm_w, wg, wu, wd)


def kernel(x, meta_tokens, attn_norm_w, w_in, q_norm_w, k_norm_w, lambda_q1, lambda_k1, lambda_q2,
           lambda_k2, subln_w, conv_w, a_log, dt_bias, o_norm_w, w_out, ffn_norm_w, w_gate, w_up,
           w_down):
    b, seq, _ = x.shape
    m = b * seq
    x2d = x.reshape(m, D_MODEL)
    lead = jnp.concatenate([jnp.zeros((N_PAD, D_MODEL), x.dtype), meta_tokens.astype(x.dtype)], 0)

    w_main = w_in[0, :, :MAIN_COLS].astype(BF16)
    w_gates = jnp.pad(w_in[0, :, MAIN_COLS:], ((0, 0), (0, HEAD_W - GATE_COLS))).astype(BF16)
    tm = min(1024, m)
    proj, gates = _inproj(x2d, attn_norm_w, w_main, w_gates, tm, 1024)
    lead_proj, lead_gates = _inproj(lead, attn_norm_w, w_main, w_gates, LEAD, 1024)
    proj3 = proj.reshape(b, seq, MAIN_COLS)

    slopes = 2.0 ** (-8.0 * jnp.arange(1, HEADS + 1, dtype=F32) / HEADS)
    lvec = jnp.concatenate([lambda_q1, lambda_k1, lambda_q2, lambda_k2], 0).astype(F32)
    o_a = _attn(proj3, lead_proj, slopes, lvec, jnp.tile(q_norm_w, (1, 2)),
                jnp.tile(k_norm_w, (1, 2)), subln_w.reshape(HEAD_W, 1), min(512, seq))

    gate_row = lambda p: jnp.pad(p.astype(F32), ((0, 0), (HEADS, HEAD_W - 2 * HEADS)))
    u, w, qg, kdt, qk, dec = _gdn_prep(
        proj3, lead_proj, gates.reshape(b, seq, HEAD_W), lead_gates, conv_w[0].T,
        gate_row(a_log), gate_row(dt_bias), 2)
    o_d = _gdn_seq(u, w, qg, kdt, qk, dec, proj3, o_norm_w)

    h1 = _outproj(x2d, o_a.reshape(m, GROUP_W), o_d.reshape(m, GROUP_W), w_out[0].astype(BF16),
                  min(512, m))
    out = _ffn(h1, ffn_norm_w, w_gate[0].astype(BF16), w_up[0].astype(BF16),
               w_down[0].astype(BF16), min(512, m), 512)
    return out.reshape(b, seq, D_MODEL)
```

```python
import functools

import jax
import jax.numpy as jnp
import numpy as np
from jax import lax
from jax.experimental import pallas as pl
from jax.experimental.pallas import tpu as pltpu

F32 = jnp.float32
BF16 = jnp.bfloat16
HIGHEST = lax.Precision.HIGHEST

D_MODEL = 2048
N_META = 16
LEAD = 128
N_PAD = LEAD - N_META
HEADS = 8
HEAD_W = 128
QK_DIM = 64
GROUP_W = HEADS * HEAD_W
MAIN_COLS = 7 * GROUP_W
GATE_COLS = 2 * HEADS
CONV_K = 4
FFN_HIDDEN = 5632
EPS = 1e-6
NEG = -1e30
LAMBDA_INIT = 0.2
CHUNK = 128
VMEM_LIMIT = 56 * 1024 * 1024
LOG2E = 1.4426950408889634


def _bf16_pieces(x, n):
    out = []
    for _ in range(n):
        bits = np.array(x, np.float32).view(np.uint32)
        bits = (bits + 0x7FFF + ((bits >> 16) & 1)) & 0xFFFF0000
        p = float(bits.view(np.float32))
        out.append(p)
        x -= p
    return tuple(out)


LOG2E_BF16_PIECES = _bf16_pieces(LOG2E, 3)


def _dot(a, b, precision=None):
    return jnp.dot(a, b, preferred_element_type=F32, precision=precision)


def _dot_nt(a, b):
    return lax.dot_general(a, b, (((1,), (1,)), ((), ())), preferred_element_type=F32)


def _split(x):
    hi = x.astype(BF16)
    return hi, (x - hi.astype(F32)).astype(BF16)


def _dot3(a, b):
    a_hi, a_lo = a
    if isinstance(b, list):
        b_hi = jnp.concatenate([x[0] for x in b], axis=1)
        b_lo = jnp.concatenate([x[1] for x in b], axis=1)
    else:
        b_hi, b_lo = b
    rows = a_hi.shape[0]
    r = _dot(jnp.concatenate([a_hi, a_lo], axis=0), b_hi)
    return r[:rows] + r[rows:] + _dot(a_hi, b_lo)


def _rms_rows(x, w_row):
    return x * lax.rsqrt(jnp.mean(x * x, axis=-1, keepdims=True) + EPS) * w_row


def _inproj_kernel(x_ref, nw_ref, w_ref, wg_ref, o_ref, g_ref, u_ref, *, row_chunk):
    j = pl.program_id(1)

    @pl.when(j == 0)
    def _():
        def body(c, carry):
            r = pl.multiple_of(c * row_chunk, row_chunk)
            u = _rms_rows(x_ref[pl.ds(r, row_chunk), :], nw_ref[...]).astype(BF16)
            u_ref[pl.ds(r, row_chunk), :] = u
            g_ref[pl.ds(r, row_chunk), :] = _dot(u, wg_ref[...])
            return carry

        lax.fori_loop(0, x_ref.shape[0] // row_chunk, body, 0)

    o_ref[...] = _dot(u_ref[...], w_ref[...]).astype(o_ref.dtype)


def _inproj(x2d, norm_w, w_main, w_gate, tm, tn):
    m = x2d.shape[0]
    assert m % tm == 0 and MAIN_COLS % tn == 0
    row_chunk = min(256, tm)
    return pl.pallas_call(
        functools.partial(_inproj_kernel, row_chunk=row_chunk),
        grid=(m // tm, MAIN_COLS // tn),
        in_specs=[
            pl.BlockSpec((tm, D_MODEL), lambda i, j: (i, 0)),
            pl.BlockSpec((1, D_MODEL), lambda i, j: (0, 0)),
            pl.BlockSpec((D_MODEL, tn), lambda i, j: (0, j)),
            pl.BlockSpec((D_MODEL, HEAD_W), lambda i, j: (0, 0)),
        ],
        out_specs=[
            pl.BlockSpec((tm, tn), lambda i, j: (i, j)),
            pl.BlockSpec((tm, HEAD_W), lambda i, j: (i, 0)),
        ],
        out_shape=[
            jax.ShapeDtypeStruct((m, MAIN_COLS), BF16),
            jax.ShapeDtypeStruct((m, HEAD_W), F32),
        ],
        scratch_shapes=[pltpu.VMEM((tm, D_MODEL), BF16)],
        compiler_params=pltpu.CompilerParams(
            dimension_semantics=("parallel", "arbitrary"), vmem_limit_bytes=VMEM_LIMIT),
        name="inproj",
    )(x2d, norm_w, w_main, w_gate)


def _halfnorm(x, w_row):
    lo = lax.broadcasted_iota(jnp.int32, x.shape, 1) < QK_DIM
    x2 = x * x
    s_lo = jnp.sum(jnp.where(lo, x2, 0.0), axis=-1, keepdims=True)
    s_hi = jnp.sum(jnp.where(lo, 0.0, x2), axis=-1, keepdims=True)
    ms = jnp.where(lo, s_lo, s_hi) * (1.0 / QK_DIM)
    return x * lax.rsqrt(ms + EPS) * w_row


def _attn_kernel(slopes_ref, lvec_ref, q_ref, k_ref, v_ref, lk_ref, lv_ref, qw_ref, kw_ref,
                 swc_ref, o_ref, kn_ref, vt_ref, lkn_ref, lvt_ref, kaug_ref, acc_ref, sa_ref,
                 sb_ref, *, tq, seq):
    h = pl.program_id(1)
    i = pl.program_id(2)
    slope = slopes_ref[h]

    @pl.when(i == 0)
    def _():
        def body(c, carry):
            r = pl.multiple_of(c * 256, 256)
            kn_ref[pl.ds(r, 256), :] = _halfnorm(
                k_ref[pl.ds(r, 256), :].astype(F32), kw_ref[...]).astype(BF16)
            vt_ref[:, pl.ds(r, 256)] = v_ref[pl.ds(r, 256), :].astype(F32).T.astype(BF16)
            return carry

        lax.fori_loop(0, seq // 256, body, 0)
        lkn_ref[...] = _halfnorm(lk_ref[...].astype(F32), kw_ref[...]).astype(BF16)
        lvt_ref[...] = lv_ref[...].astype(F32).T.astype(BF16)
        kk = lax.broadcasted_iota(jnp.int32, (tq, HEAD_W), 0)
        ln = lax.broadcasted_iota(jnp.int32, (tq, HEAD_W), 1)
        hi = ((kk // 16) * 16).astype(F32)
        lo_ = (kk % 16).astype(F32)
        kaug_ref[...] = (slope * jnp.where(ln < 3, hi, jnp.where(ln < 6, lo_, 0.0))).astype(BF16)

    lane = lax.broadcasted_iota(jnp.int32, (tq, HEAD_W), 1)
    lo = lane < QK_DIM
    qn = _halfnorm(q_ref[...].astype(F32), qw_ref[...]) * (QK_DIM ** -0.5 * LOG2E)
    sub = lax.broadcasted_iota(jnp.int32, (HEAD_W, tq), 0)
    aug = jnp.zeros((HEAD_W, tq), F32)
    for n, piece in enumerate(LOG2E_BF16_PIECES):
        aug = jnp.where((sub == n) | (sub == n + 3), piece, aug)
    ws = tuple(jnp.concatenate([x.T, aug], axis=0).astype(BF16)
               for x in (jnp.where(lo, qn, 0.0), jnp.where(lo, 0.0, qn)))

    slope2 = slope * LOG2E
    q_off = slope2 * lax.broadcasted_iota(jnp.int32, (1, tq), 1).astype(F32)
    key_ok = lax.broadcasted_iota(jnp.int32, (LEAD, tq), 0) >= N_PAD
    ms, ls = [], []
    for mp in range(2):
        s = jnp.where(key_ok, _dot(lkn_ref[...], ws[mp][:HEAD_W]) + q_off, NEG)
        m = jnp.max(s, axis=0, keepdims=True)
        p = jnp.exp2(s - m)
        ms.append(m)
        ls.append(jnp.sum(p, axis=0, keepdims=True))
        acc_ref[mp] = _dot(lvt_ref[...], p.astype(BF16))

    key_i = lax.broadcasted_iota(jnp.int32, (tq, tq), 0)
    qry_i = lax.broadcasted_iota(jnp.int32, (tq, tq), 1)

    def scores(j, dst_ref, diag):
        r = pl.multiple_of(j * tq, tq)
        lhs = jnp.concatenate([kn_ref[pl.ds(r, tq), :], kaug_ref[...]], axis=1)
        bms = []
        for mp in range(2):
            raw = _dot(lhs, ws[mp])
            if diag:
                raw = jnp.where(key_i <= qry_i, raw, NEG)
            dst_ref[mp] = raw
            bms.append(jnp.max(raw, axis=0, keepdims=True))
        return tuple(bms)

    def accumulate(j, src_ref, bms, carry):
        r = pl.multiple_of(j * tq, tq)
        vt = vt_ref[:, pl.ds(r, tq)]
        c = slope2 * ((j - i) * tq).astype(F32)
        out = []
        for mp in range(2):
            m, l = carry[2 * mp], carry[2 * mp + 1]
            m_new = jnp.maximum(m, bms[mp] + c)
            alpha = jnp.exp2(m - m_new)
            p = jnp.exp2(src_ref[mp] - (m_new - c))
            out += [m_new, alpha * l + jnp.sum(p, axis=0, keepdims=True)]
            acc_ref[mp] = alpha * acc_ref[mp] + _dot(vt, p.astype(BF16))
        return tuple(out)

    carry = (ms[0], ls[0], ms[1], ls[1])
    bm_a = scores(i, sa_ref, True)

    def pair(t, state):
        j_a, bm_a, carry = state
        bm_b = scores(2 * t, sb_ref, False)
        carry = accumulate(j_a, sa_ref, bm_a, carry)
        bm_a = scores(2 * t + 1, sa_ref, False)
        carry = accumulate(2 * t, sb_ref, bm_b, carry)
        return 2 * t + 1, bm_a, carry

    j_a, bm_a, carry = lax.fori_loop(0, i // 2, pair, (i, bm_a, carry))

    def odd_tail(carry):
        bm_b = scores(i - 1, sb_ref, False)
        carry = accumulate(j_a, sa_ref, bm_a, carry)
        return accumulate(i - 1, sb_ref, bm_b, carry)

    carry = lax.cond(i % 2 == 1, odd_tail, lambda c: accumulate(j_a, sa_ref, bm_a, c), carry)

    lv4 = lvec_ref[...]
    lam = (jnp.exp(jnp.sum(lv4[0:1] * lv4[1:2], axis=-1, keepdims=True))
           - jnp.exp(jnp.sum(lv4[2:3] * lv4[3:4], axis=-1, keepdims=True)) + LAMBDA_INIT)
    o = acc_ref[0] / carry[1] - lam * (acc_ref[1] / carry[3])
    o = o * lax.rsqrt(jnp.mean(o * o, axis=0, keepdims=True) + EPS) * swc_ref[...]
    o_ref[...] = (o * (1.0 - LAMBDA_INIT)).T.astype(o_ref.dtype)


def _attn(proj3, lead_proj, slopes, lvec, qw, kw, sw, tq):
    b, seq, _ = proj3.shape
    assert seq % tq == 0 and seq % 256 == 0 and tq % 16 == 0 and tq <= 512
    return pl.pallas_call(
        functools.partial(_attn_kernel, tq=tq, seq=seq),
        grid=(b, HEADS, seq // tq),
        in_specs=[
            pl.BlockSpec(memory_space=pltpu.SMEM),
            pl.BlockSpec((4, QK_DIM), lambda b_, h, i: (0, 0)),
            pl.BlockSpec((None, tq, HEAD_W), lambda b_, h, i: (b_, i, h)),
            pl.BlockSpec((None, seq, HEAD_W), lambda b_, h, i: (b_, 0, HEADS + h)),
            pl.BlockSpec((None, seq, HEAD_W), lambda b_, h, i: (b_, 0, 2 * HEADS + h)),
            pl.BlockSpec((LEAD, HEAD_W), lambda b_, h, i: (0, HEADS + h)),
            pl.BlockSpec((LEAD, HEAD_W), lambda b_, h, i: (0, 2 * HEADS + h)),
            pl.BlockSpec((1, HEAD_W), lambda b_, h, i: (0, 0)),
            pl.BlockSpec((1, HEAD_W), lambda b_, h, i: (0, 0)),
            pl.BlockSpec((HEAD_W, 1), lambda b_, h, i: (0, 0)),
        ],
        out_specs=pl.BlockSpec((None, tq, HEAD_W), lambda b_, h, i: (b_, i, h)),
        out_shape=jax.ShapeDtypeStruct((b, seq, GROUP_W), BF16),
        scratch_shapes=[
            pltpu.VMEM((seq, HEAD_W), BF16),
            pltpu.VMEM((HEAD_W, seq), BF16),
            pltpu.VMEM((LEAD, HEAD_W), BF16),
            pltpu.VMEM((HEAD_W, LEAD), BF16),
            pltpu.VMEM((tq, HEAD_W), BF16),
            pltpu.VMEM((2, HEAD_W, tq), F32),
            pltpu.VMEM((2, tq, tq), F32),
            pltpu.VMEM((2, tq, tq), F32),
        ],
        compiler_params=pltpu.CompilerParams(
            dimension_semantics=("parallel", "parallel", "arbitrary"),
            vmem_limit_bytes=VMEM_LIMIT),
        name="diff_attn",
    )(slopes, lvec, proj3, proj3, proj3, lead_proj, lead_proj, qw, kw, sw)


def _gdn_prep_kernel(tq_ref, tk_ref, tv_ref, hq_ref, hk_ref, hv_ref, lq_ref, lk_ref, lv_ref,
                     tg_ref, lg_ref, cwq_ref, cwk_ref, cwv_ref, alog_ref, dtb_ref,
                     u_ref, w_ref, qg_ref, kdt_ref, qk_ref, dec_ref, xs_ref, gs_ref):
    s = pl.program_id(1)
    is_lead = s == 0
    width = GROUP_W
    srcs = ((tq_ref, hq_ref, lq_ref), (tk_ref, hk_ref, lk_ref), (tv_ref, hv_ref, lv_ref))

    @pl.when(is_lead)
    def _():
        rowid = lax.broadcasted_iota(jnp.int32, (CHUNK, width), 0)
        for idx, (_, _, l_ref) in enumerate(srcs):
            cs = slice(idx * width, (idx + 1) * width)
            xs_ref[0:8, cs] = jnp.zeros((8, width), F32)
            xs_ref[8:8 + CHUNK, cs] = jnp.where(rowid >= N_PAD, l_ref[...].astype(F32), 0.0)
        gs_ref[...] = lg_ref[...]

    @pl.when(s == 1)
    def _():
        for idx, (t_ref, _, l_ref) in enumerate(srcs):
            cs = slice(idx * width, (idx + 1) * width)
            xs_ref[0:8, cs] = l_ref[LEAD - 16:LEAD, :].astype(F32)[8:16]
            xs_ref[8:8 + CHUNK, cs] = t_ref[...].astype(F32)
        gs_ref[...] = tg_ref[...]

    @pl.when(s > 1)
    def _():
        for idx, (t_ref, h_ref, _) in enumerate(srcs):
            cs = slice(idx * width, (idx + 1) * width)
            xs_ref[0:8, cs] = h_ref[...].astype(F32)[8:16]
            xs_ref[8:8 + CHUNK, cs] = t_ref[...].astype(F32)
        gs_ref[...] = tg_ref[...]

    rowi = lax.broadcasted_iota(jnp.int32, (CHUNK, CHUNK), 0)
    lanei = lax.broadcasted_iota(jnp.int32, (CHUNK, CHUNK), 1)
    incl = rowi >= lanei
    eye = (rowi == lanei).astype(F32)

    g_t = gs_ref[...].T[0:2 * HEADS]
    vmask = (lax.broadcasted_iota(jnp.int32, (HEADS, CHUNK), 1)
             >= jnp.where(is_lead, N_PAD, 0)).astype(F32)
    beta_t = jax.nn.sigmoid(g_t[0:HEADS]) * vmask
    t = g_t[HEADS:] + dtb_ref[...]
    softplus = jnp.maximum(t, 0.0) + jnp.log(1.0 + jnp.exp(-jnp.abs(t)))
    decay_t = -jnp.exp(alog_ref[...]) * softplus * vmask
    gc_t = _dot(decay_t, (rowi <= lanei).astype(F32), HIGHEST)
    cols = jnp.concatenate(
        [beta_t, gc_t, jnp.zeros((CHUNK - 2 * HEADS, CHUNK), F32)], axis=0).T

    heads = range(HEADS)
    mks, pks, rhs = [], [], []
    for hh in heads:
        hs = slice(hh * HEAD_W, (hh + 1) * HEAD_W)
        beta = cols[:, hh:hh + 1]
        gc = cols[:, HEADS + hh:HEADS + hh + 1]
        gc_row = gc_t[hh:hh + 1]
        g_last = gc_row[:, CHUNK - 1:CHUNK]

        def conv_silu(idx, cw_ref):
            c0 = idx * width + hh * HEAD_W
            y = xs_ref[5:5 + CHUNK, c0:c0 + HEAD_W] * cw_ref[0:1, hs]
            for j in range(1, CONV_K):
                y = y + xs_ref[5 + j:5 + j + CHUNK, c0:c0 + HEAD_W] * cw_ref[j:j + 1, hs]
            return y * jax.nn.sigmoid(y)

        q = conv_silu(0, cwq_ref)
        k = conv_silu(1, cwk_ref)
        v = conv_silu(2, cwv_ref)
        q = q * lax.rsqrt(jnp.sum(q * q, axis=-1, keepdims=True) + EPS) * (HEAD_W ** -0.5)
        k = k * lax.rsqrt(jnp.sum(k * k, axis=-1, keepdims=True) + EPS)

        decay = jnp.where(incl, jnp.exp(jnp.where(incl, gc - gc_row, 0.0)), 0.0)
        kb = k * beta
        k16 = k.astype(BF16)
        lmat = jnp.where(rowi > lanei, _dot_nt(kb.astype(BF16), k16) * decay, 0.0)
        qk_ref[:, hs] = (_dot_nt(q.astype(BF16), k16) * decay).astype(BF16)
        qg_ref[:, hs] = (q * jnp.exp(gc)).astype(BF16)
        kdt_ref[:, hs] = (k * jnp.exp(g_last - gc)).T.astype(BF16)
        dec_ref[hh * 8:(hh + 1) * 8, :] = jnp.broadcast_to(jnp.exp(g_last), (8, HEAD_W))
        mks.append(-lmat)
        pks.append(eye - lmat)
        rhs.append(jnp.concatenate([v * beta, kb * jnp.exp(gc)], axis=1))

    ms = [_split(m) for m in mks]
    mks = [_dot3(m, m) for m in ms]
    for _ in range(5):
        ms = [_split(m) for m in mks]
        rs = [_dot3(m, [_split(p), m]) for m, p in zip(ms, pks)]
        pks = [p + r[:, :CHUNK] for p, r in zip(pks, rs)]
        mks = [r[:, CHUNK:] for r in rs]
    pks = [p + _dot3(_split(m), _split(p)) for m, p in zip(mks, pks)]

    for hh in heads:
        hs = slice(hh * HEAD_W, (hh + 1) * HEAD_W)
        uw = _dot3(_split(pks[hh]), _split(rhs[hh]))
        u_ref[:, hs] = uw[:, :HEAD_W].astype(BF16)
        w_ref[:, hs] = uw[:, HEAD_W:].astype(BF16)


def _gdn_prep(proj3, lead_proj, gates3, lead_gates, conv_wt, alog_col, dtb_col):
    b, seq, _ = proj3.shape
    nb = 1 + seq // CHUNK
    tok = lambda g: pl.BlockSpec(
        (None, CHUNK, GROUP_W), lambda b_, s: (b_, jnp.maximum(s - 1, 0), g))
    halo = lambda g: pl.BlockSpec(
        (None, 16, GROUP_W),
        lambda b_, s: (b_, jnp.maximum((s - 1) * (CHUNK // 16) - 1, 0), g))
    lead = lambda g: pl.BlockSpec((LEAD, GROUP_W), lambda b_, s: (0, g))
    cw = lambda g: pl.BlockSpec((CONV_K, GROUP_W), lambda b_, s: (0, g))
    col = pl.BlockSpec((HEADS, 1), lambda b_, s: (0, 0))
    out = pl.BlockSpec((None, CHUNK, GROUP_W), lambda b_, s: (b_, s, 0))
    big = jax.ShapeDtypeStruct((b, nb * CHUNK, GROUP_W), BF16)
    return pl.pallas_call(
        _gdn_prep_kernel,
        grid=(b, nb),
        in_specs=[
            tok(3), tok(4), tok(5), halo(3), halo(4), halo(5), lead(3), lead(4), lead(5),
            pl.BlockSpec((None, CHUNK, HEAD_W), lambda b_, s: (b_, jnp.maximum(s - 1, 0), 0)),
            pl.BlockSpec((LEAD, HEAD_W), lambda b_, s: (0, 0)),
            cw(0), cw(1), cw(2), col, col,
        ],
        out_specs=[out, out, out, out, out,
                   pl.BlockSpec((None, None, 8 * HEADS, HEAD_W), lambda b_, s: (b_, s, 0, 0))],
        out_shape=[big, big, big, big, big,
                   jax.ShapeDtypeStruct((b, nb, 8 * HEADS, HEAD_W), F32)],
        scratch_shapes=[pltpu.VMEM((8 + CHUNK, 3 * GROUP_W), F32),
                        pltpu.VMEM((CHUNK, HEAD_W), F32)],
        compiler_params=pltpu.CompilerParams(
            dimension_semantics=("parallel", "arbitrary"), vmem_limit_bytes=VMEM_LIMIT),
        name="gdn_prep",
    )(proj3, proj3, proj3, proj3, proj3, proj3, lead_proj, lead_proj, lead_proj,
      gates3, lead_gates, conv_wt, conv_wt, conv_wt, alog_col, dtb_col)


def _gdn_seq_kernel(u_ref, w_ref, qg_ref, kdt_ref, qk_ref, dec_ref, z_ref, nw_ref, o_ref, s_ref):
    s = pl.program_id(1)

    @pl.when(s == 0)
    def _():
        s_ref[...] = jnp.zeros_like(s_ref)

    for h in range(HEADS):
        hs = slice(h * HEAD_W, (h + 1) * HEAD_W)
        state = s_ref[h]
        s16 = state.astype(BF16)
        v_new = u_ref[:, hs].astype(F32) - _dot(w_ref[:, hs], s16)
        v16 = v_new.astype(BF16)
        o = _dot(qg_ref[:, hs], s16) + _dot(qk_ref[:, hs], v16)
        s_ref[h] = dec_ref[h * 8:h * 8 + 1, :] * state + _dot(kdt_ref[:, hs], v16)

        @pl.when(s > 0)
        def _():
            z = z_ref[:, hs].astype(F32)
            o_ref[:, hs] = (_rms_rows(o, nw_ref[...]) * (z * jax.nn.sigmoid(z))).astype(o_ref.dtype)


def _gdn_seq(u, w, qg, kdt, qk, dec, proj3, onw):
    b, seq, _ = proj3.shape
    nb = 1 + seq // CHUNK
    blk = pl.BlockSpec((None, CHUNK, GROUP_W), lambda b_, s: (b_, s, 0))
    return pl.pallas_call(
        _gdn_seq_kernel,
        grid=(b, nb),
        in_specs=[
            blk, blk, blk, blk, blk,
            pl.BlockSpec((None, None, 8 * HEADS, HEAD_W), lambda b_, s: (b_, s, 0, 0)),
            pl.BlockSpec((None, CHUNK, GROUP_W), lambda b_, s: (b_, jnp.maximum(s - 1, 0), 6)),
            pl.BlockSpec((1, HEAD_W), lambda b_, s: (0, 0)),
        ],
        out_specs=pl.BlockSpec((None, CHUNK, GROUP_W), lambda b_, s: (b_, jnp.maximum(s - 1, 0), 0)),
        out_shape=jax.ShapeDtypeStruct((b, seq, GROUP_W), BF16),
        scratch_shapes=[pltpu.VMEM((HEADS, HEAD_W, HEAD_W), F32)],
        compiler_params=pltpu.CompilerParams(
            dimension_semantics=("parallel", "arbitrary"), vmem_limit_bytes=VMEM_LIMIT),
        name="gdn_seq",
    )(u, w, qg, kdt, qk, dec, proj3, onw)


def _outproj_kernel(x_ref, oa_ref, od_ref, wa_ref, wd_ref, o_ref):
    o_ref[...] = x_ref[...] + _dot(oa_ref[...], wa_ref[...]) + _dot(od_ref[...], wd_ref[...])


def _outproj(x2d, oa, od, w_out16, tm):
    m = x2d.shape[0]
    assert m % tm == 0
    return pl.pallas_call(
        _outproj_kernel,
        grid=(m // tm,),
        in_specs=[
            pl.BlockSpec((tm, D_MODEL), lambda i: (i, 0)),
            pl.BlockSpec((tm, GROUP_W), lambda i: (i, 0)),
            pl.BlockSpec((tm, GROUP_W), lambda i: (i, 0)),
            pl.BlockSpec((GROUP_W, D_MODEL), lambda i: (0, 0)),
            pl.BlockSpec((GROUP_W, D_MODEL), lambda i: (1, 0)),
        ],
        out_specs=pl.BlockSpec((tm, D_MODEL), lambda i: (i, 0)),
        out_shape=jax.ShapeDtypeStruct((m, D_MODEL), F32),
        compiler_params=pltpu.CompilerParams(
            dimension_semantics=("parallel",), vmem_limit_bytes=VMEM_LIMIT),
        name="outproj",
    )(x2d, oa, od, w_out16, w_out16)


def _ffn_kernel(h_ref, nw_ref, wg_ref, wu_ref, wd_ref, o_ref, u_ref, *, row_chunk):
    j = pl.program_id(1)

    @pl.when(j == 0)
    def _():
        def body(c, carry):
            r = pl.multiple_of(c * row_chunk, row_chunk)
            x = h_ref[pl.ds(r, row_chunk), :]
            u_ref[pl.ds(r, row_chunk), :] = _rms_rows(x, nw_ref[...]).astype(BF16)
            o_ref[pl.ds(r, row_chunk), :] = x
            return carry

        lax.fori_loop(0, h_ref.shape[0] // row_chunk, body, 0)

    u = u_ref[...]
    g = _dot(u, wg_ref[...])
    a = (g * jax.nn.sigmoid(g) * _dot(u, wu_ref[...])).astype(BF16)
    o_ref[...] += _dot(a, wd_ref[...])


def _ffn(h2d, norm_w, wg, wu, wd, tm, th):
    m = h2d.shape[0]
    assert m % tm == 0 and FFN_HIDDEN % th == 0
    return pl.pallas_call(
        functools.partial(_ffn_kernel, row_chunk=min(256, tm)),
        grid=(m // tm, FFN_HIDDEN // th),
        in_specs=[
            pl.BlockSpec((tm, D_MODEL), lambda i, j: (i, 0)),
            pl.BlockSpec((1, D_MODEL), lambda i, j: (0, 0)),
            pl.BlockSpec((D_MODEL, th), lambda i, j: (0, j)),
            pl.BlockSpec((D_MODEL, th), lambda i, j: (0, j)),
            pl.BlockSpec((th, D_MODEL), lambda i, j: (j, 0)),
        ],
        out_specs=pl.BlockSpec((tm, D_MODEL), lambda i, j: (i, 0)),
        out_shape=jax.ShapeDtypeStruct((m, D_MODEL), F32),
        scratch_shapes=[pltpu.VMEM((tm, D_MODEL), BF16)],
        compiler_params=pltpu.CompilerParams(
            dimension_semantics=("parallel", "arbitrary"), vmem_limit_bytes=VMEM_LIMIT),
        name="ffn",
    )(h2d, norm_w, wg, wu, wd)


def kernel(x, meta_tokens, attn_norm_w, w_in, q_norm_w, k_norm_w, lambda_q1, lambda_k1, lambda_q2,
           lambda_k2, subln_w, conv_w, a_log, dt_bias, o_norm_w, w_out, ffn_norm_w, w_gate, w_up,
           w_down):
    b, seq, _ = x.shape
    m = b * seq
    x2d = x.reshape(m, D_MODEL)
    lead = jnp.concatenate([jnp.zeros((N_PAD, D_MODEL), x.dtype), meta_tokens.astype(x.dtype)], 0)

    w_main = w_in[0, :, :MAIN_COLS].astype(BF16)
    w_gates = jnp.pad(w_in[0, :, MAIN_COLS:], ((0, 0), (0, HEAD_W - GATE_COLS))).astype(BF16)
    tm = min(1024, m)
    proj, gates = _inproj(x2d, attn_norm_w, w_main, w_gates, tm, 1024)
    lead_proj, lead_gates = _inproj(lead, attn_norm_w, w_main, w_gates, LEAD, 1024)
    proj3 = proj.reshape(b, seq, MAIN_COLS)

    slopes = 2.0 ** (-8.0 * jnp.arange(1, HEADS + 1, dtype=F32) / HEADS)
    lvec = jnp.concatenate([lambda_q1, lambda_k1, lambda_q2, lambda_k2], 0).astype(F32)
    o_a = _attn(proj3, lead_proj, slopes, lvec, jnp.tile(q_norm_w, (1, 2)),
                jnp.tile(k_norm_w, (1, 2)), subln_w.reshape(HEAD_W, 1), min(512, seq))

    u, w, qg, kdt, qk, dec = _gdn_prep(
        proj3, lead_proj, gates.reshape(b, seq, HEAD_W), lead_gates, conv_w[0].T,
        a_log.astype(F32).reshape(HEADS, 1), dt_bias.astype(F32).reshape(HEADS, 1))
    o_d = _gdn_seq(u, w, qg, kdt, qk, dec, proj3, o_norm_w)

    h1 = _outproj(x2d, o_a.reshape(m, GROUP_W), o_d.reshape(m, GROUP_W), w_out[0].astype(BF16),
                  min(512, m))
    out = _ffn(h1, ffn_norm_w, w_gate[0].astype(BF16), w_up[0].astype(BF16),
               w_down[0].astype(BF16), min(1024, m), 512)
    return out.reshape(b, seq, D_MODEL)
```

```python
import functools

import jax
import jax.numpy as jnp
import numpy as np
from jax import lax
from jax.experimental import pallas as pl
from jax.experimental.pallas import tpu as pltpu

F32 = jnp.float32
BF16 = jnp.bfloat16
HIGHEST = lax.Precision.HIGHEST

D_MODEL = 2048
N_META = 16
LEAD = 128
N_PAD = LEAD - N_META
HEADS = 8
HEAD_W = 128
QK_DIM = 64
GROUP_W = HEADS * HEAD_W
MAIN_COLS = 7 * GROUP_W
GATE_COLS = 2 * HEADS
CONV_K = 4
FFN_HIDDEN = 5632
EPS = 1e-6
NEG = -1e30
LAMBDA_INIT = 0.2
CHUNK = 128
VMEM_LIMIT = 56 * 1024 * 1024
LOG2E = 1.4426950408889634


def _bf16_pieces(x, n):
    out = []
    for _ in range(n):
        bits = np.array(x, np.float32).view(np.uint32)
        bits = (bits + 0x7FFF + ((bits >> 16) & 1)) & 0xFFFF0000
        p = float(bits.view(np.float32))
        out.append(p)
        x -= p
    return tuple(out)


LOG2E_BF16_PIECES = _bf16_pieces(LOG2E, 3)


def _dot(a, b, precision=None):
    return jnp.dot(a, b, preferred_element_type=F32, precision=precision)


def _dot_nt(a, b):
    return lax.dot_general(a, b, (((1,), (1,)), ((), ())), preferred_element_type=F32)


def _split(x):
    hi = x.astype(BF16)
    return hi, (x - hi.astype(F32)).astype(BF16)


def _dot3(a, b):
    a_hi, a_lo = a
    if isinstance(b, list):
        b_hi = jnp.concatenate([x[0] for x in b], axis=1)
        b_lo = jnp.concatenate([x[1] for x in b], axis=1)
    else:
        b_hi, b_lo = b
    rows = a_hi.shape[0]
    r = _dot(jnp.concatenate([a_hi, a_lo], axis=0), b_hi)
    return r[:rows] + r[rows:] + _dot(a_hi, b_lo)


def _rms_rows(x, w_row):
    return x * lax.rsqrt(jnp.mean(x * x, axis=-1, keepdims=True) + EPS) * w_row


def _inproj_kernel(x_ref, nw_ref, w_ref, wg_ref, o_ref, g_ref, u_ref, *, row_chunk):
    j = pl.program_id(1)

    @pl.when(j == 0)
    def _():
        def body(c, carry):
            r = pl.multiple_of(c * row_chunk, row_chunk)
            u = _rms_rows(x_ref[pl.ds(r, row_chunk), :], nw_ref[...]).astype(BF16)
            u_ref[pl.ds(r, row_chunk), :] = u
            g_ref[pl.ds(r, row_chunk), :] = _dot(u, wg_ref[...])
            return carry

        lax.fori_loop(0, x_ref.shape[0] // row_chunk, body, 0)

    o_ref[...] = _dot(u_ref[...], w_ref[...]).astype(o_ref.dtype)


def _inproj(x2d, norm_w, w_main, w_gate, tm, tn):
    m = x2d.shape[0]
    assert m % tm == 0 and MAIN_COLS % tn == 0
    row_chunk = min(256, tm)
    return pl.pallas_call(
        functools.partial(_inproj_kernel, row_chunk=row_chunk),
        grid=(m // tm, MAIN_COLS // tn),
        in_specs=[
            pl.BlockSpec((tm, D_MODEL), lambda i, j: (i, 0)),
            pl.BlockSpec((1, D_MODEL), lambda i, j: (0, 0)),
            pl.BlockSpec((D_MODEL, tn), lambda i, j: (0, j)),
            pl.BlockSpec((D_MODEL, HEAD_W), lambda i, j: (0, 0)),
        ],
        out_specs=[
            pl.BlockSpec((tm, tn), lambda i, j: (i, j)),
            pl.BlockSpec((tm, HEAD_W), lambda i, j: (i, 0)),
        ],
        out_shape=[
            jax.ShapeDtypeStruct((m, MAIN_COLS), BF16),
            jax.ShapeDtypeStruct((m, HEAD_W), F32),
        ],
        scratch_shapes=[pltpu.VMEM((tm, D_MODEL), BF16)],
        compiler_params=pltpu.CompilerParams(
            dimension_semantics=("parallel", "arbitrary"), vmem_limit_bytes=VMEM_LIMIT),
        name="inproj",
    )(x2d, norm_w, w_main, w_gate)


def _halfnorm(x, w_row):
    r = lax.broadcasted_iota(jnp.int32, (HEAD_W, HEAD_W), 0) < QK_DIM
    c = lax.broadcasted_iota(jnp.int32, (HEAD_W, HEAD_W), 1) < QK_DIM
    same_half = (r == c).astype(BF16)
    hi, lo = _split(x * x)
    rows = x.shape[0]
    s = _dot(jnp.concatenate([hi, lo], axis=0), same_half)
    ms = (s[:rows] + s[rows:]) * (1.0 / QK_DIM)
    return x * lax.rsqrt(ms + EPS) * w_row


def _attn_kernel(slopes_ref, lvec_ref, q_ref, k_ref, v_ref, lk_ref, lv_ref, qw_ref, kw_ref,
                 swc_ref, o_ref, kn_ref, vt_ref, lkn_ref, lvt_ref, kaug_ref, acc_ref, sa_ref,
                 sb_ref, *, tq, seq):
    h = pl.program_id(1)
    i = pl.program_id(2)
    slope = slopes_ref[h]

    @pl.when(i == 0)
    def _():
        def body(c, carry):
            r = pl.multiple_of(c * 256, 256)
            kn_ref[pl.ds(r, 256), :] = _halfnorm(
                k_ref[pl.ds(r, 256), :].astype(F32), kw_ref[...]).astype(BF16)
            vt_ref[:, pl.ds(r, 256)] = v_ref[pl.ds(r, 256), :].astype(F32).T.astype(BF16)
            return carry

        lax.fori_loop(0, seq // 256, body, 0)
        lkn_ref[...] = _halfnorm(lk_ref[...].astype(F32), kw_ref[...]).astype(BF16)
        lvt_ref[...] = lv_ref[...].astype(F32).T.astype(BF16)
        kk = lax.broadcasted_iota(jnp.int32, (tq, HEAD_W), 0)
        ln = lax.broadcasted_iota(jnp.int32, (tq, HEAD_W), 1)
        hi = ((kk // 16) * 16).astype(F32)
        lo_ = (kk % 16).astype(F32)
        kaug_ref[...] = (slope * jnp.where(ln < 3, hi, jnp.where(ln < 6, lo_, 0.0))).astype(BF16)

    lane = lax.broadcasted_iota(jnp.int32, (tq, HEAD_W), 1)
    lo = lane < QK_DIM
    qn = _halfnorm(q_ref[...].astype(F32), qw_ref[...]) * (QK_DIM ** -0.5 * LOG2E)
    sub = lax.broadcasted_iota(jnp.int32, (HEAD_W, tq), 0)
    aug = jnp.zeros((HEAD_W, tq), F32)
    for n, piece in enumerate(LOG2E_BF16_PIECES):
        aug = jnp.where((sub == n) | (sub == n + 3), piece, aug)
    ws = tuple(jnp.concatenate([x.T, aug], axis=0).astype(BF16)
               for x in (jnp.where(lo, qn, 0.0), jnp.where(lo, 0.0, qn)))

    slope2 = slope * LOG2E
    q_off = slope2 * lax.broadcasted_iota(jnp.int32, (1, tq), 1).astype(F32)
    key_ok = lax.broadcasted_iota(jnp.int32, (LEAD, tq), 0) >= N_PAD
    ms, ls = [], []
    for mp in range(2):
        s = jnp.where(key_ok, _dot(lkn_ref[...], ws[mp][:HEAD_W]) + q_off, NEG)
        m = jnp.max(s, axis=0, keepdims=True)
        p = jnp.exp2(s - m)
        ms.append(m)
        ls.append(jnp.sum(p, axis=0, keepdims=True))
        acc_ref[mp] = _dot(lvt_ref[...], p.astype(BF16))

    key_i = lax.broadcasted_iota(jnp.int32, (tq, tq), 0)
    qry_i = lax.broadcasted_iota(jnp.int32, (tq, tq), 1)

    def scores(j, dst_ref, diag):
        r = pl.multiple_of(j * tq, tq)
        lhs = jnp.concatenate([kn_ref[pl.ds(r, tq), :], kaug_ref[...]], axis=1)
        bms = []
        for mp in range(2):
            raw = _dot(lhs, ws[mp])
            if diag:
                raw = jnp.where(key_i <= qry_i, raw, NEG)
            dst_ref[mp] = raw
            bms.append(jnp.max(raw, axis=0, keepdims=True))
        return tuple(bms)

    def accumulate(j, src_ref, bms, carry):
        r = pl.multiple_of(j * tq, tq)
        vt = vt_ref[:, pl.ds(r, tq)]
        c = slope2 * ((j - i) * tq).astype(F32)
        out = []
        for mp in range(2):
            m, l = carry[2 * mp], carry[2 * mp + 1]
            m_new = jnp.maximum(m, bms[mp] + c)
            alpha = jnp.exp2(m - m_new)
            p = jnp.exp2(src_ref[mp] - (m_new - c))
            out += [m_new, alpha * l + jnp.sum(p, axis=0, keepdims=True)]
            acc_ref[mp] = alpha * acc_ref[mp] + _dot(vt, p.astype(BF16))
        return tuple(out)

    carry = (ms[0], ls[0], ms[1], ls[1])
    bm_a = scores(i, sa_ref, True)

    def pair(t, state):
        j_a, bm_a, carry = state
        bm_b = scores(2 * t, sb_ref, False)
        carry = accumulate(j_a, sa_ref, bm_a, carry)
        bm_a = scores(2 * t + 1, sa_ref, False)
        carry = accumulate(2 * t, sb_ref, bm_b, carry)
        return 2 * t + 1, bm_a, carry

    j_a, bm_a, carry = lax.fori_loop(0, i // 2, pair, (i, bm_a, carry))

    def odd_tail(carry):
        bm_b = scores(i - 1, sb_ref, False)
        carry = accumulate(j_a, sa_ref, bm_a, carry)
        return accumulate(i - 1, sb_ref, bm_b, carry)

    carry = lax.cond(i % 2 == 1, odd_tail, lambda c: accumulate(j_a, sa_ref, bm_a, c), carry)

    lv4 = lvec_ref[...]
    lam = (jnp.exp(jnp.sum(lv4[0:1] * lv4[1:2], axis=-1, keepdims=True))
           - jnp.exp(jnp.sum(lv4[2:3] * lv4[3:4], axis=-1, keepdims=True)) + LAMBDA_INIT)
    o = acc_ref[0] / carry[1] - lam * (acc_ref[1] / carry[3])
    o = o * lax.rsqrt(jnp.mean(o * o, axis=0, keepdims=True) + EPS) * swc_ref[...]
    o_ref[...] = (o * (1.0 - LAMBDA_INIT)).T.astype(o_ref.dtype)


def _attn(proj3, lead_proj, slopes, lvec, qw, kw, sw, tq):
    b, seq, _ = proj3.shape
    assert seq % tq == 0 and seq % 256 == 0 and tq % 16 == 0 and tq <= 512
    return pl.pallas_call(
        functools.partial(_attn_kernel, tq=tq, seq=seq),
        grid=(b, HEADS, seq // tq),
        in_specs=[
            pl.BlockSpec(memory_space=pltpu.SMEM),
            pl.BlockSpec((4, QK_DIM), lambda b_, h, i: (0, 0)),
            pl.BlockSpec((None, tq, HEAD_W), lambda b_, h, i: (b_, i, h)),
            pl.BlockSpec((None, seq, HEAD_W), lambda b_, h, i: (b_, 0, HEADS + h)),
            pl.BlockSpec((None, seq, HEAD_W), lambda b_, h, i: (b_, 0, 2 * HEADS + h)),
            pl.BlockSpec((LEAD, HEAD_W), lambda b_, h, i: (0, HEADS + h)),
            pl.BlockSpec((LEAD, HEAD_W), lambda b_, h, i: (0, 2 * HEADS + h)),
            pl.BlockSpec((1, HEAD_W), lambda b_, h, i: (0, 0)),
            pl.BlockSpec((1, HEAD_W), lambda b_, h, i: (0, 0)),
            pl.BlockSpec((HEAD_W, 1), lambda b_, h, i: (0, 0)),
        ],
        out_specs=pl.BlockSpec((None, tq, HEAD_W), lambda b_, h, i: (b_, i, h)),
        out_shape=jax.ShapeDtypeStruct((b, seq, GROUP_W), BF16),
        scratch_shapes=[
            pltpu.VMEM((seq, HEAD_W), BF16),
            pltpu.VMEM((HEAD_W, seq), BF16),
            pltpu.VMEM((LEAD, HEAD_W), BF16),
            pltpu.VMEM((HEAD_W, LEAD), BF16),
            pltpu.VMEM((tq, HEAD_W), BF16),
            pltpu.VMEM((2, HEAD_W, tq), F32),
            pltpu.VMEM((2, tq, tq), F32),
            pltpu.VMEM((2, tq, tq), F32),
        ],
        compiler_params=pltpu.CompilerParams(
            dimension_semantics=("parallel", "parallel", "arbitrary"),
            vmem_limit_bytes=VMEM_LIMIT),
        name="diff_attn",
    )(slopes, lvec, proj3, proj3, proj3, lead_proj, lead_proj, qw, kw, sw)


def _gdn_prep_kernel(tq_ref, tk_ref, tv_ref, hq_ref, hk_ref, hv_ref, lq_ref, lk_ref, lv_ref,
                     tg_ref, lg_ref, cwq_ref, cwk_ref, cwv_ref, alog_ref, dtb_ref,
                     u_ref, wq_ref, qkd_ref, dec_ref, xs_ref, gs_ref):
    s = pl.program_id(1)
    is_lead = s == 0
    width = GROUP_W
    srcs = ((tq_ref, hq_ref, lq_ref), (tk_ref, hk_ref, lk_ref), (tv_ref, hv_ref, lv_ref))

    @pl.when(is_lead)
    def _():
        rowid = lax.broadcasted_iota(jnp.int32, (CHUNK, width), 0)
        for idx, (_, _, l_ref) in enumerate(srcs):
            cs = slice(idx * width, (idx + 1) * width)
            xs_ref[0:8, cs] = jnp.zeros((8, width), F32)
            xs_ref[8:8 + CHUNK, cs] = jnp.where(rowid >= N_PAD, l_ref[...].astype(F32), 0.0)
        gs_ref[...] = lg_ref[...]

    @pl.when(s == 1)
    def _():
        for idx, (t_ref, _, l_ref) in enumerate(srcs):
            cs = slice(idx * width, (idx + 1) * width)
            xs_ref[0:8, cs] = l_ref[LEAD - 16:LEAD, :].astype(F32)[8:16]
            xs_ref[8:8 + CHUNK, cs] = t_ref[...].astype(F32)
        gs_ref[...] = tg_ref[...]

    @pl.when(s > 1)
    def _():
        for idx, (t_ref, h_ref, _) in enumerate(srcs):
            cs = slice(idx * width, (idx + 1) * width)
            xs_ref[0:8, cs] = h_ref[...].astype(F32)[8:16]
            xs_ref[8:8 + CHUNK, cs] = t_ref[...].astype(F32)
        gs_ref[...] = tg_ref[...]

    rowi = lax.broadcasted_iota(jnp.int32, (CHUNK, CHUNK), 0)
    lanei = lax.broadcasted_iota(jnp.int32, (CHUNK, CHUNK), 1)
    incl = rowi >= lanei
    eye = (rowi == lanei).astype(F32)

    g_t = gs_ref[...].T[0:2 * HEADS]
    vmask = (lax.broadcasted_iota(jnp.int32, (HEADS, CHUNK), 1)
             >= jnp.where(is_lead, N_PAD, 0)).astype(F32)
    beta_t = jax.nn.sigmoid(g_t[0:HEADS]) * vmask
    t = g_t[HEADS:] + dtb_ref[...]
    softplus = jnp.maximum(t, 0.0) + jnp.log(1.0 + jnp.exp(-jnp.abs(t)))
    decay_t = -jnp.exp(alog_ref[...]) * softplus * vmask
    gc_t = _dot(decay_t, (rowi <= lanei).astype(F32), HIGHEST)
    cols = jnp.concatenate(
        [beta_t, gc_t, jnp.zeros((CHUNK - 2 * HEADS, CHUNK), F32)], axis=0).T

    heads = range(HEADS)
    mks, pks, rhs = [], [], []
    for hh in heads:
        hs = slice(hh * HEAD_W, (hh + 1) * HEAD_W)
        beta = cols[:, hh:hh + 1]
        gc = cols[:, HEADS + hh:HEADS + hh + 1]
        gc_row = gc_t[hh:hh + 1]
        g_last = gc_row[:, CHUNK - 1:CHUNK]

        def conv_silu(idx, cw_ref):
            c0 = idx * width + hh * HEAD_W
            y = xs_ref[5:5 + CHUNK, c0:c0 + HEAD_W] * cw_ref[0:1, hs]
            for j in range(1, CONV_K):
                y = y + xs_ref[5 + j:5 + j + CHUNK, c0:c0 + HEAD_W] * cw_ref[j:j + 1, hs]
            return y * jax.nn.sigmoid(y)

        q = conv_silu(0, cwq_ref)
        k = conv_silu(1, cwk_ref)
        v = conv_silu(2, cwv_ref)
        q = q * lax.rsqrt(jnp.sum(q * q, axis=-1, keepdims=True) + EPS) * (HEAD_W ** -0.5)
        k = k * lax.rsqrt(jnp.sum(k * k, axis=-1, keepdims=True) + EPS)

        decay = jnp.where(incl, jnp.exp(jnp.where(incl, gc - gc_row, 0.0)), 0.0)
        kb = k * beta
        k16 = k.astype(BF16)
        lmat = jnp.where(rowi > lanei, _dot_nt(kb.astype(BF16), k16) * decay, 0.0)
        qkd_ref[0:CHUNK, hs] = (_dot_nt(q.astype(BF16), k16) * decay).astype(BF16)
        qkd_ref[CHUNK:, hs] = (k * jnp.exp(g_last - gc)).T.astype(BF16)
        wq_ref[CHUNK:, hs] = (q * jnp.exp(gc)).astype(BF16)
        dec_ref[hh * 8:(hh + 1) * 8, :] = jnp.broadcast_to(jnp.exp(g_last), (8, HEAD_W))
        mks.append(-lmat)
        pks.append(eye - lmat)
        rhs.append(jnp.concatenate([v * beta, kb * jnp.exp(gc)], axis=1))

    ms = [_split(m) for m in mks]
    mks = [_dot3(m, m) for m in ms]
    for _ in range(5):
        ms = [_split(m) for m in mks]
        rs = [_dot3(m, [_split(p), m]) for m, p in zip(ms, pks)]
        pks = [p + r[:, :CHUNK] for p, r in zip(pks, rs)]
        mks = [r[:, CHUNK:] for r in rs]
    pks = [p + _dot3(_split(m), _split(p)) for m, p in zip(mks, pks)]

    for hh in heads:
        hs = slice(hh * HEAD_W, (hh + 1) * HEAD_W)
        uw = _dot3(_split(pks[hh]), _split(rhs[hh]))
        u_ref[:, hs] = uw[:, :HEAD_W].astype(BF16)
        wq_ref[0:CHUNK, hs] = uw[:, HEAD_W:].astype(BF16)


def _gdn_prep(proj3, lead_proj, gates3, lead_gates, conv_wt, alog_col, dtb_col):
    b, seq, _ = proj3.shape
    nb = 1 + seq // CHUNK
    tok = lambda g: pl.BlockSpec(
        (None, CHUNK, GROUP_W), lambda b_, s: (b_, jnp.maximum(s - 1, 0), g))
    halo = lambda g: pl.BlockSpec(
        (None, 16, GROUP_W),
        lambda b_, s: (b_, jnp.maximum((s - 1) * (CHUNK // 16) - 1, 0), g))
    lead = lambda g: pl.BlockSpec((LEAD, GROUP_W), lambda b_, s: (0, g))
    cw = lambda g: pl.BlockSpec((CONV_K, GROUP_W), lambda b_, s: (0, g))
    col = pl.BlockSpec((HEADS, 1), lambda b_, s: (0, 0))
    out = pl.BlockSpec((None, CHUNK, GROUP_W), lambda b_, s: (b_, s, 0))
    out2 = pl.BlockSpec((None, 2 * CHUNK, GROUP_W), lambda b_, s: (b_, s, 0))
    big = jax.ShapeDtypeStruct((b, nb * CHUNK, GROUP_W), BF16)
    big2 = jax.ShapeDtypeStruct((b, nb * 2 * CHUNK, GROUP_W), BF16)
    return pl.pallas_call(
        _gdn_prep_kernel,
        grid=(b, nb),
        in_specs=[
            tok(3), tok(4), tok(5), halo(3), halo(4), halo(5), lead(3), lead(4), lead(5),
            pl.BlockSpec((None, CHUNK, HEAD_W), lambda b_, s: (b_, jnp.maximum(s - 1, 0), 0)),
            pl.BlockSpec((LEAD, HEAD_W), lambda b_, s: (0, 0)),
            cw(0), cw(1), cw(2), col, col,
        ],
        out_specs=[out, out2, out2,
                   pl.BlockSpec((None, None, 8 * HEADS, HEAD_W), lambda b_, s: (b_, s, 0, 0))],
        out_shape=[big, big2, big2,
                   jax.ShapeDtypeStruct((b, nb, 8 * HEADS, HEAD_W), F32)],
        scratch_shapes=[pltpu.VMEM((8 + CHUNK, 3 * GROUP_W), F32),
                        pltpu.VMEM((CHUNK, HEAD_W), F32)],
        compiler_params=pltpu.CompilerParams(
            dimension_semantics=("parallel", "arbitrary"), vmem_limit_bytes=VMEM_LIMIT),
        name="gdn_prep",
    )(proj3, proj3, proj3, proj3, proj3, proj3, lead_proj, lead_proj, lead_proj,
      gates3, lead_gates, conv_wt, conv_wt, conv_wt, alog_col, dtb_col)


def _gdn_seq_kernel(u_ref, wq_ref, qkd_ref, dec_ref, z_ref, nw_ref, o_ref, s_ref):
    s = pl.program_id(1)

    @pl.when(s == 0)
    def _():
        s_ref[...] = jnp.zeros_like(s_ref)

    heads = range(HEADS)
    cols = [slice(h * HEAD_W, (h + 1) * HEAD_W) for h in heads]
    states = [s_ref[h] for h in heads]
    ws_qs = [_dot(wq_ref[:, cols[h]], states[h].astype(BF16)) for h in heads]
    v16 = [(u_ref[:, cols[h]].astype(F32) - ws_qs[h][:CHUNK]).astype(BF16) for h in heads]
    qv_kv = [_dot(qkd_ref[:, cols[h]], v16[h]) for h in heads]
    for h in heads:
        s_ref[h] = dec_ref[h * 8:h * 8 + 1, :] * states[h] + qv_kv[h][CHUNK:]
        o = ws_qs[h][CHUNK:] + qv_kv[h][:CHUNK]
        z = z_ref[:, cols[h]].astype(F32)
        o_ref[:, cols[h]] = (
            _rms_rows(o, nw_ref[...]) * (z * jax.nn.sigmoid(z))).astype(o_ref.dtype)


def _gdn_seq(u, wq, qkd, dec, proj3, onw):
    b, seq, _ = proj3.shape
    nb = 1 + seq // CHUNK
    blk = pl.BlockSpec((None, CHUNK, GROUP_W), lambda b_, s: (b_, s, 0))
    blk2 = pl.BlockSpec((None, 2 * CHUNK, GROUP_W), lambda b_, s: (b_, s, 0))
    return pl.pallas_call(
        _gdn_seq_kernel,
        grid=(b, nb),
        in_specs=[
            blk, blk2, blk2,
            pl.BlockSpec((None, None, 8 * HEADS, HEAD_W), lambda b_, s: (b_, s, 0, 0)),
            pl.BlockSpec((None, CHUNK, GROUP_W), lambda b_, s: (b_, jnp.maximum(s - 1, 0), 6)),
            pl.BlockSpec((1, HEAD_W), lambda b_, s: (0, 0)),
        ],
        out_specs=pl.BlockSpec((None, CHUNK, GROUP_W), lambda b_, s: (b_, jnp.maximum(s - 1, 0), 0)),
        out_shape=jax.ShapeDtypeStruct((b, seq, GROUP_W), BF16),
        scratch_shapes=[pltpu.VMEM((HEADS, HEAD_W, HEAD_W), F32)],
        compiler_params=pltpu.CompilerParams(
            dimension_semantics=("parallel", "arbitrary"), vmem_limit_bytes=VMEM_LIMIT),
        name="gdn_seq",
    )(u, wq, qkd, dec, proj3, onw)


def _outproj_kernel(x_ref, oa_ref, od_ref, wa_ref, wd_ref, o_ref):
    o_ref[...] = x_ref[...] + _dot(oa_ref[...], wa_ref[...]) + _dot(od_ref[...], wd_ref[...])


def _outproj(x2d, oa, od, w_out16, tm):
    m = x2d.shape[0]
    assert m % tm == 0
    return pl.pallas_call(
        _outproj_kernel,
        grid=(m // tm,),
        in_specs=[
            pl.BlockSpec((tm, D_MODEL), lambda i: (i, 0)),
            pl.BlockSpec((tm, GROUP_W), lambda i: (i, 0)),
            pl.BlockSpec((tm, GROUP_W), lambda i: (i, 0)),
            pl.BlockSpec((GROUP_W, D_MODEL), lambda i: (0, 0)),
            pl.BlockSpec((GROUP_W, D_MODEL), lambda i: (1, 0)),
        ],
        out_specs=pl.BlockSpec((tm, D_MODEL), lambda i: (i, 0)),
        out_shape=jax.ShapeDtypeStruct((m, D_MODEL), F32),
        compiler_params=pltpu.CompilerParams(
            dimension_semantics=("parallel",), vmem_limit_bytes=VMEM_LIMIT),
        name="outproj",
    )(x2d, oa, od, w_out16, w_out16)


def _ffn_kernel(h_ref, nw_ref, wg_ref, wu_ref, wd_ref, o_ref, u_ref, *, row_chunk):
    j = pl.program_id(1)

    @pl.when(j == 0)
    def _():
        def body(c, carry):
            r = pl.multiple_of(c * row_chunk, row_chunk)
            x = h_ref[pl.ds(r, row_chunk), :]
            u_ref[pl.ds(r, row_chunk), :] = _rms_rows(x, nw_ref[...]).astype(BF16)
            o_ref[pl.ds(r, row_chunk), :] = x
            return carry

        lax.fori_loop(0, h_ref.shape[0] // row_chunk, body, 0)

    u = u_ref[...]
    g = _dot(u, wg_ref[...])
    a = (g * jax.nn.sigmoid(g) * _dot(u, wu_ref[...])).astype(BF16)
    o_ref[...] += _dot(a, wd_ref[...])


def _ffn(h2d, norm_w, wg, wu, wd, tm, th):
    m = h2d.shape[0]
    assert m % tm == 0 and FFN_HIDDEN % th == 0
    return pl.pallas_call(
        functools.partial(_ffn_kernel, row_chunk=min(256, tm)),
        grid=(m // tm, FFN_HIDDEN // th),
        in_specs=[
            pl.BlockSpec((tm, D_MODEL), lambda i, j: (i, 0)),
            pl.BlockSpec((1, D_MODEL), lambda i, j: (0, 0)),
            pl.BlockSpec((D_MODEL, th), lambda i, j: (0, j)),
            pl.BlockSpec((D_MODEL, th), lambda i, j: (0, j)),
            pl.BlockSpec((th, D_MODEL), lambda i, j: (j, 0)),
        ],
        out_specs=pl.BlockSpec((tm, D_MODEL), lambda i, j: (i, 0)),
        out_shape=jax.ShapeDtypeStruct((m, D_MODEL), F32),
        scratch_shapes=[pltpu.VMEM((tm, D_MODEL), BF16)],
        compiler_params=pltpu.CompilerParams(
            dimension_semantics=("parallel", "arbitrary"), vmem_limit_bytes=VMEM_LIMIT),
        name="ffn",
    )(h2d, norm_w, wg, wu, wd)


def kernel(x, meta_tokens, attn_norm_w, w_in, q_norm_w, k_norm_w, lambda_q1, lambda_k1, lambda_q2,
           lambda_k2, subln_w, conv_w, a_log, dt_bias, o_norm_w, w_out, ffn_norm_w, w_gate, w_up,
           w_down):
    b, seq, _ = x.shape
    m = b * seq
    x2d = x.reshape(m, D_MODEL)
    lead = jnp.concatenate([jnp.zeros((N_PAD, D_MODEL), x.dtype), meta_tokens.astype(x.dtype)], 0)

    w_main = w_in[0, :, :MAIN_COLS].astype(BF16)
    w_gates = jnp.pad(w_in[0, :, MAIN_COLS:], ((0, 0), (0, HEAD_W - GATE_COLS))).astype(BF16)
    tm = min(1024, m)
    proj, gates = _inproj(x2d, attn_norm_w, w_main, w_gates, tm, 1024)
    lead_proj, lead_gates = _inproj(lead, attn_norm_w, w_main, w_gates, LEAD, 1024)
    proj3 = proj.reshape(b, seq, MAIN_COLS)

    slopes = 2.0 ** (-8.0 * jnp.arange(1, HEADS + 1, dtype=F32) / HEADS)
    lvec = jnp.concatenate([lambda_q1, lambda_k1, lambda_q2, lambda_k2], 0).astype(F32)
    o_a = _attn(proj3, lead_proj, slopes, lvec, jnp.tile(q_norm_w, (1, 2)),
                jnp.tile(k_norm_w, (1, 2)), subln_w.reshape(HEAD_W, 1), min(512, seq))

    u, wq, qkd, dec = _gdn_prep(
        proj3, lead_proj, gates.reshape(b, seq, HEAD_W), lead_gates, conv_w[0].T,
        a_log.astype(F32).reshape(HEADS, 1), dt_bias.astype(F32).reshape(HEADS, 1))
    o_d = _gdn_seq(u, wq, qkd, dec, proj3, o_norm_w)

    h1 = _outproj(x2d, o_a.reshape(m, GROUP_W), o_d.reshape(m, GROUP_W), w_out[0].astype(BF16),
                  min(512, m))
    out = _ffn(h1, ffn_norm_w, w_gate[0].astype(BF16), w_up[0].astype(BF16),
               w_down[0].astype(BF16), min(1024, m), 512)
    return out.reshape(b, seq, D_MODEL)
```

```python
import functools

import jax
import jax.numpy as jnp
import numpy as np
from jax import lax
from jax.experimental import pallas as pl
from jax.experimental.pallas import tpu as pltpu

F32 = jnp.float32
BF16 = jnp.bfloat16
HIGHEST = lax.Precision.HIGHEST

D_MODEL = 2048
N_META = 16
LEAD = 128
N_PAD = LEAD - N_META
HEADS = 8
HEAD_W = 128
QK_DIM = 64
GROUP_W = HEADS * HEAD_W
MAIN_COLS = 7 * GROUP_W
GATE_COLS = 2 * HEADS
CONV_K = 4
FFN_HIDDEN = 5632
EPS = 1e-6
NEG = -1e30
LAMBDA_INIT = 0.2
CHUNK = 128
VMEM_LIMIT = 56 * 1024 * 1024
LOG2E = 1.4426950408889634


def _bf16_pieces(x, n):
    out = []
    for _ in range(n):
        bits = np.array(x, np.float32).view(np.uint32)
        bits = (bits + 0x7FFF + ((bits >> 16) & 1)) & 0xFFFF0000
        p = float(bits.view(np.float32))
        out.append(p)
        x -= p
    return tuple(out)


LOG2E_BF16_PIECES = _bf16_pieces(LOG2E, 3)


def _dot(a, b, precision=None):
    return jnp.dot(a, b, preferred_element_type=F32, precision=precision)


def _dot_nt(a, b):
    return lax.dot_general(a, b, (((1,), (1,)), ((), ())), preferred_element_type=F32)


def _split(x):
    hi = x.astype(BF16)
    return hi, (x - hi.astype(F32)).astype(BF16)


def _dot3(a, b):
    a_hi, a_lo = a
    if isinstance(b, list):
        b_hi = jnp.concatenate([x[0] for x in b], axis=1)
        b_lo = jnp.concatenate([x[1] for x in b], axis=1)
    else:
        b_hi, b_lo = b
    rows = a_hi.shape[0]
    r = _dot(jnp.concatenate([a_hi, a_lo], axis=0), b_hi)
    return r[:rows] + r[rows:] + _dot(a_hi, b_lo)


def _rms_rows(x, w_row):
    return x * lax.rsqrt(jnp.mean(x * x, axis=-1, keepdims=True) + EPS) * w_row


def _inproj_kernel(x_ref, nw_ref, w_ref, wg_ref, o_ref, g_ref, u_ref, *, row_chunk):
    j = pl.program_id(1)

    @pl.when(j == 0)
    def _():
        def body(c, carry):
            r = pl.multiple_of(c * row_chunk, row_chunk)
            u = _rms_rows(x_ref[pl.ds(r, row_chunk), :], nw_ref[...]).astype(BF16)
            u_ref[pl.ds(r, row_chunk), :] = u
            g_ref[pl.ds(r, row_chunk), :] = _dot(u, wg_ref[...])
            return carry

        lax.fori_loop(0, x_ref.shape[0] // row_chunk, body, 0)

    o_ref[...] = _dot(u_ref[...], w_ref[...]).astype(o_ref.dtype)


def _inproj(x2d, norm_w, w_main, w_gate, tm, tn):
    m = x2d.shape[0]
    assert m % tm == 0 and MAIN_COLS % tn == 0
    row_chunk = min(256, tm)
    return pl.pallas_call(
        functools.partial(_inproj_kernel, row_chunk=row_chunk),
        grid=(m // tm, MAIN_COLS // tn),
        in_specs=[
            pl.BlockSpec((tm, D_MODEL), lambda i, j: (i, 0)),
            pl.BlockSpec((1, D_MODEL), lambda i, j: (0, 0)),
            pl.BlockSpec((D_MODEL, tn), lambda i, j: (0, j)),
            pl.BlockSpec((D_MODEL, HEAD_W), lambda i, j: (0, 0)),
        ],
        out_specs=[
            pl.BlockSpec((tm, tn), lambda i, j: (i, j)),
            pl.BlockSpec((tm, HEAD_W), lambda i, j: (i, 0)),
        ],
        out_shape=[
            jax.ShapeDtypeStruct((m, MAIN_COLS), BF16),
            jax.ShapeDtypeStruct((m, HEAD_W), F32),
        ],
        scratch_shapes=[pltpu.VMEM((tm, D_MODEL), BF16)],
        compiler_params=pltpu.CompilerParams(
            dimension_semantics=("parallel", "arbitrary"), vmem_limit_bytes=VMEM_LIMIT),
        name="inproj",
    )(x2d, norm_w, w_main, w_gate)


def _halfnorm(x, w_row):
    lo = lax.broadcasted_iota(jnp.int32, x.shape, 1) < QK_DIM
    x2 = x * x
    s_lo = jnp.sum(jnp.where(lo, x2, 0.0), axis=-1, keepdims=True)
    s_hi = jnp.sum(jnp.where(lo, 0.0, x2), axis=-1, keepdims=True)
    ms = jnp.where(lo, s_lo, s_hi) * (1.0 / QK_DIM)
    return x * lax.rsqrt(ms + EPS) * w_row


def _attn_kernel(slopes_ref, lvec_ref, q_ref, qnext_ref, k_ref, v_ref, lk_ref, lv_ref, qw_ref,
                 kw_ref, swc_ref, o_ref, kn_ref, vt_ref, lkn_ref, lvt_ref, kaug_ref, acc_ref,
                 sa_ref, sb_ref, wn_ref, wc_ref, *, tq, seq):
    h = pl.program_id(1)
    i = pl.program_id(2)
    slope = slopes_ref[h]

    def query_operands(q_blk):
        lo = lax.broadcasted_iota(jnp.int32, (tq, HEAD_W), 1) < QK_DIM
        qn = _halfnorm(q_blk.astype(F32), qw_ref[...]) * (QK_DIM ** -0.5 * LOG2E)
        sub = lax.broadcasted_iota(jnp.int32, (HEAD_W, tq), 0)
        aug = jnp.zeros((HEAD_W, tq), F32)
        for n, piece in enumerate(LOG2E_BF16_PIECES):
            aug = jnp.where((sub == n) | (sub == n + 3), piece, aug)
        return tuple(jnp.concatenate([x.T, aug], axis=0).astype(BF16)
                     for x in (jnp.where(lo, qn, 0.0), jnp.where(lo, 0.0, qn)))

    @pl.when(i == 0)
    def _():
        w0 = query_operands(q_ref[...])
        wn_ref[0] = w0[0]
        wn_ref[1] = w0[1]

        def body(c, carry):
            r = pl.multiple_of(c * 256, 256)
            kn_ref[pl.ds(r, 256), :] = _halfnorm(
                k_ref[pl.ds(r, 256), :].astype(F32), kw_ref[...]).astype(BF16)
            vt_ref[:, pl.ds(r, 256)] = v_ref[pl.ds(r, 256), :].astype(F32).T.astype(BF16)
            return carry

        lax.fori_loop(0, seq // 256, body, 0)
        lkn_ref[...] = _halfnorm(lk_ref[...].astype(F32), kw_ref[...]).astype(BF16)
        lvt_ref[...] = lv_ref[...].astype(F32).T.astype(BF16)
        kk = lax.broadcasted_iota(jnp.int32, (tq, HEAD_W), 0)
        ln = lax.broadcasted_iota(jnp.int32, (tq, HEAD_W), 1)
        hi = ((kk // 16) * 16).astype(F32)
        lo_ = (kk % 16).astype(F32)
        kaug_ref[...] = (slope * jnp.where(ln < 3, hi, jnp.where(ln < 6, lo_, 0.0))).astype(BF16)

    wc_ref[...] = wn_ref[...]
    w_next = query_operands(qnext_ref[...])
    wn_ref[0] = w_next[0]
    wn_ref[1] = w_next[1]

    slope2 = slope * LOG2E
    q_off = slope2 * lax.broadcasted_iota(jnp.int32, (1, tq), 1).astype(F32)
    key_ok = lax.broadcasted_iota(jnp.int32, (LEAD, tq), 0) >= N_PAD
    ms, ls = [], []
    for mp in range(2):
        s = jnp.where(key_ok, _dot(lkn_ref[...], wc_ref[mp, 0:HEAD_W, :]) + q_off, NEG)
        m = jnp.max(s, axis=0, keepdims=True)
        p = jnp.exp2(s - m)
        ms.append(m)
        ls.append(jnp.sum(p, axis=0, keepdims=True))
        acc_ref[mp] = _dot(lvt_ref[...], p.astype(BF16))

    key_i = lax.broadcasted_iota(jnp.int32, (tq, tq), 0)
    qry_i = lax.broadcasted_iota(jnp.int32, (tq, tq), 1)

    def scores(j, dst_ref, diag):
        r = pl.multiple_of(j * tq, tq)
        lhs = jnp.concatenate([kn_ref[pl.ds(r, tq), :], kaug_ref[...]], axis=1)
        bms = []
        for mp in range(2):
            raw = _dot(lhs, wc_ref[mp])
            if diag:
                raw = jnp.where(key_i <= qry_i, raw, NEG)
            dst_ref[mp] = raw
            bms.append(jnp.max(raw, axis=0, keepdims=True))
        return tuple(bms)

    def accumulate(j, src_ref, bms, carry):
        r = pl.multiple_of(j * tq, tq)
        vt = vt_ref[:, pl.ds(r, tq)]
        c = slope2 * ((j - i) * tq).astype(F32)
        out = []
        for mp in range(2):
            m, l = carry[2 * mp], carry[2 * mp + 1]
            m_new = jnp.maximum(m, bms[mp] + c)
            alpha = jnp.exp2(m - m_new)
            p = jnp.exp2(src_ref[mp] - (m_new - c))
            out += [m_new, alpha * l + jnp.sum(p, axis=0, keepdims=True)]
            acc_ref[mp] = alpha * acc_ref[mp] + _dot(vt, p.astype(BF16))
        return tuple(out)

    carry = (ms[0], ls[0], ms[1], ls[1])
    bm_a = scores(i, sa_ref, True)

    def pair(t, state):
        j_a, bm_a, carry = state
        bm_b = scores(2 * t, sb_ref, False)
        carry = accumulate(j_a, sa_ref, bm_a, carry)
        bm_a = scores(2 * t + 1, sa_ref, False)
        carry = accumulate(2 * t, sb_ref, bm_b, carry)
        return 2 * t + 1, bm_a, carry

    j_a, bm_a, carry = lax.fori_loop(0, i // 2, pair, (i, bm_a, carry))

    def odd_tail(carry):
        bm_b = scores(i - 1, sb_ref, False)
        carry = accumulate(j_a, sa_ref, bm_a, carry)
        return accumulate(i - 1, sb_ref, bm_b, carry)

    carry = lax.cond(i % 2 == 1, odd_tail, lambda c: accumulate(j_a, sa_ref, bm_a, c), carry)

    lv4 = lvec_ref[...]
    lam = (jnp.exp(jnp.sum(lv4[0:1] * lv4[1:2], axis=-1, keepdims=True))
           - jnp.exp(jnp.sum(lv4[2:3] * lv4[3:4], axis=-1, keepdims=True)) + LAMBDA_INIT)
    o = acc_ref[0] / carry[1] - lam * (acc_ref[1] / carry[3])
    o = o * lax.rsqrt(jnp.mean(o * o, axis=0, keepdims=True) + EPS) * swc_ref[...]
    o_ref[...] = (o * (1.0 - LAMBDA_INIT)).T.astype(o_ref.dtype)


def _attn(proj3, lead_proj, slopes, lvec, qw, kw, sw, tq):
    b, seq, _ = proj3.shape
    assert seq % tq == 0 and seq % 256 == 0 and tq % 16 == 0 and tq <= 512
    return pl.pallas_call(
        functools.partial(_attn_kernel, tq=tq, seq=seq),
        grid=(b, HEADS, seq // tq),
        in_specs=[
            pl.BlockSpec(memory_space=pltpu.SMEM),
            pl.BlockSpec((4, QK_DIM), lambda b_, h, i: (0, 0)),
            pl.BlockSpec((None, tq, HEAD_W), lambda b_, h, i: (b_, i, h)),
            pl.BlockSpec((None, tq, HEAD_W),
                         lambda b_, h, i: (b_, jnp.minimum(i + 1, seq // tq - 1), h)),
            pl.BlockSpec((None, seq, HEAD_W), lambda b_, h, i: (b_, 0, HEADS + h)),
            pl.BlockSpec((None, seq, HEAD_W), lambda b_, h, i: (b_, 0, 2 * HEADS + h)),
            pl.BlockSpec((LEAD, HEAD_W), lambda b_, h, i: (0, HEADS + h)),
            pl.BlockSpec((LEAD, HEAD_W), lambda b_, h, i: (0, 2 * HEADS + h)),
            pl.BlockSpec((1, HEAD_W), lambda b_, h, i: (0, 0)),
            pl.BlockSpec((1, HEAD_W), lambda b_, h, i: (0, 0)),
            pl.BlockSpec((HEAD_W, 1), lambda b_, h, i: (0, 0)),
        ],
        out_specs=pl.BlockSpec((None, tq, HEAD_W), lambda b_, h, i: (b_, i, h)),
        out_shape=jax.ShapeDtypeStruct((b, seq, GROUP_W), BF16),
        scratch_shapes=[
            pltpu.VMEM((seq, HEAD_W), BF16),
            pltpu.VMEM((HEAD_W, seq), BF16),
            pltpu.VMEM((LEAD, HEAD_W), BF16),
            pltpu.VMEM((HEAD_W, LEAD), BF16),
            pltpu.VMEM((tq, HEAD_W), BF16),
            pltpu.VMEM((2, HEAD_W, tq), F32),
            pltpu.VMEM((2, tq, tq), F32),
            pltpu.VMEM((2, tq, tq), F32),
            pltpu.VMEM((2, 2 * HEAD_W, tq), BF16),
            pltpu.VMEM((2, 2 * HEAD_W, tq), BF16),
        ],
        compiler_params=pltpu.CompilerParams(
            dimension_semantics=("parallel", "parallel", "arbitrary"),
            vmem_limit_bytes=VMEM_LIMIT),
        name="diff_attn",
    )(slopes, lvec, proj3, proj3, proj3, proj3, lead_proj, lead_proj, qw, kw, sw)


def _gdn_kernel(tq_ref, tk_ref, tv_ref, hq_ref, hk_ref, hv_ref, lq_ref, lk_ref, lv_ref,
                tg_ref, lg_ref, cwq_ref, cwk_ref, cwv_ref, alog_ref, dtb_ref, z_ref, nw_ref,
                o_ref, xs_ref, gs_ref, s_ref):
    s = pl.program_id(1)
    is_lead = s == 0
    width = GROUP_W
    srcs = ((tq_ref, hq_ref, lq_ref), (tk_ref, hk_ref, lk_ref), (tv_ref, hv_ref, lv_ref))

    @pl.when(is_lead)
    def _():
        rowid = lax.broadcasted_iota(jnp.int32, (CHUNK, width), 0)
        for idx, (_, _, l_ref) in enumerate(srcs):
            cs = slice(idx * width, (idx + 1) * width)
            xs_ref[0:8, cs] = jnp.zeros((8, width), F32)
            xs_ref[8:8 + CHUNK, cs] = jnp.where(rowid >= N_PAD, l_ref[...].astype(F32), 0.0)
        gs_ref[...] = lg_ref[...]
        s_ref[...] = jnp.zeros_like(s_ref)

    @pl.when(s == 1)
    def _():
        for idx, (t_ref, _, l_ref) in enumerate(srcs):
            cs = slice(idx * width, (idx + 1) * width)
            xs_ref[0:8, cs] = l_ref[LEAD - 16:LEAD, :].astype(F32)[8:16]
            xs_ref[8:8 + CHUNK, cs] = t_ref[...].astype(F32)
        gs_ref[...] = tg_ref[...]

    @pl.when(s > 1)
    def _():
        for idx, (t_ref, h_ref, _) in enumerate(srcs):
            cs = slice(idx * width, (idx + 1) * width)
            xs_ref[0:8, cs] = h_ref[...].astype(F32)[8:16]
            xs_ref[8:8 + CHUNK, cs] = t_ref[...].astype(F32)
        gs_ref[...] = tg_ref[...]

    rowi = lax.broadcasted_iota(jnp.int32, (CHUNK, CHUNK), 0)
    lanei = lax.broadcasted_iota(jnp.int32, (CHUNK, CHUNK), 1)
    incl = rowi >= lanei
    eye = (rowi == lanei).astype(F32)

    g_t = gs_ref[...].T[0:2 * HEADS]
    vmask = (lax.broadcasted_iota(jnp.int32, (HEADS, CHUNK), 1)
             >= jnp.where(is_lead, N_PAD, 0)).astype(F32)
    beta_t = jax.nn.sigmoid(g_t[0:HEADS]) * vmask
    t = g_t[HEADS:] + dtb_ref[...]
    softplus = jnp.maximum(t, 0.0) + jnp.log(1.0 + jnp.exp(-jnp.abs(t)))
    decay_t = -jnp.exp(alog_ref[...]) * softplus * vmask
    gc_t = _dot(decay_t, (rowi <= lanei).astype(F32), HIGHEST)
    cols = jnp.concatenate(
        [beta_t, gc_t, jnp.zeros((CHUNK - 2 * HEADS, CHUNK), F32)], axis=0).T

    heads = range(HEADS)
    hcols = [slice(h * HEAD_W, (h + 1) * HEAD_W) for h in heads]
    mks, pks, rhs, qgs, qkds, decs = [], [], [], [], [], []
    for hh in heads:
        hs = hcols[hh]
        beta = cols[:, hh:hh + 1]
        gc = cols[:, HEADS + hh:HEADS + hh + 1]
        gc_row = gc_t[hh:hh + 1]
        g_last = gc_row[:, CHUNK - 1:CHUNK]

        def conv_silu(idx, cw_ref):
            c0 = idx * width + hh * HEAD_W
            y = xs_ref[5:5 + CHUNK, c0:c0 + HEAD_W] * cw_ref[0:1, hs]
            for j in range(1, CONV_K):
                y = y + xs_ref[5 + j:5 + j + CHUNK, c0:c0 + HEAD_W] * cw_ref[j:j + 1, hs]
            return y * jax.nn.sigmoid(y)

        q = conv_silu(0, cwq_ref)
        k = conv_silu(1, cwk_ref)
        v = conv_silu(2, cwv_ref)
        q = q * lax.rsqrt(jnp.sum(q * q, axis=-1, keepdims=True) + EPS) * (HEAD_W ** -0.5)
        k = k * lax.rsqrt(jnp.sum(k * k, axis=-1, keepdims=True) + EPS)

        decay = jnp.where(incl, jnp.exp(jnp.where(incl, gc - gc_row, 0.0)), 0.0)
        kb = k * beta
        k16 = k.astype(BF16)
        lmat = jnp.where(rowi > lanei, _dot_nt(kb.astype(BF16), k16) * decay, 0.0)
        qkds.append(jnp.concatenate(
            [(_dot_nt(q.astype(BF16), k16) * decay).astype(BF16),
             (k * jnp.exp(g_last - gc)).T.astype(BF16)], axis=0))
        qgs.append((q * jnp.exp(gc)).astype(BF16))
        decs.append(jnp.exp(g_last))
        mks.append(-lmat)
        pks.append(eye - lmat)
        rhs.append(jnp.concatenate([v * beta, kb * jnp.exp(gc)], axis=1))

    ms = [_split(m) for m in mks]
    mks = [_dot3(m, m) for m in ms]
    for _ in range(5):
        ms = [_split(m) for m in mks]
        rs = [_dot3(m, [_split(p), m]) for m, p in zip(ms, pks)]
        pks = [p + r[:, :CHUNK] for p, r in zip(pks, rs)]
        mks = [r[:, CHUNK:] for r in rs]
    pks = [p + _dot3(_split(m), _split(p)) for m, p in zip(mks, pks)]
    uws = [_dot3(_split(p), _split(r)) for p, r in zip(pks, rhs)]

    states = [s_ref[h] for h in heads]
    ws_qs = [_dot(jnp.concatenate([uws[h][:, HEAD_W:].astype(BF16), qgs[h]], axis=0),
                  states[h].astype(BF16)) for h in heads]
    v16 = [(uws[h][:, :HEAD_W] - ws_qs[h][:CHUNK]).astype(BF16) for h in heads]
    qv_kv = [_dot(qkds[h], v16[h]) for h in heads]
    for h in heads:
        s_ref[h] = decs[h] * states[h] + qv_kv[h][CHUNK:]
        o = ws_qs[h][CHUNK:] + qv_kv[h][:CHUNK]
        z = z_ref[:, hcols[h]].astype(F32)
        o_ref[:, hcols[h]] = (
            _rms_rows(o, nw_ref[...]) * (z * jax.nn.sigmoid(z))).astype(o_ref.dtype)


def _gdn(proj3, lead_proj, gates3, lead_gates, conv_wt, alog_col, dtb_col, onw):
    b, seq, _ = proj3.shape
    nb = 1 + seq // CHUNK
    tok = lambda g: pl.BlockSpec(
        (None, CHUNK, GROUP_W), lambda b_, s: (b_, jnp.maximum(s - 1, 0), g))
    halo = lambda g: pl.BlockSpec(
        (None, 16, GROUP_W),
        lambda b_, s: (b_, jnp.maximum((s - 1) * (CHUNK // 16) - 1, 0), g))
    lead = lambda g: pl.BlockSpec((LEAD, GROUP_W), lambda b_, s: (0, g))
    cw = lambda g: pl.BlockSpec((CONV_K, GROUP_W), lambda b_, s: (0, g))
    col = pl.BlockSpec((HEADS, 1), lambda b_, s: (0, 0))
    return pl.pallas_call(
        _gdn_kernel,
        grid=(b, nb),
        in_specs=[
            tok(3), tok(4), tok(5), halo(3), halo(4), halo(5), lead(3), lead(4), lead(5),
            pl.BlockSpec((None, CHUNK, HEAD_W), lambda b_, s: (b_, jnp.maximum(s - 1, 0), 0)),
            pl.BlockSpec((LEAD, HEAD_W), lambda b_, s: (0, 0)),
            cw(0), cw(1), cw(2), col, col, tok(6),
            pl.BlockSpec((1, HEAD_W), lambda b_, s: (0, 0)),
        ],
        out_specs=pl.BlockSpec((None, CHUNK, GROUP_W), lambda b_, s: (b_, jnp.maximum(s - 1, 0), 0)),
        out_shape=jax.ShapeDtypeStruct((b, seq, GROUP_W), BF16),
        scratch_shapes=[pltpu.VMEM((8 + CHUNK, 3 * GROUP_W), F32),
                        pltpu.VMEM((CHUNK, HEAD_W), F32),
                        pltpu.VMEM((HEADS, HEAD_W, HEAD_W), F32)],
        compiler_params=pltpu.CompilerParams(
            dimension_semantics=("parallel", "arbitrary"), vmem_limit_bytes=VMEM_LIMIT),
        name="gdn",
    )(proj3, proj3, proj3, proj3, proj3, proj3, lead_proj, lead_proj, lead_proj,
      gates3, lead_gates, conv_wt, conv_wt, conv_wt, alog_col, dtb_col, proj3, onw)


def _outproj_kernel(x_ref, oa_ref, od_ref, wa_ref, wd_ref, o_ref):
    o_ref[...] = x_ref[...] + _dot(oa_ref[...], wa_ref[...]) + _dot(od_ref[...], wd_ref[...])


def _outproj(x2d, oa, od, w_out16, tm):
    m = x2d.shape[0]
    assert m % tm == 0
    return pl.pallas_call(
        _outproj_kernel,
        grid=(m // tm,),
        in_specs=[
            pl.BlockSpec((tm, D_MODEL), lambda i: (i, 0)),
            pl.BlockSpec((tm, GROUP_W), lambda i: (i, 0)),
            pl.BlockSpec((tm, GROUP_W), lambda i: (i, 0)),
            pl.BlockSpec((GROUP_W, D_MODEL), lambda i: (0, 0)),
            pl.BlockSpec((GROUP_W, D_MODEL), lambda i: (1, 0)),
        ],
        out_specs=pl.BlockSpec((tm, D_MODEL), lambda i: (i, 0)),
        out_shape=jax.ShapeDtypeStruct((m, D_MODEL), F32),
        compiler_params=pltpu.CompilerParams(
            dimension_semantics=("parallel",), vmem_limit_bytes=VMEM_LIMIT),
        name="outproj",
    )(x2d, oa, od, w_out16, w_out16)


def _ffn_kernel(h_ref, nw_ref, wg_ref, wu_ref, wd_ref, o_ref, u_ref, *, row_chunk):
    j = pl.program_id(1)

    @pl.when(j == 0)
    def _():
        def body(c, carry):
            r = pl.multiple_of(c * row_chunk, row_chunk)
            x = h_ref[pl.ds(r, row_chunk), :]
            u_ref[pl.ds(r, row_chunk), :] = _rms_rows(x, nw_ref[...]).astype(BF16)
            o_ref[pl.ds(r, row_chunk), :] = x
            return carry

        lax.fori_loop(0, h_ref.shape[0] // row_chunk, body, 0)

    u = u_ref[...]
    g = _dot(u, wg_ref[...])
    a = (g * jax.nn.sigmoid(g) * _dot(u, wu_ref[...])).astype(BF16)
    o_ref[...] += _dot(a, wd_ref[...])


def _ffn(h2d, norm_w, wg, wu, wd, tm, th):
    m = h2d.shape[0]
    assert m % tm == 0 and FFN_HIDDEN % th == 0
    return pl.pallas_call(
        functools.partial(_ffn_kernel, row_chunk=min(256, tm)),
        grid=(m // tm, FFN_HIDDEN // th),
        in_specs=[
            pl.BlockSpec((tm, D_MODEL), lambda i, j: (i, 0)),
            pl.BlockSpec((1, D_MODEL), lambda i, j: (0, 0)),
            pl.BlockSpec((D_MODEL, th), lambda i, j: (0, j)),
            pl.BlockSpec((D_MODEL, th), lambda i, j: (0, j)),
            pl.BlockSpec((th, D_MODEL), lambda i, j: (j, 0)),
        ],
        out_specs=pl.BlockSpec((tm, D_MODEL), lambda i, j: (i, 0)),
        out_shape=jax.ShapeDtypeStruct((m, D_MODEL), F32),
        scratch_shapes=[pltpu.VMEM((tm, D_MODEL), BF16)],
        compiler_params=pltpu.CompilerParams(
            dimension_semantics=("parallel", "arbitrary"), vmem_limit_bytes=VMEM_LIMIT),
        name="ffn",
    )(h2d, norm_w, wg, wu, wd)


def kernel(x, meta_tokens, attn_norm_w, w_in, q_norm_w, k_norm_w, lambda_q1, lambda_k1, lambda_q2,
           lambda_k2, subln_w, conv_w, a_log, dt_bias, o_norm_w, w_out, ffn_norm_w, w_gate, w_up,
           w_down):
    b, seq, _ = x.shape
    m = b * seq
    x2d = x.reshape(m, D_MODEL)
    lead = jnp.concatenate([jnp.zeros((N_PAD, D_MODEL), x.dtype), meta_tokens.astype(x.dtype)], 0)

    w_main = w_in[0, :, :MAIN_COLS].astype(BF16)
    w_gates = jnp.pad(w_in[0, :, MAIN_COLS:], ((0, 0), (0, HEAD_W - GATE_COLS))).astype(BF16)
    tm = min(1024, m)
    proj, gates = _inproj(x2d, attn_norm_w, w_main, w_gates, tm, 1024)
    lead_proj, lead_gates = _inproj(lead, attn_norm_w, w_main, w_gates, LEAD, 1024)
    proj3 = proj.reshape(b, seq, MAIN_COLS)

    slopes = 2.0 ** (-8.0 * jnp.arange(1, HEADS + 1, dtype=F32) / HEADS)
    lvec = jnp.concatenate([lambda_q1, lambda_k1, lambda_q2, lambda_k2], 0).astype(F32)
    o_a = _attn(proj3, lead_proj, slopes, lvec, jnp.tile(q_norm_w, (1, 2)),
                jnp.tile(k_norm_w, (1, 2)), subln_w.reshape(HEAD_W, 1), min(512, seq))

    o_d = _gdn(
        proj3, lead_proj, gates.reshape(b, seq, HEAD_W), lead_gates, conv_w[0].T,
        a_log.astype(F32).reshape(HEADS, 1), dt_bias.astype(F32).reshape(HEADS, 1), o_norm_w)

    h1 = _outproj(x2d, o_a.reshape(m, GROUP_W), o_d.reshape(m, GROUP_W), w_out[0].astype(BF16),
                  min(512, m))
    out = _ffn(h1, ffn_norm_w, w_gate[0].astype(BF16), w_up[0].astype(BF16),
               w_down[0].astype(BF16), min(1024, m), 512)
    return out.reshape(b, seq, D_MODEL)
```

```python
import functools

import jax
import jax.numpy as jnp
import numpy as np
from jax import lax
from jax.experimental import pallas as pl
from jax.experimental.pallas import tpu as pltpu

F32 = jnp.float32
BF16 = jnp.bfloat16
HIGHEST = lax.Precision.HIGHEST

D_MODEL = 2048
N_META = 16
LEAD = 128
N_PAD = LEAD - N_META
HEADS = 8
HEAD_W = 128
QK_DIM = 64
GROUP_W = HEADS * HEAD_W
MAIN_COLS = 7 * GROUP_W
GATE_COLS = 2 * HEADS
CONV_K = 4
FFN_HIDDEN = 5632
EPS = 1e-6
NEG = -1e30
LAMBDA_INIT = 0.2
CHUNK = 128
VMEM_LIMIT = 56 * 1024 * 1024
LOG2E = 1.4426950408889634


def _bf16_pieces(x, n):
    out = []
    for _ in range(n):
        bits = np.array(x, np.float32).view(np.uint32)
        bits = (bits + 0x7FFF + ((bits >> 16) & 1)) & 0xFFFF0000
        p = float(bits.view(np.float32))
        out.append(p)
        x -= p
    return tuple(out)


LOG2E_BF16_PIECES = _bf16_pieces(LOG2E, 3)


def _dot(a, b, precision=None):
    return jnp.dot(a, b, preferred_element_type=F32, precision=precision)


def _dot_nt(a, b):
    return lax.dot_general(a, b, (((1,), (1,)), ((), ())), preferred_element_type=F32)


def _split(x):
    hi = x.astype(BF16)
    return hi, (x - hi.astype(F32)).astype(BF16)


def _dot3(a, b):
    a_hi, a_lo = a
    if isinstance(b, list):
        b_hi = jnp.concatenate([x[0] for x in b], axis=1)
        b_lo = jnp.concatenate([x[1] for x in b], axis=1)
    else:
        b_hi, b_lo = b
    return _dot(jnp.concatenate([a_hi, a_lo, a_hi], axis=1),
                jnp.concatenate([b_hi, b_hi, b_lo], axis=0))


def _rms_rows(x, w_row):
    return x * lax.rsqrt(jnp.mean(x * x, axis=-1, keepdims=True) + EPS) * w_row


def _inproj_kernel(x_ref, nw_ref, w_ref, wg_ref, o_ref, g_ref, u_ref, *, row_chunk):
    j = pl.program_id(1)

    @pl.when(j == 0)
    def _():
        def body(c, carry):
            r = pl.multiple_of(c * row_chunk, row_chunk)
            u = _rms_rows(x_ref[pl.ds(r, row_chunk), :], nw_ref[...]).astype(BF16)
            u_ref[pl.ds(r, row_chunk), :] = u
            g_ref[pl.ds(r, row_chunk), :] = _dot(u, wg_ref[...])
            return carry

        lax.fori_loop(0, x_ref.shape[0] // row_chunk, body, 0)

    o_ref[...] = _dot(u_ref[...], w_ref[...]).astype(o_ref.dtype)


def _inproj(x2d, norm_w, w_main, w_gate, tm, tn):
    m = x2d.shape[0]
    assert m % tm == 0 and MAIN_COLS % tn == 0
    row_chunk = min(256, tm)
    return pl.pallas_call(
        functools.partial(_inproj_kernel, row_chunk=row_chunk),
        grid=(m // tm, MAIN_COLS // tn),
        in_specs=[
            pl.BlockSpec((tm, D_MODEL), lambda i, j: (i, 0)),
            pl.BlockSpec((1, D_MODEL), lambda i, j: (0, 0)),
            pl.BlockSpec((D_MODEL, tn), lambda i, j: (0, j)),
            pl.BlockSpec((D_MODEL, HEAD_W), lambda i, j: (0, 0)),
        ],
        out_specs=[
            pl.BlockSpec((tm, tn), lambda i, j: (i, j)),
            pl.BlockSpec((tm, HEAD_W), lambda i, j: (i, 0)),
        ],
        out_shape=[
            jax.ShapeDtypeStruct((m, MAIN_COLS), BF16),
            jax.ShapeDtypeStruct((m, HEAD_W), F32),
        ],
        scratch_shapes=[pltpu.VMEM((tm, D_MODEL), BF16)],
        compiler_params=pltpu.CompilerParams(
            dimension_semantics=("parallel", "arbitrary"), vmem_limit_bytes=VMEM_LIMIT),
        name="inproj",
    )(x2d, norm_w, w_main, w_gate)


def _halfnorm(x, w_row):
    lo = lax.broadcasted_iota(jnp.int32, x.shape, 1) < QK_DIM
    x2 = x * x
    s_lo = jnp.sum(jnp.where(lo, x2, 0.0), axis=-1, keepdims=True)
    s_hi = jnp.sum(jnp.where(lo, 0.0, x2), axis=-1, keepdims=True)
    ms = jnp.where(lo, s_lo, s_hi) * (1.0 / QK_DIM)
    return x * lax.rsqrt(ms + EPS) * w_row


def _attn_kernel(slopes_ref, lvec_ref, q_ref, qnext_ref, k_ref, v_ref, lk_ref, lv_ref, qw_ref,
                 kw_ref, swc_ref, o_ref, kn_ref, vt_ref, lkn_ref, lvt_ref, kaug_ref, acc_ref,
                 sa_ref, sb_ref, wn_ref, wc_ref, *, tq, seq):
    h = pl.program_id(1)
    g = pl.program_id(2)
    slope = slopes_ref[h]

    def query_operands(src_ref, dst_ref):
        lo = lax.broadcasted_iota(jnp.int32, (tq, HEAD_W), 1) < QK_DIM
        sub = lax.broadcasted_iota(jnp.int32, (HEAD_W, tq), 0)
        aug = jnp.zeros((HEAD_W, tq), F32)
        for n, piece in enumerate(LOG2E_BF16_PIECES):
            aug = jnp.where((sub == n) | (sub == n + 3), piece, aug)
        for blk in range(2):
            q_blk = src_ref[blk * tq:(blk + 1) * tq, :].astype(F32)
            qn = _halfnorm(q_blk, qw_ref[...]) * (QK_DIM ** -0.5 * LOG2E)
            for mp, x in enumerate((jnp.where(lo, qn, 0.0), jnp.where(lo, 0.0, qn))):
                dst_ref[blk, mp] = jnp.concatenate([x.T, aug], axis=0).astype(BF16)

    @pl.when(g == 0)
    def _():
        query_operands(q_ref, wn_ref)

        def body(c, carry):
            r = pl.multiple_of(c * 256, 256)
            kn_ref[pl.ds(r, 256), :] = _halfnorm(
                k_ref[pl.ds(r, 256), :].astype(F32), kw_ref[...]).astype(BF16)
            vt_ref[:, pl.ds(r, 256)] = v_ref[pl.ds(r, 256), :].astype(F32).T.astype(BF16)
            return carry

        lax.fori_loop(0, seq // 256, body, 0)
        lkn_ref[...] = _halfnorm(lk_ref[...].astype(F32), kw_ref[...]).astype(BF16)
        lvt_ref[...] = lv_ref[...].astype(F32).T.astype(BF16)
        kk = lax.broadcasted_iota(jnp.int32, (tq, HEAD_W), 0)
        ln = lax.broadcasted_iota(jnp.int32, (tq, HEAD_W), 1)
        hi = ((kk // 16) * 16).astype(F32)
        lo_ = (kk % 16).astype(F32)
        kaug_ref[...] = (slope * jnp.where(ln < 3, hi, jnp.where(ln < 6, lo_, 0.0))).astype(BF16)

    wc_ref[...] = wn_ref[...]
    query_operands(qnext_ref, wn_ref)

    slope2 = slope * LOG2E
    q_off = slope2 * lax.broadcasted_iota(jnp.int32, (1, tq), 1).astype(F32)
    key_ok = lax.broadcasted_iota(jnp.int32, (LEAD, tq), 0) >= N_PAD
    carries = []
    for blk in range(2):
        carry = []
        for mp in range(2):
            s = jnp.where(
                key_ok, _dot(lkn_ref[...], wc_ref[blk, mp, 0:HEAD_W, :]) + q_off, NEG)
            m = jnp.max(s, axis=0, keepdims=True)
            p = jnp.exp2(s - m)
            carry += [m, jnp.sum(p, axis=0, keepdims=True)]
            acc_ref[blk, mp] = _dot(lvt_ref[...], p.astype(BF16))
        carries.append(tuple(carry))

    key_i = lax.broadcasted_iota(jnp.int32, (tq, tq), 0)
    qry_i = lax.broadcasted_iota(jnp.int32, (tq, tq), 1)

    def scores(j, dst_ref, blk, diag):
        r = pl.multiple_of(j * tq, tq)
        lhs = jnp.concatenate([kn_ref[pl.ds(r, tq), :], kaug_ref[...]], axis=1)
        bms = []
        for mp in range(2):
            raw = _dot(lhs, wc_ref[blk, mp])
            if diag:
                raw = jnp.where(key_i <= qry_i, raw, NEG)
            dst_ref[mp] = raw
            bms.append(jnp.max(raw, axis=0, keepdims=True))
        return tuple(bms)

    def accumulate(j, src_ref, bms, carry, blk):
        r = pl.multiple_of(j * tq, tq)
        vt = vt_ref[:, pl.ds(r, tq)]
        c = slope2 * ((j - (2 * g + blk)) * tq).astype(F32)
        out = []
        for mp in range(2):
            m, l = carry[2 * mp], carry[2 * mp + 1]
            m_new = jnp.maximum(m, bms[mp] + c)
            alpha = jnp.exp2(m - m_new)
            p = jnp.exp2(src_ref[mp] - (m_new - c))
            out += [m_new, alpha * l + jnp.sum(p, axis=0, keepdims=True)]
            acc_ref[blk, mp] = alpha * acc_ref[blk, mp] + _dot(vt, p.astype(BF16))
        return tuple(out)

    i_a, i_b = 2 * g, 2 * g + 1
    carry_a, carry_b = carries
    bm = scores(i_a, sa_ref, 0, True)
    bm_b = scores(i_a, sb_ref, 1, False)
    carry_a = accumulate(i_a, sa_ref, bm, carry_a, 0)
    bm = scores(i_b, sa_ref, 1, True)
    carry_b = accumulate(i_a, sb_ref, bm_b, carry_b, 1)

    def both(j, state):
        j_b, bm_b, carry_a, carry_b = state
        bm_a = scores(j, sb_ref, 0, False)
        carry_b = accumulate(j_b, sa_ref, bm_b, carry_b, 1)
        bm_b = scores(j, sa_ref, 1, False)
        carry_a = accumulate(j, sb_ref, bm_a, carry_a, 0)
        return j, bm_b, carry_a, carry_b

    j_b, bm_b, carry_a, carry_b = lax.fori_loop(0, i_a, both, (i_b, bm, carry_a, carry_b))
    carry_b = accumulate(j_b, sa_ref, bm_b, carry_b, 1)

    lv4 = lvec_ref[...]
    lam = (jnp.exp(jnp.sum(lv4[0:1] * lv4[1:2], axis=-1, keepdims=True))
           - jnp.exp(jnp.sum(lv4[2:3] * lv4[3:4], axis=-1, keepdims=True)) + LAMBDA_INIT)
    for blk, carry in enumerate((carry_a, carry_b)):
        o = acc_ref[blk, 0] / carry[1] - lam * (acc_ref[blk, 1] / carry[3])
        o = o * lax.rsqrt(jnp.mean(o * o, axis=0, keepdims=True) + EPS) * swc_ref[...]
        o_ref[blk * tq:(blk + 1) * tq, :] = (o * (1.0 - LAMBDA_INIT)).T.astype(o_ref.dtype)


def _attn(proj3, lead_proj, slopes, lvec, qw, kw, sw, tq):
    b, seq, _ = proj3.shape
    assert seq % (2 * tq) == 0 and seq % 256 == 0 and tq % 16 == 0 and tq <= 512
    nsteps = seq // (2 * tq)
    return pl.pallas_call(
        functools.partial(_attn_kernel, tq=tq, seq=seq),
        grid=(b, HEADS, nsteps),
        in_specs=[
            pl.BlockSpec(memory_space=pltpu.SMEM),
            pl.BlockSpec((4, QK_DIM), lambda b_, h, i: (0, 0)),
            pl.BlockSpec((None, 2 * tq, HEAD_W), lambda b_, h, i: (b_, i, h)),
            pl.BlockSpec((None, 2 * tq, HEAD_W),
                         lambda b_, h, i: (b_, jnp.minimum(i + 1, nsteps - 1), h)),
            pl.BlockSpec((None, seq, HEAD_W), lambda b_, h, i: (b_, 0, HEADS + h)),
            pl.BlockSpec((None, seq, HEAD_W), lambda b_, h, i: (b_, 0, 2 * HEADS + h)),
            pl.BlockSpec((LEAD, HEAD_W), lambda b_, h, i: (0, HEADS + h)),
            pl.BlockSpec((LEAD, HEAD_W), lambda b_, h, i: (0, 2 * HEADS + h)),
            pl.BlockSpec((1, HEAD_W), lambda b_, h, i: (0, 0)),
            pl.BlockSpec((1, HEAD_W), lambda b_, h, i: (0, 0)),
            pl.BlockSpec((HEAD_W, 1), lambda b_, h, i: (0, 0)),
        ],
        out_specs=pl.BlockSpec((None, 2 * tq, HEAD_W), lambda b_, h, i: (b_, i, h)),
        out_shape=jax.ShapeDtypeStruct((b, seq, GROUP_W), BF16),
        scratch_shapes=[
            pltpu.VMEM((seq, HEAD_W), BF16),
            pltpu.VMEM((HEAD_W, seq), BF16),
            pltpu.VMEM((LEAD, HEAD_W), BF16),
            pltpu.VMEM((HEAD_W, LEAD), BF16),
            pltpu.VMEM((tq, HEAD_W), BF16),
            pltpu.VMEM((2, 2, HEAD_W, tq), F32),
            pltpu.VMEM((2, tq, tq), F32),
            pltpu.VMEM((2, tq, tq), F32),
            pltpu.VMEM((2, 2, 2 * HEAD_W, tq), BF16),
            pltpu.VMEM((2, 2, 2 * HEAD_W, tq), BF16),
        ],
        compiler_params=pltpu.CompilerParams(
            dimension_semantics=("parallel", "parallel", "arbitrary"),
            vmem_limit_bytes=VMEM_LIMIT),
        name="diff_attn",
    )(slopes, lvec, proj3, proj3, proj3, proj3, lead_proj, lead_proj, qw, kw, sw)


def _gdn_kernel(tq_ref, tk_ref, tv_ref, hq_ref, hk_ref, hv_ref, lq_ref, lk_ref, lv_ref,
                tg_ref, lg_ref, cwq_ref, cwk_ref, cwv_ref, alog_ref, dtb_ref, z_ref, nw_ref,
                o_ref, xs_ref, gs_ref, s_ref):
    s = pl.program_id(1)
    is_lead = s == 0
    width = GROUP_W
    srcs = ((tq_ref, hq_ref, lq_ref), (tk_ref, hk_ref, lk_ref), (tv_ref, hv_ref, lv_ref))

    @pl.when(is_lead)
    def _():
        rowid = lax.broadcasted_iota(jnp.int32, (CHUNK, width), 0)
        for idx, (_, _, l_ref) in enumerate(srcs):
            cs = slice(idx * width, (idx + 1) * width)
            xs_ref[0:8, cs] = jnp.zeros((8, width), F32)
            xs_ref[8:8 + CHUNK, cs] = jnp.where(rowid >= N_PAD, l_ref[...].astype(F32), 0.0)
        gs_ref[...] = lg_ref[...]
        s_ref[...] = jnp.zeros_like(s_ref)

    @pl.when(s == 1)
    def _():
        for idx, (t_ref, _, l_ref) in enumerate(srcs):
            cs = slice(idx * width, (idx + 1) * width)
            xs_ref[0:8, cs] = l_ref[LEAD - 16:LEAD, :].astype(F32)[8:16]
            xs_ref[8:8 + CHUNK, cs] = t_ref[...].astype(F32)
        gs_ref[...] = tg_ref[...]

    @pl.when(s > 1)
    def _():
        for idx, (t_ref, h_ref, _) in enumerate(srcs):
            cs = slice(idx * width, (idx + 1) * width)
            xs_ref[0:8, cs] = h_ref[...].astype(F32)[8:16]
            xs_ref[8:8 + CHUNK, cs] = t_ref[...].astype(F32)
        gs_ref[...] = tg_ref[...]

    rowi = lax.broadcasted_iota(jnp.int32, (CHUNK, CHUNK), 0)
    lanei = lax.broadcasted_iota(jnp.int32, (CHUNK, CHUNK), 1)
    incl = rowi >= lanei
    eye = (rowi == lanei).astype(F32)

    g_t = gs_ref[...].T[0:2 * HEADS]
    vmask = (lax.broadcasted_iota(jnp.int32, (HEADS, CHUNK), 1)
             >= jnp.where(is_lead, N_PAD, 0)).astype(F32)
    beta_t = jax.nn.sigmoid(g_t[0:HEADS]) * vmask
    t = g_t[HEADS:] + dtb_ref[...]
    softplus = jnp.maximum(t, 0.0) + jnp.log(1.0 + jnp.exp(-jnp.abs(t)))
    decay_t = -jnp.exp(alog_ref[...]) * softplus * vmask
    gc_t = _dot(decay_t, (rowi <= lanei).astype(F32), HIGHEST)
    cols = jnp.concatenate(
        [beta_t, gc_t, jnp.zeros((CHUNK - 2 * HEADS, CHUNK), F32)], axis=0).T

    heads = range(HEADS)
    hcols = [slice(h * HEAD_W, (h + 1) * HEAD_W) for h in heads]
    mks, pks, rhs, qgs, qkds, decs = [], [], [], [], [], []
    for hh in heads:
        hs = hcols[hh]
        beta = cols[:, hh:hh + 1]
        gc = cols[:, HEADS + hh:HEADS + hh + 1]
        gc_row = gc_t[hh:hh + 1]
        g_last = gc_row[:, CHUNK - 1:CHUNK]

        def conv_silu(idx, cw_ref):
            c0 = idx * width + hh * HEAD_W
            y = xs_ref[5:5 + CHUNK, c0:c0 + HEAD_W] * cw_ref[0:1, hs]
            for j in range(1, CONV_K):
                y = y + xs_ref[5 + j:5 + j + CHUNK, c0:c0 + HEAD_W] * cw_ref[j:j + 1, hs]
            return y * jax.nn.sigmoid(y)

        q = conv_silu(0, cwq_ref)
        k = conv_silu(1, cwk_ref)
        v = conv_silu(2, cwv_ref)
        q = q * lax.rsqrt(jnp.sum(q * q, axis=-1, keepdims=True) + EPS) * (HEAD_W ** -0.5)
        k = k * lax.rsqrt(jnp.sum(k * k, axis=-1, keepdims=True) + EPS)

        decay = jnp.where(incl, jnp.exp(jnp.where(incl, gc - gc_row, 0.0)), 0.0)
        kb = k * beta
        k_t = k.T
        kt16 = k_t.astype(BF16)
        lmat = jnp.where(rowi > lanei, _dot(kb.astype(BF16), kt16) * decay, 0.0)
        qkds.append(jnp.concatenate(
            [(_dot(q.astype(BF16), kt16) * decay).astype(BF16),
             (k_t * jnp.exp(g_last - gc_row)).astype(BF16)], axis=0))
        qgs.append((q * jnp.exp(gc)).astype(BF16))
        decs.append(jnp.exp(g_last))
        mks.append(-lmat)
        pks.append(eye - lmat)
        rhs.append(jnp.concatenate([v * beta, kb * jnp.exp(gc)], axis=1))

    ms = [_split(m) for m in mks]
    mks = [_dot3(m, m) for m in ms]
    for _ in range(5):
        ms = [_split(m) for m in mks]
        rs = [_dot3(m, [_split(p), m]) for m, p in zip(ms, pks)]
        pks = [p + r[:, :CHUNK] for p, r in zip(pks, rs)]
        mks = [r[:, CHUNK:] for r in rs]
    pks = [p + _dot3(_split(m), _split(p)) for m, p in zip(mks, pks)]
    uws = [_dot3(_split(p), _split(r)) for p, r in zip(pks, rhs)]

    states = [s_ref[h] for h in heads]
    ws_qs = [_dot(jnp.concatenate([uws[h][:, HEAD_W:].astype(BF16), qgs[h]], axis=0),
                  states[h].astype(BF16)) for h in heads]
    v16 = [(uws[h][:, :HEAD_W] - ws_qs[h][:CHUNK]).astype(BF16) for h in heads]
    qv_kv = [_dot(qkds[h], v16[h]) for h in heads]
    for h in heads:
        s_ref[h] = decs[h] * states[h] + qv_kv[h][CHUNK:]
        o = ws_qs[h][CHUNK:] + qv_kv[h][:CHUNK]
        z = z_ref[:, hcols[h]].astype(F32)
        o_ref[:, hcols[h]] = (
            _rms_rows(o, nw_ref[...]) * (z * jax.nn.sigmoid(z))).astype(o_ref.dtype)


def _gdn(proj3, lead_proj, gates3, lead_gates, conv_wt, alog_col, dtb_col, onw):
    b, seq, _ = proj3.shape
    nb = 1 + seq // CHUNK
    tok = lambda g: pl.BlockSpec(
        (None, CHUNK, GROUP_W), lambda b_, s: (b_, jnp.maximum(s - 1, 0), g))
    halo = lambda g: pl.BlockSpec(
        (None, 16, GROUP_W),
        lambda b_, s: (b_, jnp.maximum((s - 1) * (CHUNK // 16) - 1, 0), g))
    lead = lambda g: pl.BlockSpec((LEAD, GROUP_W), lambda b_, s: (0, g))
    cw = lambda g: pl.BlockSpec((CONV_K, GROUP_W), lambda b_, s: (0, g))
    col = pl.BlockSpec((HEADS, 1), lambda b_, s: (0, 0))
    return pl.pallas_call(
        _gdn_kernel,
        grid=(b, nb),
        in_specs=[
            tok(3), tok(4), tok(5), halo(3), halo(4), halo(5), lead(3), lead(4), lead(5),
            pl.BlockSpec((None, CHUNK, HEAD_W), lambda b_, s: (b_, jnp.maximum(s - 1, 0), 0)),
            pl.BlockSpec((LEAD, HEAD_W), lambda b_, s: (0, 0)),
            cw(0), cw(1), cw(2), col, col, tok(6),
            pl.BlockSpec((1, HEAD_W), lambda b_, s: (0, 0)),
        ],
        out_specs=pl.BlockSpec((None, CHUNK, GROUP_W), lambda b_, s: (b_, jnp.maximum(s - 1, 0), 0)),
        out_shape=jax.ShapeDtypeStruct((b, seq, GROUP_W), BF16),
        scratch_shapes=[pltpu.VMEM((8 + CHUNK, 3 * GROUP_W), F32),
                        pltpu.VMEM((CHUNK, HEAD_W), F32),
                        pltpu.VMEM((HEADS, HEAD_W, HEAD_W), F32)],
        compiler_params=pltpu.CompilerParams(
            dimension_semantics=("parallel", "arbitrary"), vmem_limit_bytes=VMEM_LIMIT),
        name="gdn",
    )(proj3, proj3, proj3, proj3, proj3, proj3, lead_proj, lead_proj, lead_proj,
      gates3, lead_gates, conv_wt, conv_wt, conv_wt, alog_col, dtb_col, proj3, onw)


def _outproj_kernel(x_ref, oa_ref, od_ref, wa_ref, wd_ref, o_ref):
    o_ref[...] = x_ref[...] + _dot(oa_ref[...], wa_ref[...]) + _dot(od_ref[...], wd_ref[...])


def _outproj(x2d, oa, od, w_out16, tm):
    m = x2d.shape[0]
    assert m % tm == 0
    return pl.pallas_call(
        _outproj_kernel,
        grid=(m // tm,),
        in_specs=[
            pl.BlockSpec((tm, D_MODEL), lambda i: (i, 0)),
            pl.BlockSpec((tm, GROUP_W), lambda i: (i, 0)),
            pl.BlockSpec((tm, GROUP_W), lambda i: (i, 0)),
            pl.BlockSpec((GROUP_W, D_MODEL), lambda i: (0, 0)),
            pl.BlockSpec((GROUP_W, D_MODEL), lambda i: (1, 0)),
        ],
        out_specs=pl.BlockSpec((tm, D_MODEL), lambda i: (i, 0)),
        out_shape=jax.ShapeDtypeStruct((m, D_MODEL), F32),
        compiler_params=pltpu.CompilerParams(
            dimension_semantics=("parallel",), vmem_limit_bytes=VMEM_LIMIT),
        name="outproj",
    )(x2d, oa, od, w_out16, w_out16)


def _ffn_kernel(h_ref, nw_ref, wg_ref, wu_ref, wd_ref, o_ref, u_ref, *, row_chunk):
    j = pl.program_id(1)

    @pl.when(j == 0)
    def _():
        def body(c, carry):
            r = pl.multiple_of(c * row_chunk, row_chunk)
            x = h_ref[pl.ds(r, row_chunk), :]
            u_ref[pl.ds(r, row_chunk), :] = _rms_rows(x, nw_ref[...]).astype(BF16)
            o_ref[pl.ds(r, row_chunk), :] = x
            return carry

        lax.fori_loop(0, h_ref.shape[0] // row_chunk, body, 0)

    u = u_ref[...]
    g = _dot(u, wg_ref[...])
    a = (g * jax.nn.sigmoid(g) * _dot(u, wu_ref[...])).astype(BF16)
    o_ref[...] += _dot(a, wd_ref[...])


def _ffn(h2d, norm_w, wg, wu, wd, tm, th):
    m = h2d.shape[0]
    assert m % tm == 0 and FFN_HIDDEN % th == 0
    return pl.pallas_call(
        functools.partial(_ffn_kernel, row_chunk=min(256, tm)),
        grid=(m // tm, FFN_HIDDEN // th),
        in_specs=[
            pl.BlockSpec((tm, D_MODEL), lambda i, j: (i, 0)),
            pl.BlockSpec((1, D_MODEL), lambda i, j: (0, 0)),
            pl.BlockSpec((D_MODEL, th), lambda i, j: (0, j)),
            pl.BlockSpec((D_MODEL, th), lambda i, j: (0, j)),
            pl.BlockSpec((th, D_MODEL), lambda i, j: (j, 0)),
        ],
        out_specs=pl.BlockSpec((tm, D_MODEL), lambda i, j: (i, 0)),
        out_shape=jax.ShapeDtypeStruct((m, D_MODEL), F32),
        scratch_shapes=[pltpu.VMEM((tm, D_MODEL), BF16)],
        compiler_params=pltpu.CompilerParams(
            dimension_semantics=("parallel", "arbitrary"), vmem_limit_bytes=VMEM_LIMIT),
        name="ffn",
    )(h2d, norm_w, wg, wu, wd)


def kernel(x, meta_tokens, attn_norm_w, w_in, q_norm_w, k_norm_w, lambda_q1, lambda_k1, lambda_q2,
           lambda_k2, subln_w, conv_w, a_log, dt_bias, o_norm_w, w_out, ffn_norm_w, w_gate, w_up,
           w_down):
    b, seq, _ = x.shape
    m = b * seq
    x2d = x.reshape(m, D_MODEL)
    lead = jnp.concatenate([jnp.zeros((N_PAD, D_MODEL), x.dtype), meta_tokens.astype(x.dtype)], 0)

    w_main = w_in[0, :, :MAIN_COLS].astype(BF16)
    w_gates = jnp.pad(w_in[0, :, MAIN_COLS:], ((0, 0), (0, HEAD_W - GATE_COLS))).astype(BF16)
    tm = min(1024, m)
    proj, gates = _inproj(x2d, attn_norm_w, w_main, w_gates, tm, 1024)
    lead_proj, lead_gates = _inproj(lead, attn_norm_w, w_main, w_gates, LEAD, 1024)
    proj3 = proj.reshape(b, seq, MAIN_COLS)

    slopes = 2.0 ** (-8.0 * jnp.arange(1, HEADS + 1, dtype=F32) / HEADS)
    lvec = jnp.concatenate([lambda_q1, lambda_k1, lambda_q2, lambda_k2], 0).astype(F32)
    o_a = _attn(proj3, lead_proj, slopes, lvec, jnp.tile(q_norm_w, (1, 2)),
                jnp.tile(k_norm_w, (1, 2)), subln_w.reshape(HEAD_W, 1), min(512, seq))

    o_d = _gdn(
        proj3, lead_proj, gates.reshape(b, seq, HEAD_W), lead_gates, conv_w[0].T,
        a_log.astype(F32).reshape(HEADS, 1), dt_bias.astype(F32).reshape(HEADS, 1), o_norm_w)

    h1 = _outproj(x2d, o_a.reshape(m, GROUP_W), o_d.reshape(m, GROUP_W), w_out[0].astype(BF16),
                  min(512, m))
    out = _ffn(h1, ffn_norm_w, w_gate[0].astype(BF16), w_up[0].astype(BF16),
               w_down[0].astype(BF16), min(1024, m), 512)
    return out.reshape(b, seq, D_MODEL)
```

```python
import functools

import jax
import jax.numpy as jnp
import numpy as np
from jax import lax
from jax.experimental import pallas as pl
from jax.experimental.pallas import tpu as pltpu

F32 = jnp.float32
BF16 = jnp.bfloat16
HIGHEST = lax.Precision.HIGHEST

D_MODEL = 2048
N_META = 16
LEAD = 128
N_PAD = LEAD - N_META
HEADS = 8
HEAD_W = 128
QK_DIM = 64
GROUP_W = HEADS * HEAD_W
MAIN_COLS = 7 * GROUP_W
GATE_COLS = 2 * HEADS
CONV_K = 4
FFN_HIDDEN = 5632
EPS = 1e-6
NEG = -1e30
LAMBDA_INIT = 0.2
CHUNK = 128
VMEM_LIMIT = 56 * 1024 * 1024
LOG2E = 1.4426950408889634


def _bf16_pieces(x, n):
    out = []
    for _ in range(n):
        bits = np.array(x, np.float32).view(np.uint32)
        bits = (bits + 0x7FFF + ((bits >> 16) & 1)) & 0xFFFF0000
        p = float(bits.view(np.float32))
        out.append(p)
        x -= p
    return tuple(out)


LOG2E_BF16_PIECES = _bf16_pieces(LOG2E, 3)


def _dot(a, b, precision=None):
    return jnp.dot(a, b, preferred_element_type=F32, precision=precision)


def _dot_nt(a, b):
    return lax.dot_general(a, b, (((1,), (1,)), ((), ())), preferred_element_type=F32)


def _split(x):
    hi = x.astype(BF16)
    return hi, (x - hi.astype(F32)).astype(BF16)


def _dot3(a, b):
    a_hi, a_lo = a
    if isinstance(b, list):
        b_hi = jnp.concatenate([x[0] for x in b], axis=1)
        b_lo = jnp.concatenate([x[1] for x in b], axis=1)
    else:
        b_hi, b_lo = b
    return _dot(jnp.concatenate([a_hi, a_lo, a_hi], axis=1),
                jnp.concatenate([b_hi, b_hi, b_lo], axis=0))


def _rms_rows(x, w_row):
    return x * lax.rsqrt(jnp.mean(x * x, axis=-1, keepdims=True) + EPS) * w_row


def _inproj_kernel(x_ref, nw_ref, w_ref, wg_ref, o_ref, g_ref, u_ref, *, row_chunk):
    j = pl.program_id(1)

    @pl.when(j == 0)
    def _():
        def body(c, carry):
            r = pl.multiple_of(c * row_chunk, row_chunk)
            u = _rms_rows(x_ref[pl.ds(r, row_chunk), :], nw_ref[...]).astype(BF16)
            u_ref[pl.ds(r, row_chunk), :] = u
            g_ref[pl.ds(r, row_chunk), :] = _dot(u, wg_ref[...])
            return carry

        lax.fori_loop(0, x_ref.shape[0] // row_chunk, body, 0)

    o_ref[...] = _dot(u_ref[...], w_ref[...]).astype(o_ref.dtype)


def _inproj(x2d, norm_w, w_main, w_gate, tm, tn):
    m = x2d.shape[0]
    assert m % tm == 0 and MAIN_COLS % tn == 0
    row_chunk = min(256, tm)
    return pl.pallas_call(
        functools.partial(_inproj_kernel, row_chunk=row_chunk),
        grid=(m // tm, MAIN_COLS // tn),
        in_specs=[
            pl.BlockSpec((tm, D_MODEL), lambda i, j: (i, 0)),
            pl.BlockSpec((1, D_MODEL), lambda i, j: (0, 0)),
            pl.BlockSpec((D_MODEL, tn), lambda i, j: (0, j)),
            pl.BlockSpec((D_MODEL, HEAD_W), lambda i, j: (0, 0)),
        ],
        out_specs=[
            pl.BlockSpec((tm, tn), lambda i, j: (i, j)),
            pl.BlockSpec((tm, HEAD_W), lambda i, j: (i, 0)),
        ],
        out_shape=[
            jax.ShapeDtypeStruct((m, MAIN_COLS), BF16),
            jax.ShapeDtypeStruct((m, HEAD_W), F32),
        ],
        scratch_shapes=[pltpu.VMEM((tm, D_MODEL), BF16)],
        compiler_params=pltpu.CompilerParams(
            dimension_semantics=("parallel", "arbitrary"), vmem_limit_bytes=VMEM_LIMIT),
        name="inproj",
    )(x2d, norm_w, w_main, w_gate)


def _halfnorm(x, w_row):
    lo = lax.broadcasted_iota(jnp.int32, x.shape, 1) < QK_DIM
    x2 = x * x
    s_lo = jnp.sum(jnp.where(lo, x2, 0.0), axis=-1, keepdims=True)
    s_hi = jnp.sum(jnp.where(lo, 0.0, x2), axis=-1, keepdims=True)
    ms = jnp.where(lo, s_lo, s_hi) * (1.0 / QK_DIM)
    return x * lax.rsqrt(ms + EPS) * w_row


def _attn_kernel(slopes_ref, lvec_ref, q_ref, qnext_ref, k_ref, v_ref, lk_ref, lv_ref, qw_ref,
                 kw_ref, swc_ref, o_ref, kn_ref, vt_ref, lkn_ref, lvt_ref, kaug_ref, acc_ref,
                 sa_ref, sb_ref, wn_ref, wc_ref, kstage_ref, vstage_ref, *, tq, seq):
    h = pl.program_id(1)
    g = pl.program_id(2)
    slope = slopes_ref[h]

    def query_operands(src_ref, dst_ref):
        lo = lax.broadcasted_iota(jnp.int32, (tq, HEAD_W), 1) < QK_DIM
        sub = lax.broadcasted_iota(jnp.int32, (HEAD_W, tq), 0)
        aug = jnp.zeros((HEAD_W, tq), F32)
        for n, piece in enumerate(LOG2E_BF16_PIECES):
            aug = jnp.where((sub == n) | (sub == n + 3), piece, aug)
        for blk in range(2):
            q_blk = src_ref[blk * tq:(blk + 1) * tq, :].astype(F32)
            qn = _halfnorm(q_blk, qw_ref[...]) * (QK_DIM ** -0.5 * LOG2E)
            for mp, x in enumerate((jnp.where(lo, qn, 0.0), jnp.where(lo, 0.0, qn))):
                dst_ref[blk, mp] = jnp.concatenate([x.T, aug], axis=0).astype(BF16)

    def key_operands(row0, kdst_ref, kdst0, vdst_ref, vdst0):
        al = lambda x: x if isinstance(x, int) else pl.multiple_of(x, 256)
        for c in range(2 * tq // 256):
            src = pl.ds(al(row0 + c * 256), 256)
            dst = pl.ds(al(kdst0 + c * 256), 256)
            dstv = pl.ds(al(vdst0 + c * 256), 256)
            kdst_ref[dst, :] = _halfnorm(k_ref[src, :].astype(F32), kw_ref[...]).astype(BF16)
            vdst_ref[:, dstv] = v_ref[src, :].astype(F32).T.astype(BF16)

    step_rows = 2 * tq

    @pl.when(g == 0)
    def _():
        query_operands(q_ref, wn_ref)
        key_operands(0, kn_ref, 0, vt_ref, 0)
        lkn_ref[...] = _halfnorm(lk_ref[...].astype(F32), kw_ref[...]).astype(BF16)
        lvt_ref[...] = lv_ref[...].astype(F32).T.astype(BF16)
        kk = lax.broadcasted_iota(jnp.int32, (tq, HEAD_W), 0)
        ln = lax.broadcasted_iota(jnp.int32, (tq, HEAD_W), 1)
        hi = ((kk // 16) * 16).astype(F32)
        lo_ = (kk % 16).astype(F32)
        kaug_ref[...] = (slope * jnp.where(ln < 3, hi, jnp.where(ln < 6, lo_, 0.0))).astype(BF16)

    @pl.when(g > 0)
    def _():
        r = pl.multiple_of(g * step_rows, step_rows)
        kn_ref[pl.ds(r, step_rows), :] = kstage_ref[...]
        vt_ref[:, pl.ds(r, step_rows)] = vstage_ref[...]

    wc_ref[...] = wn_ref[...]
    query_operands(qnext_ref, wn_ref)
    key_operands(jnp.minimum(g + 1, seq // step_rows - 1) * step_rows, kstage_ref, 0,
                 vstage_ref, 0)

    slope2 = slope * LOG2E
    q_off = slope2 * lax.broadcasted_iota(jnp.int32, (1, tq), 1).astype(F32)
    key_ok = lax.broadcasted_iota(jnp.int32, (LEAD, tq), 0) >= N_PAD
    carries = []
    for blk in range(2):
        carry = []
        for mp in range(2):
            s = jnp.where(
                key_ok, _dot(lkn_ref[...], wc_ref[blk, mp, 0:HEAD_W, :]) + q_off, NEG)
            m = jnp.max(s, axis=0, keepdims=True)
            p = jnp.exp2(s - m)
            carry += [m, jnp.sum(p, axis=0, keepdims=True)]
            acc_ref[blk, mp] = _dot(lvt_ref[...], p.astype(BF16))
        carries.append(tuple(carry))

    key_i = lax.broadcasted_iota(jnp.int32, (tq, tq), 0)
    qry_i = lax.broadcasted_iota(jnp.int32, (tq, tq), 1)

    def scores(j, dst_ref, blk, diag):
        r = pl.multiple_of(j * tq, tq)
        lhs = jnp.concatenate([kn_ref[pl.ds(r, tq), :], kaug_ref[...]], axis=1)
        bms = []
        for mp in range(2):
            raw = _dot(lhs, wc_ref[blk, mp])
            if diag:
                raw = jnp.where(key_i <= qry_i, raw, NEG)
            dst_ref[mp] = raw
            bms.append(jnp.max(raw, axis=0, keepdims=True))
        return tuple(bms)

    def accumulate(j, src_ref, bms, carry, blk):
        r = pl.multiple_of(j * tq, tq)
        vt = vt_ref[:, pl.ds(r, tq)]
        c = slope2 * ((j - (2 * g + blk)) * tq).astype(F32)
        out = []
        for mp in range(2):
            m, l = carry[2 * mp], carry[2 * mp + 1]
            m_new = jnp.maximum(m, bms[mp] + c)
            alpha = jnp.exp2(m - m_new)
            p = jnp.exp2(src_ref[mp] - (m_new - c))
            out += [m_new, alpha * l + jnp.sum(p, axis=0, keepdims=True)]
            acc_ref[blk, mp] = alpha * acc_ref[blk, mp] + _dot(vt, p.astype(BF16))
        return tuple(out)

    i_a, i_b = 2 * g, 2 * g + 1
    carry_a, carry_b = carries
    bm = scores(i_a, sa_ref, 0, True)
    bm_b = scores(i_a, sb_ref, 1, False)
    carry_a = accumulate(i_a, sa_ref, bm, carry_a, 0)
    bm = scores(i_b, sa_ref, 1, True)
    carry_b = accumulate(i_a, sb_ref, bm_b, carry_b, 1)

    def both(j, state):
        j_b, bm_b, carry_a, carry_b = state
        bm_a = scores(j, sb_ref, 0, False)
        carry_b = accumulate(j_b, sa_ref, bm_b, carry_b, 1)
        bm_b = scores(j, sa_ref, 1, False)
        carry_a = accumulate(j, sb_ref, bm_a, carry_a, 0)
        return j, bm_b, carry_a, carry_b

    j_b, bm_b, carry_a, carry_b = lax.fori_loop(0, i_a, both, (i_b, bm, carry_a, carry_b))
    carry_b = accumulate(j_b, sa_ref, bm_b, carry_b, 1)

    lv4 = lvec_ref[...]
    lam = (jnp.exp(jnp.sum(lv4[0:1] * lv4[1:2], axis=-1, keepdims=True))
           - jnp.exp(jnp.sum(lv4[2:3] * lv4[3:4], axis=-1, keepdims=True)) + LAMBDA_INIT)
    for blk, carry in enumerate((carry_a, carry_b)):
        o = acc_ref[blk, 0] / carry[1] - lam * (acc_ref[blk, 1] / carry[3])
        o = o * lax.rsqrt(jnp.mean(o * o, axis=0, keepdims=True) + EPS) * swc_ref[...]
        o_ref[blk * tq:(blk + 1) * tq, :] = (o * (1.0 - LAMBDA_INIT)).T.astype(o_ref.dtype)


def _attn(proj3, lead_proj, slopes, lvec, qw, kw, sw, tq):
    b, seq, _ = proj3.shape
    assert seq % (2 * tq) == 0 and seq % 256 == 0 and tq % 16 == 0 and tq <= 512
    nsteps = seq // (2 * tq)
    return pl.pallas_call(
        functools.partial(_attn_kernel, tq=tq, seq=seq),
        grid=(b, HEADS, nsteps),
        in_specs=[
            pl.BlockSpec(memory_space=pltpu.SMEM),
            pl.BlockSpec((4, QK_DIM), lambda b_, h, i: (0, 0)),
            pl.BlockSpec((None, 2 * tq, HEAD_W), lambda b_, h, i: (b_, i, h)),
            pl.BlockSpec((None, 2 * tq, HEAD_W),
                         lambda b_, h, i: (b_, jnp.minimum(i + 1, nsteps - 1), h)),
            pl.BlockSpec((None, seq, HEAD_W), lambda b_, h, i: (b_, 0, HEADS + h)),
            pl.BlockSpec((None, seq, HEAD_W), lambda b_, h, i: (b_, 0, 2 * HEADS + h)),
            pl.BlockSpec((LEAD, HEAD_W), lambda b_, h, i: (0, HEADS + h)),
            pl.BlockSpec((LEAD, HEAD_W), lambda b_, h, i: (0, 2 * HEADS + h)),
            pl.BlockSpec((1, HEAD_W), lambda b_, h, i: (0, 0)),
            pl.BlockSpec((1, HEAD_W), lambda b_, h, i: (0, 0)),
            pl.BlockSpec((HEAD_W, 1), lambda b_, h, i: (0, 0)),
        ],
        out_specs=pl.BlockSpec((None, 2 * tq, HEAD_W), lambda b_, h, i: (b_, i, h)),
        out_shape=jax.ShapeDtypeStruct((b, seq, GROUP_W), BF16),
        scratch_shapes=[
            pltpu.VMEM((seq, HEAD_W), BF16),
            pltpu.VMEM((HEAD_W, seq), BF16),
            pltpu.VMEM((LEAD, HEAD_W), BF16),
            pltpu.VMEM((HEAD_W, LEAD), BF16),
            pltpu.VMEM((tq, HEAD_W), BF16),
            pltpu.VMEM((2, 2, HEAD_W, tq), F32),
            pltpu.VMEM((2, tq, tq), F32),
            pltpu.VMEM((2, tq, tq), F32),
            pltpu.VMEM((2, 2, 2 * HEAD_W, tq), BF16),
            pltpu.VMEM((2, 2, 2 * HEAD_W, tq), BF16),
            pltpu.VMEM((2 * tq, HEAD_W), BF16),
            pltpu.VMEM((HEAD_W, 2 * tq), BF16),
        ],
        compiler_params=pltpu.CompilerParams(
            dimension_semantics=("parallel", "parallel", "arbitrary"),
            vmem_limit_bytes=VMEM_LIMIT),
        name="diff_attn",
    )(slopes, lvec, proj3, proj3, proj3, proj3, lead_proj, lead_proj, qw, kw, sw)


def _gdn_kernel(tq_ref, tk_ref, tv_ref, hq_ref, hk_ref, hv_ref, lq_ref, lk_ref, lv_ref,
                tg_ref, lg_ref, cwq_ref, cwk_ref, cwv_ref, alog_ref, dtb_ref, z_ref, nw_ref,
                o_ref, xs_ref, gs_ref, s_ref):
    s = pl.program_id(1)
    is_lead = s == 0
    width = GROUP_W
    srcs = ((tq_ref, hq_ref, lq_ref), (tk_ref, hk_ref, lk_ref), (tv_ref, hv_ref, lv_ref))

    @pl.when(is_lead)
    def _():
        rowid = lax.broadcasted_iota(jnp.int32, (CHUNK, width), 0)
        for idx, (_, _, l_ref) in enumerate(srcs):
            cs = slice(idx * width, (idx + 1) * width)
            xs_ref[0:8, cs] = jnp.zeros((8, width), F32)
            xs_ref[8:8 + CHUNK, cs] = jnp.where(rowid >= N_PAD, l_ref[...].astype(F32), 0.0)
        gs_ref[...] = lg_ref[...]
        s_ref[...] = jnp.zeros_like(s_ref)

    @pl.when(s == 1)
    def _():
        for idx, (t_ref, _, l_ref) in enumerate(srcs):
            cs = slice(idx * width, (idx + 1) * width)
            xs_ref[0:8, cs] = l_ref[LEAD - 16:LEAD, :].astype(F32)[8:16]
            xs_ref[8:8 + CHUNK, cs] = t_ref[...].astype(F32)
        gs_ref[...] = tg_ref[...]

    @pl.when(s > 1)
    def _():
        for idx, (t_ref, h_ref, _) in enumerate(srcs):
            cs = slice(idx * width, (idx + 1) * width)
            xs_ref[0:8, cs] = h_ref[...].astype(F32)[8:16]
            xs_ref[8:8 + CHUNK, cs] = t_ref[...].astype(F32)
        gs_ref[...] = tg_ref[...]

    rowi = lax.broadcasted_iota(jnp.int32, (CHUNK, CHUNK), 0)
    lanei = lax.broadcasted_iota(jnp.int32, (CHUNK, CHUNK), 1)
    incl = rowi >= lanei
    eye = (rowi == lanei).astype(F32)

    g_t = gs_ref[...].T[0:2 * HEADS]
    vmask = (lax.broadcasted_iota(jnp.int32, (HEADS, CHUNK), 1)
             >= jnp.where(is_lead, N_PAD, 0)).astype(F32)
    beta_t = jax.nn.sigmoid(g_t[0:HEADS]) * vmask
    t = g_t[HEADS:] + dtb_ref[...]
    softplus = jnp.maximum(t, 0.0) + jnp.log(1.0 + jnp.exp(-jnp.abs(t)))
    decay_t = -jnp.exp(alog_ref[...]) * softplus * vmask
    gc_t = _dot(decay_t, (rowi <= lanei).astype(F32), HIGHEST)
    cols = jnp.concatenate(
        [beta_t, gc_t, jnp.zeros((CHUNK - 2 * HEADS, CHUNK), F32)], axis=0).T

    heads = range(HEADS)
    hcols = [slice(h * HEAD_W, (h + 1) * HEAD_W) for h in heads]
    mks, pks, rhs, qgs, qkds, decs = [], [], [], [], [], []
    for hh in heads:
        hs = hcols[hh]
        beta = cols[:, hh:hh + 1]
        gc = cols[:, HEADS + hh:HEADS + hh + 1]
        gc_row = gc_t[hh:hh + 1]
        g_last = gc_row[:, CHUNK - 1:CHUNK]

        def conv_silu(idx, cw_ref):
            c0 = idx * width + hh * HEAD_W
            y = xs_ref[5:5 + CHUNK, c0:c0 + HEAD_W] * cw_ref[0:1, hs]
            for j in range(1, CONV_K):
                y = y + xs_ref[5 + j:5 + j + CHUNK, c0:c0 + HEAD_W] * cw_ref[j:j + 1, hs]
            return y * jax.nn.sigmoid(y)

        q = conv_silu(0, cwq_ref)
        k = conv_silu(1, cwk_ref)
        v = conv_silu(2, cwv_ref)
        q = q * lax.rsqrt(jnp.sum(q * q, axis=-1, keepdims=True) + EPS) * (HEAD_W ** -0.5)
        k = k * lax.rsqrt(jnp.sum(k * k, axis=-1, keepdims=True) + EPS)

        decay = jnp.where(incl, jnp.exp(jnp.where(incl, gc - gc_row, 0.0)), 0.0)
        kb = k * beta
        k_t = k.T
        kt16 = k_t.astype(BF16)
        lmat = jnp.where(rowi > lanei, _dot(kb.astype(BF16), kt16) * decay, 0.0)
        qkds.append(jnp.concatenate(
            [(_dot(q.astype(BF16), kt16) * decay).astype(BF16),
             (k_t * jnp.exp(g_last - gc_row)).astype(BF16)], axis=0))
        qgs.append((q * jnp.exp(gc)).astype(BF16))
        decs.append(jnp.exp(g_last))
        mks.append(-lmat)
        pks.append(eye - lmat)
        rhs.append(jnp.concatenate([v * beta, kb * jnp.exp(gc)], axis=1))

    ms = [_split(m) for m in mks]
    mks = [_dot3(m, m) for m in ms]
    for _ in range(5):
        ms = [_split(m) for m in mks]
        rs = [_dot3(m, [_split(p), m]) for m, p in zip(ms, pks)]
        pks = [p + r[:, :CHUNK] for p, r in zip(pks, rs)]
        mks = [r[:, CHUNK:] for r in rs]
    pks = [p + _dot3(_split(m), _split(p)) for m, p in zip(mks, pks)]
    uws = [_dot3(_split(p), _split(r)) for p, r in zip(pks, rhs)]

    states = [s_ref[h] for h in heads]
    ws_qs = [_dot(jnp.concatenate([uws[h][:, HEAD_W:].astype(BF16), qgs[h]], axis=0),
                  states[h].astype(BF16)) for h in heads]
    v16 = [(uws[h][:, :HEAD_W] - ws_qs[h][:CHUNK]).astype(BF16) for h in heads]
    qv_kv = [_dot(qkds[h], v16[h]) for h in heads]
    for h in heads:
        s_ref[h] = decs[h] * states[h] + qv_kv[h][CHUNK:]
        o = ws_qs[h][CHUNK:] + qv_kv[h][:CHUNK]
        z = z_ref[:, hcols[h]].astype(F32)
        o_ref[:, hcols[h]] = (
            _rms_rows(o, nw_ref[...]) * (z * jax.nn.sigmoid(z))).astype(o_ref.dtype)


def _gdn(proj3, lead_proj, gates3, lead_gates, conv_wt, alog_col, dtb_col, onw):
    b, seq, _ = proj3.shape
    nb = 1 + seq // CHUNK
    tok = lambda g: pl.BlockSpec(
        (None, CHUNK, GROUP_W), lambda b_, s: (b_, jnp.maximum(s - 1, 0), g))
    halo = lambda g: pl.BlockSpec(
        (None, 16, GROUP_W),
        lambda b_, s: (b_, jnp.maximum((s - 1) * (CHUNK // 16) - 1, 0), g))
    lead = lambda g: pl.BlockSpec((LEAD, GROUP_W), lambda b_, s: (0, g))
    cw = lambda g: pl.BlockSpec((CONV_K, GROUP_W), lambda b_, s: (0, g))
    col = pl.BlockSpec((HEADS, 1), lambda b_, s: (0, 0))
    return pl.pallas_call(
        _gdn_kernel,
        grid=(b, nb),
        in_specs=[
            tok(3), tok(4), tok(5), halo(3), halo(4), halo(5), lead(3), lead(4), lead(5),
            pl.BlockSpec((None, CHUNK, HEAD_W), lambda b_, s: (b_, jnp.maximum(s - 1, 0), 0)),
            pl.BlockSpec((LEAD, HEAD_W), lambda b_, s: (0, 0)),
            cw(0), cw(1), cw(2), col, col, tok(6),
            pl.BlockSpec((1, HEAD_W), lambda b_, s: (0, 0)),
        ],
        out_specs=pl.BlockSpec((None, CHUNK, GROUP_W), lambda b_, s: (b_, jnp.maximum(s - 1, 0), 0)),
        out_shape=jax.ShapeDtypeStruct((b, seq, GROUP_W), BF16),
        scratch_shapes=[pltpu.VMEM((8 + CHUNK, 3 * GROUP_W), F32),
                        pltpu.VMEM((CHUNK, HEAD_W), F32),
                        pltpu.VMEM((HEADS, HEAD_W, HEAD_W), F32)],
        compiler_params=pltpu.CompilerParams(
            dimension_semantics=("parallel", "arbitrary"), vmem_limit_bytes=VMEM_LIMIT),
        name="gdn",
    )(proj3, proj3, proj3, proj3, proj3, proj3, lead_proj, lead_proj, lead_proj,
      gates3, lead_gates, conv_wt, conv_wt, conv_wt, alog_col, dtb_col, proj3, onw)


def _outproj_kernel(x_ref, oa_ref, od_ref, wa_ref, wd_ref, o_ref):
    o_ref[...] = x_ref[...] + _dot(oa_ref[...], wa_ref[...]) + _dot(od_ref[...], wd_ref[...])


def _outproj(x2d, oa, od, w_out16, tm):
    m = x2d.shape[0]
    assert m % tm == 0
    return pl.pallas_call(
        _outproj_kernel,
        grid=(m // tm,),
        in_specs=[
            pl.BlockSpec((tm, D_MODEL), lambda i: (i, 0)),
            pl.BlockSpec((tm, GROUP_W), lambda i: (i, 0)),
            pl.BlockSpec((tm, GROUP_W), lambda i: (i, 0)),
            pl.BlockSpec((GROUP_W, D_MODEL), lambda i: (0, 0)),
            pl.BlockSpec((GROUP_W, D_MODEL), lambda i: (1, 0)),
        ],
        out_specs=pl.BlockSpec((tm, D_MODEL), lambda i: (i, 0)),
        out_shape=jax.ShapeDtypeStruct((m, D_MODEL), F32),
        compiler_params=pltpu.CompilerParams(
            dimension_semantics=("parallel",), vmem_limit_bytes=VMEM_LIMIT),
        name="outproj",
    )(x2d, oa, od, w_out16, w_out16)


def _ffn_kernel(h_ref, nw_ref, wg_ref, wu_ref, wd_ref, o_ref, u_ref, *, row_chunk):
    j = pl.program_id(1)

    @pl.when(j == 0)
    def _():
        def body(c, carry):
            r = pl.multiple_of(c * row_chunk, row_chunk)
            x = h_ref[pl.ds(r, row_chunk), :]
            u_ref[pl.ds(r, row_chunk), :] = _rms_rows(x, nw_ref[...]).astype(BF16)
            o_ref[pl.ds(r, row_chunk), :] = x
            return carry

        lax.fori_loop(0, h_ref.shape[0] // row_chunk, body, 0)

    u = u_ref[...]
    g = _dot(u, wg_ref[...])
    a = (g * jax.nn.sigmoid(g) * _dot(u, wu_ref[...])).astype(BF16)
    o_ref[...] += _dot(a, wd_ref[...])


def _ffn(h2d, norm_w, wg, wu, wd, tm, th):
    m = h2d.shape[0]
    assert m % tm == 0 and FFN_HIDDEN % th == 0
    return pl.pallas_call(
        functools.partial(_ffn_kernel, row_chunk=min(256, tm)),
        grid=(m // tm, FFN_HIDDEN // th),
        in_specs=[
            pl.BlockSpec((tm, D_MODEL), lambda i, j: (i, 0)),
            pl.BlockSpec((1, D_MODEL), lambda i, j: (0, 0)),
            pl.BlockSpec((D_MODEL, th), lambda i, j: (0, j)),
            pl.BlockSpec((D_MODEL, th), lambda i, j: (0, j)),
            pl.BlockSpec((th, D_MODEL), lambda i, j: (j, 0)),
        ],
        out_specs=pl.BlockSpec((tm, D_MODEL), lambda i, j: (i, 0)),
        out_shape=jax.ShapeDtypeStruct((m, D_MODEL), F32),
        scratch_shapes=[pltpu.VMEM((tm, D_MODEL), BF16)],
        compiler_params=pltpu.CompilerParams(
            dimension_semantics=("parallel", "arbitrary"), vmem_limit_bytes=VMEM_LIMIT),
        name="ffn",
    )(h2d, norm_w, wg, wu, wd)


def kernel(x, meta_tokens, attn_norm_w, w_in, q_norm_w, k_norm_w, lambda_q1, lambda_k1, lambda_q2,
           lambda_k2, subln_w, conv_w, a_log, dt_bias, o_norm_w, w_out, ffn_norm_w, w_gate, w_up,
           w_down):
    b, seq, _ = x.shape
    m = b * seq
    x2d = x.reshape(m, D_MODEL)
    lead = jnp.concatenate([jnp.zeros((N_PAD, D_MODEL), x.dtype), meta_tokens.astype(x.dtype)], 0)

    w_main = w_in[0, :, :MAIN_COLS].astype(BF16)
    w_gates = jnp.pad(w_in[0, :, MAIN_COLS:], ((0, 0), (0, HEAD_W - GATE_COLS))).astype(BF16)
    tm = min(1024, m)
    proj, gates = _inproj(x2d, attn_norm_w, w_main, w_gates, tm, 1024)
    lead_proj, lead_gates = _inproj(lead, attn_norm_w, w_main, w_gates, LEAD, 1024)
    proj3 = proj.reshape(b, seq, MAIN_COLS)

    slopes = 2.0 ** (-8.0 * jnp.arange(1, HEADS + 1, dtype=F32) / HEADS)
    lvec = jnp.concatenate([lambda_q1, lambda_k1, lambda_q2, lambda_k2], 0).astype(F32)
    o_a = _attn(proj3, lead_proj, slopes, lvec, jnp.tile(q_norm_w, (1, 2)),
                jnp.tile(k_norm_w, (1, 2)), subln_w.reshape(HEAD_W, 1), min(512, seq))

    o_d = _gdn(
        proj3, lead_proj, gates.reshape(b, seq, HEAD_W), lead_gates, conv_w[0].T,
        a_log.astype(F32).reshape(HEADS, 1), dt_bias.astype(F32).reshape(HEADS, 1), o_norm_w)

    h1 = _outproj(x2d, o_a.reshape(m, GROUP_W), o_d.reshape(m, GROUP_W), w_out[0].astype(BF16),
                  min(512, m))
    out = _ffn(h1, ffn_norm_w, w_gate[0].astype(BF16), w_up[0].astype(BF16),
               w_down[0].astype(BF16), min(1024, m), 512)
    return out.reshape(b, seq, D_MODEL)
```

```python
import functools

import jax
import jax.numpy as jnp
import numpy as np
from jax import lax
from jax.experimental import pallas as pl
from jax.experimental.pallas import tpu as pltpu

F32 = jnp.float32
BF16 = jnp.bfloat16
HIGHEST = lax.Precision.HIGHEST

D_MODEL = 2048
N_META = 16
LEAD = 128
N_PAD = LEAD - N_META
HEADS = 8
HEAD_W = 128
QK_DIM = 64
GROUP_W = HEADS * HEAD_W
MAIN_COLS = 7 * GROUP_W
GATE_COLS = 2 * HEADS
CONV_K = 4
FFN_HIDDEN = 5632
EPS = 1e-6
NEG = -1e30
LAMBDA_INIT = 0.2
CHUNK = 128
VMEM_LIMIT = 56 * 1024 * 1024
LOG2E = 1.4426950408889634
VT_ROWS = HEAD_W + 16


def _bf16_pieces(x, n):
    out = []
    for _ in range(n):
        bits = np.array(x, np.float32).view(np.uint32)
        bits = (bits + 0x7FFF + ((bits >> 16) & 1)) & 0xFFFF0000
        p = float(bits.view(np.float32))
        out.append(p)
        x -= p
    return tuple(out)


LOG2E_BF16_PIECES = _bf16_pieces(LOG2E, 3)


def _dot(a, b, precision=None):
    return jnp.dot(a, b, preferred_element_type=F32, precision=precision)


def _dot_nt(a, b):
    return lax.dot_general(a, b, (((1,), (1,)), ((), ())), preferred_element_type=F32)


def _split(x):
    hi = x.astype(BF16)
    return hi, (x - hi.astype(F32)).astype(BF16)


def _dot3(a, b):
    a_hi, a_lo = a
    if isinstance(b, list):
        b_hi = jnp.concatenate([x[0] for x in b], axis=1)
        b_lo = jnp.concatenate([x[1] for x in b], axis=1)
    else:
        b_hi, b_lo = b
    return _dot(jnp.concatenate([a_hi, a_lo, a_hi], axis=1),
                jnp.concatenate([b_hi, b_hi, b_lo], axis=0))


def _rms_rows(x, w_row):
    return x * lax.rsqrt(jnp.mean(x * x, axis=-1, keepdims=True) + EPS) * w_row


def _inproj_kernel(x_ref, nw_ref, w_ref, wg_ref, o_ref, g_ref, u_ref, *, row_chunk):
    j = pl.program_id(1)

    @pl.when(j == 0)
    def _():
        def body(c, carry):
            r = pl.multiple_of(c * row_chunk, row_chunk)
            u = _rms_rows(x_ref[pl.ds(r, row_chunk), :], nw_ref[...]).astype(BF16)
            u_ref[pl.ds(r, row_chunk), :] = u
            g_ref[pl.ds(r, row_chunk), :] = _dot(u, wg_ref[...])
            return carry

        lax.fori_loop(0, x_ref.shape[0] // row_chunk, body, 0)

    o_ref[...] = _dot(u_ref[...], w_ref[...]).astype(o_ref.dtype)


def _inproj(x2d, norm_w, w_main, w_gate, tm, tn):
    m = x2d.shape[0]
    assert m % tm == 0 and MAIN_COLS % tn == 0
    row_chunk = min(256, tm)
    return pl.pallas_call(
        functools.partial(_inproj_kernel, row_chunk=row_chunk),
        grid=(m // tm, MAIN_COLS // tn),
        in_specs=[
            pl.BlockSpec((tm, D_MODEL), lambda i, j: (i, 0)),
            pl.BlockSpec((1, D_MODEL), lambda i, j: (0, 0)),
            pl.BlockSpec((D_MODEL, tn), lambda i, j: (0, j)),
            pl.BlockSpec((D_MODEL, HEAD_W), lambda i, j: (0, 0)),
        ],
        out_specs=[
            pl.BlockSpec((tm, tn), lambda i, j: (i, j)),
            pl.BlockSpec((tm, HEAD_W), lambda i, j: (i, 0)),
        ],
        out_shape=[
            jax.ShapeDtypeStruct((m, MAIN_COLS), BF16),
            jax.ShapeDtypeStruct((m, HEAD_W), F32),
        ],
        scratch_shapes=[pltpu.VMEM((tm, D_MODEL), BF16)],
        compiler_params=pltpu.CompilerParams(
            dimension_semantics=("parallel", "arbitrary"), vmem_limit_bytes=VMEM_LIMIT),
        name="inproj",
    )(x2d, norm_w, w_main, w_gate)


def _halfnorm(x, w_row):
    lo = lax.broadcasted_iota(jnp.int32, x.shape, 1) < QK_DIM
    x2 = x * x
    s_lo = jnp.sum(jnp.where(lo, x2, 0.0), axis=-1, keepdims=True)
    s_hi = jnp.sum(jnp.where(lo, 0.0, x2), axis=-1, keepdims=True)
    ms = jnp.where(lo, s_lo, s_hi) * (1.0 / QK_DIM)
    return x * lax.rsqrt(ms + EPS) * w_row


def _attn_kernel(slopes_ref, lvec_ref, q_ref, qnext_ref, k_ref, v_ref, lk_ref, lv_ref, qw_ref,
                 kw_ref, swc_ref, o_ref, kn_ref, vt_ref, lkn_ref, lvt_ref, kaug_ref, acc_ref,
                 sa_ref, sb_ref, wn_ref, wc_ref, kstage_ref, vstage_ref, *, tq, seq):
    h = pl.program_id(1)
    g = pl.program_id(2)
    slope = slopes_ref[h]

    def query_operands(src_ref, dst_ref):
        lo = lax.broadcasted_iota(jnp.int32, (tq, HEAD_W), 1) < QK_DIM
        sub = lax.broadcasted_iota(jnp.int32, (HEAD_W, tq), 0)
        aug = jnp.zeros((HEAD_W, tq), F32)
        for n, piece in enumerate(LOG2E_BF16_PIECES):
            aug = jnp.where((sub == n) | (sub == n + 3), piece, aug)
        for blk in range(2):
            q_blk = src_ref[blk * tq:(blk + 1) * tq, :].astype(F32)
            qn = _halfnorm(q_blk, qw_ref[...]) * (QK_DIM ** -0.5 * LOG2E)
            for mp, x in enumerate((jnp.where(lo, qn, 0.0), jnp.where(lo, 0.0, qn))):
                dst_ref[blk, mp] = jnp.concatenate([x.T, aug], axis=0).astype(BF16)

    def key_operands(row0, kdst_ref, kdst0, vdst_ref, vdst0):
        al = lambda x: x if isinstance(x, int) else pl.multiple_of(x, 256)
        for c in range(2 * tq // 256):
            src = pl.ds(al(row0 + c * 256), 256)
            dst = pl.ds(al(kdst0 + c * 256), 256)
            dstv = pl.ds(al(vdst0 + c * 256), 256)
            kdst_ref[dst, :] = _halfnorm(k_ref[src, :].astype(F32), kw_ref[...]).astype(BF16)
            vdst_ref[0:HEAD_W, dstv] = v_ref[src, :].astype(F32).T.astype(BF16)
            vdst_ref[HEAD_W:, dstv] = ones_row(256)

    step_rows = 2 * tq

    def ones_row(n):
        return (lax.broadcasted_iota(jnp.int32, (VT_ROWS - HEAD_W, n), 0) == 0).astype(BF16)

    @pl.when(g == 0)
    def _():
        query_operands(q_ref, wn_ref)
        key_operands(0, kn_ref, 0, vt_ref, 0)
        lkn_ref[...] = _halfnorm(lk_ref[...].astype(F32), kw_ref[...]).astype(BF16)
        lvt_ref[0:HEAD_W, :] = lv_ref[...].astype(F32).T.astype(BF16)
        lvt_ref[HEAD_W:, :] = ones_row(LEAD)
        kk = lax.broadcasted_iota(jnp.int32, (tq, HEAD_W), 0)
        ln = lax.broadcasted_iota(jnp.int32, (tq, HEAD_W), 1)
        hi = ((kk // 16) * 16).astype(F32)
        lo_ = (kk % 16).astype(F32)
        kaug_ref[...] = (slope * jnp.where(ln < 3, hi, jnp.where(ln < 6, lo_, 0.0))).astype(BF16)

    @pl.when(g > 0)
    def _():
        r = pl.multiple_of(g * step_rows, step_rows)
        kn_ref[pl.ds(r, step_rows), :] = kstage_ref[...]
        vt_ref[:, pl.ds(r, step_rows)] = vstage_ref[...]

    wc_ref[...] = wn_ref[...]
    query_operands(qnext_ref, wn_ref)
    key_operands(jnp.minimum(g + 1, seq // step_rows - 1) * step_rows, kstage_ref, 0,
                 vstage_ref, 0)

    slope2 = slope * LOG2E
    q_off = slope2 * lax.broadcasted_iota(jnp.int32, (1, tq), 1).astype(F32)
    key_ok = lax.broadcasted_iota(jnp.int32, (LEAD, tq), 0) >= N_PAD
    carries = []
    for blk in range(2):
        carry = []
        for mp in range(2):
            s = jnp.where(
                key_ok, _dot(lkn_ref[...], wc_ref[blk, mp, 0:HEAD_W, :]) + q_off, NEG)
            m = jnp.max(s, axis=0, keepdims=True)
            carry.append(m)
            acc_ref[blk, mp] = _dot(lvt_ref[...], jnp.exp2(s - m).astype(BF16))
        carries.append(tuple(carry))

    key_i = lax.broadcasted_iota(jnp.int32, (tq, tq), 0)
    qry_i = lax.broadcasted_iota(jnp.int32, (tq, tq), 1)

    def scores(j, dst_ref, blk, diag):
        r = pl.multiple_of(j * tq, tq)
        lhs = jnp.concatenate([kn_ref[pl.ds(r, tq), :], kaug_ref[...]], axis=1)
        bms = []
        for mp in range(2):
            raw = _dot(lhs, wc_ref[blk, mp])
            if diag:
                raw = jnp.where(key_i <= qry_i, raw, NEG)
            dst_ref[mp] = raw
            bms.append(jnp.max(raw, axis=0, keepdims=True))
        return tuple(bms)

    def accumulate(j, src_ref, bms, carry, blk):
        r = pl.multiple_of(j * tq, tq)
        vt = vt_ref[:, pl.ds(r, tq)]
        c = slope2 * ((j - (2 * g + blk)) * tq).astype(F32)
        out = []
        for mp in range(2):
            m_new = jnp.maximum(carry[mp], bms[mp] + c)
            alpha = jnp.exp2(carry[mp] - m_new)
            p = jnp.exp2(src_ref[mp] - (m_new - c)).astype(BF16)
            out.append(m_new)
            acc_ref[blk, mp] = alpha * acc_ref[blk, mp] + _dot(vt, p)
        return tuple(out)

    i_a, i_b = 2 * g, 2 * g + 1
    carry_a, carry_b = carries
    bm = scores(i_a, sa_ref, 0, True)
    bm_b = scores(i_a, sb_ref, 1, False)
    carry_a = accumulate(i_a, sa_ref, bm, carry_a, 0)
    bm = scores(i_b, sa_ref, 1, True)
    carry_b = accumulate(i_a, sb_ref, bm_b, carry_b, 1)

    def both(j, state):
        j_b, bm_b, carry_a, carry_b = state
        bm_a = scores(j, sb_ref, 0, False)
        carry_b = accumulate(j_b, sa_ref, bm_b, carry_b, 1)
        bm_b = scores(j, sa_ref, 1, False)
        carry_a = accumulate(j, sb_ref, bm_a, carry_a, 0)
        return j, bm_b, carry_a, carry_b

    j_b, bm_b, carry_a, carry_b = lax.fori_loop(0, i_a, both, (i_b, bm, carry_a, carry_b))
    carry_b = accumulate(j_b, sa_ref, bm_b, carry_b, 1)

    lv4 = lvec_ref[...]
    lam = (jnp.exp(jnp.sum(lv4[0:1] * lv4[1:2], axis=-1, keepdims=True))
           - jnp.exp(jnp.sum(lv4[2:3] * lv4[3:4], axis=-1, keepdims=True)) + LAMBDA_INIT)
    for blk in range(2):
        a0, a1 = acc_ref[blk, 0], acc_ref[blk, 1]
        o = (a0[0:HEAD_W] / a0[HEAD_W:HEAD_W + 1]
             - lam * (a1[0:HEAD_W] / a1[HEAD_W:HEAD_W + 1]))
        o = o * lax.rsqrt(jnp.mean(o * o, axis=0, keepdims=True) + EPS) * swc_ref[...]
        o_ref[blk * tq:(blk + 1) * tq, :] = (o * (1.0 - LAMBDA_INIT)).T.astype(o_ref.dtype)


def _attn(proj3, lead_proj, slopes, lvec, qw, kw, sw, tq):
    b, seq, _ = proj3.shape
    assert seq % (2 * tq) == 0 and seq % 256 == 0 and tq % 16 == 0 and tq <= 512
    nsteps = seq // (2 * tq)
    return pl.pallas_call(
        functools.partial(_attn_kernel, tq=tq, seq=seq),
        grid=(b, HEADS, nsteps),
        in_specs=[
            pl.BlockSpec(memory_space=pltpu.SMEM),
            pl.BlockSpec((4, QK_DIM), lambda b_, h, i: (0, 0)),
            pl.BlockSpec((None, 2 * tq, HEAD_W), lambda b_, h, i: (b_, i, h)),
            pl.BlockSpec((None, 2 * tq, HEAD_W),
                         lambda b_, h, i: (b_, jnp.minimum(i + 1, nsteps - 1), h)),
            pl.BlockSpec((None, seq, HEAD_W), lambda b_, h, i: (b_, 0, HEADS + h)),
            pl.BlockSpec((None, seq, HEAD_W), lambda b_, h, i: (b_, 0, 2 * HEADS + h)),
            pl.BlockSpec((LEAD, HEAD_W), lambda b_, h, i: (0, HEADS + h)),
            pl.BlockSpec((LEAD, HEAD_W), lambda b_, h, i: (0, 2 * HEADS + h)),
            pl.BlockSpec((1, HEAD_W), lambda b_, h, i: (0, 0)),
            pl.BlockSpec((1, HEAD_W), lambda b_, h, i: (0, 0)),
            pl.BlockSpec((HEAD_W, 1), lambda b_, h, i: (0, 0)),
        ],
        out_specs=pl.BlockSpec((None, 2 * tq, HEAD_W), lambda b_, h, i: (b_, i, h)),
        out_shape=jax.ShapeDtypeStruct((b, seq, GROUP_W), BF16),
        scratch_shapes=[
            pltpu.VMEM((seq, HEAD_W), BF16),
            pltpu.VMEM((VT_ROWS, seq), BF16),
            pltpu.VMEM((LEAD, HEAD_W), BF16),
            pltpu.VMEM((VT_ROWS, LEAD), BF16),
            pltpu.VMEM((tq, HEAD_W), BF16),
            pltpu.VMEM((2, 2, VT_ROWS, tq), F32),
            pltpu.VMEM((2, tq, tq), F32),
            pltpu.VMEM((2, tq, tq), F32),
            pltpu.VMEM((2, 2, 2 * HEAD_W, tq), BF16),
            pltpu.VMEM((2, 2, 2 * HEAD_W, tq), BF16),
            pltpu.VMEM((2 * tq, HEAD_W), BF16),
            pltpu.VMEM((VT_ROWS, 2 * tq), BF16),
        ],
        compiler_params=pltpu.CompilerParams(
            dimension_semantics=("parallel", "parallel", "arbitrary"),
            vmem_limit_bytes=VMEM_LIMIT),
        name="diff_attn",
    )(slopes, lvec, proj3, proj3, proj3, proj3, lead_proj, lead_proj, qw, kw, sw)


def _gdn_kernel(tq_ref, tk_ref, tv_ref, hq_ref, hk_ref, hv_ref, lq_ref, lk_ref, lv_ref,
                tg_ref, lg_ref, cwq_ref, cwk_ref, cwv_ref, alog_ref, dtb_ref, z_ref, nw_ref,
                o_ref, xs_ref, gs_ref, s_ref):
    s = pl.program_id(1)
    is_lead = s == 0
    width = GROUP_W
    srcs = ((tq_ref, hq_ref, lq_ref), (tk_ref, hk_ref, lk_ref), (tv_ref, hv_ref, lv_ref))

    @pl.when(is_lead)
    def _():
        rowid = lax.broadcasted_iota(jnp.int32, (CHUNK, width), 0)
        for idx, (_, _, l_ref) in enumerate(srcs):
            cs = slice(idx * width, (idx + 1) * width)
            xs_ref[0:8, cs] = jnp.zeros((8, width), F32)
            xs_ref[8:8 + CHUNK, cs] = jnp.where(rowid >= N_PAD, l_ref[...].astype(F32), 0.0)
        gs_ref[...] = lg_ref[...]
        s_ref[...] = jnp.zeros_like(s_ref)

    @pl.when(s == 1)
    def _():
        for idx, (t_ref, _, l_ref) in enumerate(srcs):
            cs = slice(idx * width, (idx + 1) * width)
            xs_ref[0:8, cs] = l_ref[LEAD - 16:LEAD, :].astype(F32)[8:16]
            xs_ref[8:8 + CHUNK, cs] = t_ref[...].astype(F32)
        gs_ref[...] = tg_ref[...]

    @pl.when(s > 1)
    def _():
        for idx, (t_ref, h_ref, _) in enumerate(srcs):
            cs = slice(idx * width, (idx + 1) * width)
            xs_ref[0:8, cs] = h_ref[...].astype(F32)[8:16]
            xs_ref[8:8 + CHUNK, cs] = t_ref[...].astype(F32)
        gs_ref[...] = tg_ref[...]

    rowi = lax.broadcasted_iota(jnp.int32, (CHUNK, CHUNK), 0)
    lanei = lax.broadcasted_iota(jnp.int32, (CHUNK, CHUNK), 1)
    incl = rowi >= lanei
    eye = (rowi == lanei).astype(F32)

    g_t = gs_ref[...].T[0:2 * HEADS]
    vmask = (lax.broadcasted_iota(jnp.int32, (HEADS, CHUNK), 1)
             >= jnp.where(is_lead, N_PAD, 0)).astype(F32)
    beta_t = jax.nn.sigmoid(g_t[0:HEADS]) * vmask
    t = g_t[HEADS:] + dtb_ref[...]
    softplus = jnp.maximum(t, 0.0) + jnp.log(1.0 + jnp.exp(-jnp.abs(t)))
    decay_t = -jnp.exp(alog_ref[...]) * softplus * vmask
    gc_t = _dot(decay_t, (rowi <= lanei).astype(F32), HIGHEST)
    cols = jnp.concatenate(
        [beta_t, gc_t, jnp.zeros((CHUNK - 2 * HEADS, CHUNK), F32)], axis=0).T

    heads = range(HEADS)
    hcols = [slice(h * HEAD_W, (h + 1) * HEAD_W) for h in heads]
    mks, pks, rhs, qgs, qkds, decs = [], [], [], [], [], []
    for hh in heads:
        hs = hcols[hh]
        beta = cols[:, hh:hh + 1]
        gc = cols[:, HEADS + hh:HEADS + hh + 1]
        gc_row = gc_t[hh:hh + 1]
        g_last = gc_row[:, CHUNK - 1:CHUNK]

        def conv_silu(idx, cw_ref):
            c0 = idx * width + hh * HEAD_W
            y = xs_ref[5:5 + CHUNK, c0:c0 + HEAD_W] * cw_ref[0:1, hs]
            for j in range(1, CONV_K):
                y = y + xs_ref[5 + j:5 + j + CHUNK, c0:c0 + HEAD_W] * cw_ref[j:j + 1, hs]
            return y * jax.nn.sigmoid(y)

        q = conv_silu(0, cwq_ref)
        k = conv_silu(1, cwk_ref)
        v = conv_silu(2, cwv_ref)
        q = q * lax.rsqrt(jnp.sum(q * q, axis=-1, keepdims=True) + EPS) * (HEAD_W ** -0.5)
        k = k * lax.rsqrt(jnp.sum(k * k, axis=-1, keepdims=True) + EPS)

        decay = jnp.where(incl, jnp.exp(jnp.where(incl, gc - gc_row, 0.0)), 0.0)
        kb = k * beta
        k_t = k.T
        kt16 = k_t.astype(BF16)
        lmat = jnp.where(rowi > lanei, _dot(kb.astype(BF16), kt16) * decay, 0.0)
        qkds.append(jnp.concatenate(
            [(_dot(q.astype(BF16), kt16) * decay).astype(BF16),
             (k_t * jnp.exp(g_last - gc_row)).astype(BF16)], axis=0))
        qgs.append((q * jnp.exp(gc)).astype(BF16))
        decs.append(jnp.exp(g_last))
        mks.append(-lmat)
        pks.append(eye - lmat)
        rhs.append(jnp.concatenate([v * beta, kb * jnp.exp(gc)], axis=1))

    ms = [_split(m) for m in mks]
    mks = [_dot3(m, m) for m in ms]
    for _ in range(5):
        ms = [_split(m) for m in mks]
        rs = [_dot3(m, [_split(p), m]) for m, p in zip(ms, pks)]
        pks = [p + r[:, :CHUNK] for p, r in zip(pks, rs)]
        mks = [r[:, CHUNK:] for r in rs]
    pks = [p + _dot3(_split(m), _split(p)) for m, p in zip(mks, pks)]
    uws = [_dot3(_split(p), _split(r)) for p, r in zip(pks, rhs)]

    states = [s_ref[h] for h in heads]
    ws_qs = [_dot(jnp.concatenate([uws[h][:, HEAD_W:].astype(BF16), qgs[h]], axis=0),
                  states[h].astype(BF16)) for h in heads]
    v16 = [(uws[h][:, :HEAD_W] - ws_qs[h][:CHUNK]).astype(BF16) for h in heads]
    qv_kv = [_dot(qkds[h], v16[h]) for h in heads]
    for h in heads:
        s_ref[h] = decs[h] * states[h] + qv_kv[h][CHUNK:]
        o = ws_qs[h][CHUNK:] + qv_kv[h][:CHUNK]
        z = z_ref[:, hcols[h]].astype(F32)
        o_ref[:, hcols[h]] = (
            _rms_rows(o, nw_ref[...]) * (z * jax.nn.sigmoid(z))).astype(o_ref.dtype)


def _gdn(proj3, lead_proj, gates3, lead_gates, conv_wt, alog_col, dtb_col, onw):
    b, seq, _ = proj3.shape
    nb = 1 + seq // CHUNK
    tok = lambda g: pl.BlockSpec(
        (None, CHUNK, GROUP_W), lambda b_, s: (b_, jnp.maximum(s - 1, 0), g))
    halo = lambda g: pl.BlockSpec(
        (None, 16, GROUP_W),
        lambda b_, s: (b_, jnp.maximum((s - 1) * (CHUNK // 16) - 1, 0), g))
    lead = lambda g: pl.BlockSpec((LEAD, GROUP_W), lambda b_, s: (0, g))
    cw = lambda g: pl.BlockSpec((CONV_K, GROUP_W), lambda b_, s: (0, g))
    col = pl.BlockSpec((HEADS, 1), lambda b_, s: (0, 0))
    return pl.pallas_call(
        _gdn_kernel,
        grid=(b, nb),
        in_specs=[
            tok(3), tok(4), tok(5), halo(3), halo(4), halo(5), lead(3), lead(4), lead(5),
            pl.BlockSpec((None, CHUNK, HEAD_W), lambda b_, s: (b_, jnp.maximum(s - 1, 0), 0)),
            pl.BlockSpec((LEAD, HEAD_W), lambda b_, s: (0, 0)),
            cw(0), cw(1), cw(2), col, col, tok(6),
            pl.BlockSpec((1, HEAD_W), lambda b_, s: (0, 0)),
        ],
        out_specs=pl.BlockSpec((None, CHUNK, GROUP_W), lambda b_, s: (b_, jnp.maximum(s - 1, 0), 0)),
        out_shape=jax.ShapeDtypeStruct((b, seq, GROUP_W), BF16),
        scratch_shapes=[pltpu.VMEM((8 + CHUNK, 3 * GROUP_W), F32),
                        pltpu.VMEM((CHUNK, HEAD_W), F32),
                        pltpu.VMEM((HEADS, HEAD_W, HEAD_W), F32)],
        compiler_params=pltpu.CompilerParams(
            dimension_semantics=("parallel", "arbitrary"), vmem_limit_bytes=VMEM_LIMIT),
        name="gdn",
    )(proj3, proj3, proj3, proj3, proj3, proj3, lead_proj, lead_proj, lead_proj,
      gates3, lead_gates, conv_wt, conv_wt, conv_wt, alog_col, dtb_col, proj3, onw)


def _outproj_kernel(x_ref, oa_ref, od_ref, wa_ref, wd_ref, o_ref):
    o_ref[...] = x_ref[...] + _dot(oa_ref[...], wa_ref[...]) + _dot(od_ref[...], wd_ref[...])


def _outproj(x2d, oa, od, w_out16, tm):
    m = x2d.shape[0]
    assert m % tm == 0
    return pl.pallas_call(
        _outproj_kernel,
        grid=(m // tm,),
        in_specs=[
            pl.BlockSpec((tm, D_MODEL), lambda i: (i, 0)),
            pl.BlockSpec((tm, GROUP_W), lambda i: (i, 0)),
            pl.BlockSpec((tm, GROUP_W), lambda i: (i, 0)),
            pl.BlockSpec((GROUP_W, D_MODEL), lambda i: (0, 0)),
            pl.BlockSpec((GROUP_W, D_MODEL), lambda i: (1, 0)),
        ],
        out_specs=pl.BlockSpec((tm, D_MODEL), lambda i: (i, 0)),
        out_shape=jax.ShapeDtypeStruct((m, D_MODEL), F32),
        compiler_params=pltpu.CompilerParams(
            dimension_semantics=("parallel",), vmem_limit_bytes=VMEM_LIMIT),
        name="outproj",
    )(x2d, oa, od, w_out16, w_out16)


def _ffn_kernel(h_ref, nw_ref, wg_ref, wu_ref, wd_ref, o_ref, u_ref, *, row_chunk):
    j = pl.program_id(1)

    @pl.when(j == 0)
    def _():
        def body(c, carry):
            r = pl.multiple_of(c * row_chunk, row_chunk)
            x = h_ref[pl.ds(r, row_chunk), :]
            u_ref[pl.ds(r, row_chunk), :] = _rms_rows(x, nw_ref[...]).astype(BF16)
            o_ref[pl.ds(r, row_chunk), :] = x
            return carry

        lax.fori_loop(0, h_ref.shape[0] // row_chunk, body, 0)

    u = u_ref[...]
    g = _dot(u, wg_ref[...])
    a = (g * jax.nn.sigmoid(g) * _dot(u, wu_ref[...])).astype(BF16)
    o_ref[...] += _dot(a, wd_ref[...])


def _ffn(h2d, norm_w, wg, wu, wd, tm, th):
    m = h2d.shape[0]
    assert m % tm == 0 and FFN_HIDDEN % th == 0
    return pl.pallas_call(
        functools.partial(_ffn_kernel, row_chunk=min(256, tm)),
        grid=(m // tm, FFN_HIDDEN // th),
        in_specs=[
            pl.BlockSpec((tm, D_MODEL), lambda i, j: (i, 0)),
            pl.BlockSpec((1, D_MODEL), lambda i, j: (0, 0)),
            pl.BlockSpec((D_MODEL, th), lambda i, j: (0, j)),
            pl.BlockSpec((D_MODEL, th), lambda i, j: (0, j)),
            pl.BlockSpec((th, D_MODEL), lambda i, j: (j, 0)),
        ],
        out_specs=pl.BlockSpec((tm, D_MODEL), lambda i, j: (i, 0)),
        out_shape=jax.ShapeDtypeStruct((m, D_MODEL), F32),
        scratch_shapes=[pltpu.VMEM((tm, D_MODEL), BF16)],
        compiler_params=pltpu.CompilerParams(
            dimension_semantics=("parallel", "arbitrary"), vmem_limit_bytes=VMEM_LIMIT),
        name="ffn",
    )(h2d, norm_w, wg, wu, wd)


def kernel(x, meta_tokens, attn_norm_w, w_in, q_norm_w, k_norm_w, lambda_q1, lambda_k1, lambda_q2,
           lambda_k2, subln_w, conv_w, a_log, dt_bias, o_norm_w, w_out, ffn_norm_w, w_gate, w_up,
           w_down):
    b, seq, _ = x.shape
    m = b * seq
    x2d = x.reshape(m, D_MODEL)
    lead = jnp.concatenate([jnp.zeros((N_PAD, D_MODEL), x.dtype), meta_tokens.astype(x.dtype)], 0)

    w_main = w_in[0, :, :MAIN_COLS].astype(BF16)
    w_gates = jnp.pad(w_in[0, :, MAIN_COLS:], ((0, 0), (0, HEAD_W - GATE_COLS))).astype(BF16)
    tm = min(1024, m)
    proj, gates = _inproj(x2d, attn_norm_w, w_main, w_gates, tm, 1024)
    lead_proj, lead_gates = _inproj(lead, attn_norm_w, w_main, w_gates, LEAD, 1024)
    proj3 = proj.reshape(b, seq, MAIN_COLS)

    slopes = 2.0 ** (-8.0 * jnp.arange(1, HEADS + 1, dtype=F32) / HEADS)
    lvec = jnp.concatenate([lambda_q1, lambda_k1, lambda_q2, lambda_k2], 0).astype(F32)
    o_a = _attn(proj3, lead_proj, slopes, lvec, jnp.tile(q_norm_w, (1, 2)),
                jnp.tile(k_norm_w, (1, 2)), subln_w.reshape(HEAD_W, 1), min(512, seq))

    o_d = _gdn(
        proj3, lead_proj, gates.reshape(b, seq, HEAD_W), lead_gates, conv_w[0].T,
        a_log.astype(F32).reshape(HEADS, 1), dt_bias.astype(F32).reshape(HEADS, 1), o_norm_w)

    h1 = _outproj(x2d, o_a.reshape(m, GROUP_W), o_d.reshape(m, GROUP_W), w_out[0].astype(BF16),
                  min(512, m))
    out = _ffn(h1, ffn_norm_w, w_gate[0].astype(BF16), w_up[0].astype(BF16),
               w_down[0].astype(BF16), min(1024, m), 512)
    return out.reshape(b, seq, D_MODEL)
```

```python
import functools

import jax
import jax.numpy as jnp
import numpy as np
from jax import lax
from jax.experimental import pallas as pl
from jax.experimental.pallas import tpu as pltpu

F32 = jnp.float32
BF16 = jnp.bfloat16
HIGHEST = lax.Precision.HIGHEST

D_MODEL = 2048
N_META = 16
LEAD = 128
N_PAD = LEAD - N_META
HEADS = 8
HEAD_W = 128
QK_DIM = 64
GROUP_W = HEADS * HEAD_W
MAIN_COLS = 7 * GROUP_W
GATE_COLS = 2 * HEADS
CONV_K = 4
HIST = 16
FFN_HIDDEN = 5632
EPS = 1e-6
NEG = -1e30
LAMBDA_INIT = 0.2
CHUNK = 128
VMEM_LIMIT = 56 * 1024 * 1024
LOG2E = 1.4426950408889634
VT_ROWS = HEAD_W + 16


def _bf16_pieces(x, n):
    out = []
    for _ in range(n):
        bits = np.array(x, np.float32).view(np.uint32)
        bits = (bits + 0x7FFF + ((bits >> 16) & 1)) & 0xFFFF0000
        p = float(bits.view(np.float32))
        out.append(p)
        x -= p
    return tuple(out)


LOG2E_BF16_PIECES = _bf16_pieces(LOG2E, 3)


def _dot(a, b, precision=None):
    return jnp.dot(a, b, preferred_element_type=F32, precision=precision)


def _dot_nt(a, b):
    return lax.dot_general(a, b, (((1,), (1,)), ((), ())), preferred_element_type=F32)


def _split(x):
    hi = x.astype(BF16)
    return hi, (x - hi.astype(F32)).astype(BF16)


def _dot3(a, b):
    a_hi, a_lo = a
    if isinstance(b, list):
        b_hi = jnp.concatenate([x[0] for x in b], axis=1)
        b_lo = jnp.concatenate([x[1] for x in b], axis=1)
    else:
        b_hi, b_lo = b
    return _dot(jnp.concatenate([a_hi, a_lo, a_hi], axis=1),
                jnp.concatenate([b_hi, b_hi, b_lo], axis=0))


def _rms_rows(x, w_row):
    return x * lax.rsqrt(jnp.mean(x * x, axis=-1, keepdims=True) + EPS) * w_row


def _inproj_kernel(x_ref, nw_ref, w_ref, wg_ref, o_ref, g_ref, u_ref, *, row_chunk):
    j = pl.program_id(1)

    @pl.when(j == 0)
    def _():
        def body(c, carry):
            r = pl.multiple_of(c * row_chunk, row_chunk)
            u = _rms_rows(x_ref[pl.ds(r, row_chunk), :], nw_ref[...]).astype(BF16)
            u_ref[pl.ds(r, row_chunk), :] = u
            g_ref[pl.ds(r, row_chunk), :] = _dot(u, wg_ref[...])
            return carry

        lax.fori_loop(0, x_ref.shape[0] // row_chunk, body, 0)

    o_ref[...] = _dot(u_ref[...], w_ref[...]).astype(o_ref.dtype)


def _inproj(x2d, norm_w, w_main, w_gate, tm, tn):
    m = x2d.shape[0]
    assert m % tm == 0 and MAIN_COLS % tn == 0
    row_chunk = min(256, tm)
    return pl.pallas_call(
        functools.partial(_inproj_kernel, row_chunk=row_chunk),
        grid=(m // tm, MAIN_COLS // tn),
        in_specs=[
            pl.BlockSpec((tm, D_MODEL), lambda i, j: (i, 0)),
            pl.BlockSpec((1, D_MODEL), lambda i, j: (0, 0)),
            pl.BlockSpec((D_MODEL, tn), lambda i, j: (0, j)),
            pl.BlockSpec((D_MODEL, HEAD_W), lambda i, j: (0, 0)),
        ],
        out_specs=[
            pl.BlockSpec((tm, tn), lambda i, j: (i, j)),
            pl.BlockSpec((tm, HEAD_W), lambda i, j: (i, 0)),
        ],
        out_shape=[
            jax.ShapeDtypeStruct((m, MAIN_COLS), BF16),
            jax.ShapeDtypeStruct((m, HEAD_W), F32),
        ],
        scratch_shapes=[pltpu.VMEM((tm, D_MODEL), BF16)],
        compiler_params=pltpu.CompilerParams(
            dimension_semantics=("parallel", "arbitrary"), vmem_limit_bytes=VMEM_LIMIT),
        name="inproj",
    )(x2d, norm_w, w_main, w_gate)


def _halfnorm(x, w_row):
    lo = lax.broadcasted_iota(jnp.int32, x.shape, 1) < QK_DIM
    x2 = x * x
    s_lo = jnp.sum(jnp.where(lo, x2, 0.0), axis=-1, keepdims=True)
    s_hi = jnp.sum(jnp.where(lo, 0.0, x2), axis=-1, keepdims=True)
    ms = jnp.where(lo, s_lo, s_hi) * (1.0 / QK_DIM)
    return x * lax.rsqrt(ms + EPS) * w_row


def _attn_kernel(slopes_ref, lvec_ref, q_ref, qnext_ref, k_ref, v_ref, lk_ref, lv_ref, qw_ref,
                 kw_ref, swc_ref, o_ref, kn_ref, vt_ref, lkn_ref, lvt_ref, kaug_ref, acc_ref,
                 sa_ref, sb_ref, wn_ref, wc_ref, kstage_ref, vstage_ref, *, tq, seq):
    h = pl.program_id(1)
    g = pl.program_id(2)
    slope = slopes_ref[h]

    def query_operands(src_ref, dst_ref):
        lo = lax.broadcasted_iota(jnp.int32, (tq, HEAD_W), 1) < QK_DIM
        sub = lax.broadcasted_iota(jnp.int32, (HEAD_W, tq), 0)
        aug = jnp.zeros((HEAD_W, tq), F32)
        for n, piece in enumerate(LOG2E_BF16_PIECES):
            aug = jnp.where((sub == n) | (sub == n + 3), piece, aug)
        for blk in range(2):
            q_blk = src_ref[blk * tq:(blk + 1) * tq, :].astype(F32)
            qn = _halfnorm(q_blk, qw_ref[...]) * (QK_DIM ** -0.5 * LOG2E)
            for mp, x in enumerate((jnp.where(lo, qn, 0.0), jnp.where(lo, 0.0, qn))):
                dst_ref[blk, mp] = jnp.concatenate([x.T, aug], axis=0).astype(BF16)

    def key_operands(row0, kdst_ref, kdst0, vdst_ref, vdst0):
        al = lambda x: x if isinstance(x, int) else pl.multiple_of(x, 256)
        for c in range(2 * tq // 256):
            src = pl.ds(al(row0 + c * 256), 256)
            dst = pl.ds(al(kdst0 + c * 256), 256)
            dstv = pl.ds(al(vdst0 + c * 256), 256)
            kdst_ref[dst, :] = _halfnorm(k_ref[src, :].astype(F32), kw_ref[...]).astype(BF16)
            vdst_ref[0:HEAD_W, dstv] = v_ref[src, :].astype(F32).T.astype(BF16)
            vdst_ref[HEAD_W:, dstv] = ones_row(256)

    step_rows = 2 * tq

    def ones_row(n):
        return (lax.broadcasted_iota(jnp.int32, (VT_ROWS - HEAD_W, n), 0) == 0).astype(BF16)

    @pl.when(g == 0)
    def _():
        query_operands(q_ref, wn_ref)
        key_operands(0, kn_ref, 0, vt_ref, 0)
        lkn_ref[...] = _halfnorm(lk_ref[...].astype(F32), kw_ref[...]).astype(BF16)
        lvt_ref[0:HEAD_W, :] = lv_ref[...].astype(F32).T.astype(BF16)
        lvt_ref[HEAD_W:, :] = ones_row(LEAD)
        kk = lax.broadcasted_iota(jnp.int32, (tq, HEAD_W), 0)
        ln = lax.broadcasted_iota(jnp.int32, (tq, HEAD_W), 1)
        hi = ((kk // 16) * 16).astype(F32)
        lo_ = (kk % 16).astype(F32)
        kaug_ref[...] = (slope * jnp.where(ln < 3, hi, jnp.where(ln < 6, lo_, 0.0))).astype(BF16)

    @pl.when(g > 0)
    def _():
        r = pl.multiple_of(g * step_rows, step_rows)
        kn_ref[pl.ds(r, step_rows), :] = kstage_ref[...]
        vt_ref[:, pl.ds(r, step_rows)] = vstage_ref[...]

    wc_ref[...] = wn_ref[...]
    query_operands(qnext_ref, wn_ref)
    key_operands(jnp.minimum(g + 1, seq // step_rows - 1) * step_rows, kstage_ref, 0,
                 vstage_ref, 0)

    slope2 = slope * LOG2E
    q_off = slope2 * lax.broadcasted_iota(jnp.int32, (1, tq), 1).astype(F32)
    key_ok = lax.broadcasted_iota(jnp.int32, (LEAD, tq), 0) >= N_PAD
    carries = []
    for blk in range(2):
        carry = []
        for mp in range(2):
            s = jnp.where(
                key_ok, _dot(lkn_ref[...], wc_ref[blk, mp, 0:HEAD_W, :]) + q_off, NEG)
            m = jnp.max(s, axis=0, keepdims=True)
            carry.append(m)
            acc_ref[blk, mp] = _dot(lvt_ref[...], jnp.exp2(s - m).astype(BF16))
        carries.append(tuple(carry))

    key_i = lax.broadcasted_iota(jnp.int32, (tq, tq), 0)
    qry_i = lax.broadcasted_iota(jnp.int32, (tq, tq), 1)

    def scores(j, dst_ref, blk, diag):
        r = pl.multiple_of(j * tq, tq)
        lhs = jnp.concatenate([kn_ref[pl.ds(r, tq), :], kaug_ref[...]], axis=1)
        bms = []
        for mp in range(2):
            raw = _dot(lhs, wc_ref[blk, mp])
            if diag:
                raw = jnp.where(key_i <= qry_i, raw, NEG)
            dst_ref[mp] = raw
            bms.append(jnp.max(raw, axis=0, keepdims=True))
        return tuple(bms)

    def accumulate(j, src_ref, bms, carry, blk):
        r = pl.multiple_of(j * tq, tq)
        vt = vt_ref[:, pl.ds(r, tq)]
        c = slope2 * ((j - (2 * g + blk)) * tq).astype(F32)
        out = []
        for mp in range(2):
            m_new = jnp.maximum(carry[mp], bms[mp] + c)
            alpha = jnp.exp2(carry[mp] - m_new)
            p = jnp.exp2(src_ref[mp] - (m_new - c)).astype(BF16)
            out.append(m_new)
            acc_ref[blk, mp] = alpha * acc_ref[blk, mp] + _dot(vt, p)
        return tuple(out)

    i_a, i_b = 2 * g, 2 * g + 1
    carry_a, carry_b = carries
    bm = scores(i_a, sa_ref, 0, True)
    bm_b = scores(i_a, sb_ref, 1, False)
    carry_a = accumulate(i_a, sa_ref, bm, carry_a, 0)
    bm = scores(i_b, sa_ref, 1, True)
    carry_b = accumulate(i_a, sb_ref, bm_b, carry_b, 1)

    def both(j, state):
        j_b, bm_b, carry_a, carry_b = state
        bm_a = scores(j, sb_ref, 0, False)
        carry_b = accumulate(j_b, sa_ref, bm_b, carry_b, 1)
        bm_b = scores(j, sa_ref, 1, False)
        carry_a = accumulate(j, sb_ref, bm_a, carry_a, 0)
        return j, bm_b, carry_a, carry_b

    j_b, bm_b, carry_a, carry_b = lax.fori_loop(0, i_a, both, (i_b, bm, carry_a, carry_b))
    carry_b = accumulate(j_b, sa_ref, bm_b, carry_b, 1)

    lv4 = lvec_ref[...]
    lam = (jnp.exp(jnp.sum(lv4[0:1] * lv4[1:2], axis=-1, keepdims=True))
           - jnp.exp(jnp.sum(lv4[2:3] * lv4[3:4], axis=-1, keepdims=True)) + LAMBDA_INIT)
    for blk in range(2):
        a0, a1 = acc_ref[blk, 0], acc_ref[blk, 1]
        o = (a0[0:HEAD_W] / a0[HEAD_W:HEAD_W + 1]
             - lam * (a1[0:HEAD_W] / a1[HEAD_W:HEAD_W + 1]))
        o = o * lax.rsqrt(jnp.mean(o * o, axis=0, keepdims=True) + EPS) * swc_ref[...]
        o_ref[blk * tq:(blk + 1) * tq, :] = (o * (1.0 - LAMBDA_INIT)).T.astype(o_ref.dtype)


def _attn(proj3, lead_proj, slopes, lvec, qw, kw, sw, tq):
    b, seq, _ = proj3.shape
    assert seq % (2 * tq) == 0 and seq % 256 == 0 and tq % 16 == 0 and tq <= 512
    nsteps = seq // (2 * tq)
    return pl.pallas_call(
        functools.partial(_attn_kernel, tq=tq, seq=seq),
        grid=(b, HEADS, nsteps),
        in_specs=[
            pl.BlockSpec(memory_space=pltpu.SMEM),
            pl.BlockSpec((4, QK_DIM), lambda b_, h, i: (0, 0)),
            pl.BlockSpec((None, 2 * tq, HEAD_W), lambda b_, h, i: (b_, i, h)),
            pl.BlockSpec((None, 2 * tq, HEAD_W),
                         lambda b_, h, i: (b_, jnp.minimum(i + 1, nsteps - 1), h)),
            pl.BlockSpec((None, seq, HEAD_W), lambda b_, h, i: (b_, 0, HEADS + h)),
            pl.BlockSpec((None, seq, HEAD_W), lambda b_, h, i: (b_, 0, 2 * HEADS + h)),
            pl.BlockSpec((LEAD, HEAD_W), lambda b_, h, i: (0, HEADS + h)),
            pl.BlockSpec((LEAD, HEAD_W), lambda b_, h, i: (0, 2 * HEADS + h)),
            pl.BlockSpec((1, HEAD_W), lambda b_, h, i: (0, 0)),
            pl.BlockSpec((1, HEAD_W), lambda b_, h, i: (0, 0)),
            pl.BlockSpec((HEAD_W, 1), lambda b_, h, i: (0, 0)),
        ],
        out_specs=pl.BlockSpec((None, 2 * tq, HEAD_W), lambda b_, h, i: (b_, i, h)),
        out_shape=jax.ShapeDtypeStruct((b, seq, GROUP_W), BF16),
        scratch_shapes=[
            pltpu.VMEM((seq, HEAD_W), BF16),
            pltpu.VMEM((VT_ROWS, seq), BF16),
            pltpu.VMEM((LEAD, HEAD_W), BF16),
            pltpu.VMEM((VT_ROWS, LEAD), BF16),
            pltpu.VMEM((tq, HEAD_W), BF16),
            pltpu.VMEM((2, 2, VT_ROWS, tq), F32),
            pltpu.VMEM((2, tq, tq), F32),
            pltpu.VMEM((2, tq, tq), F32),
            pltpu.VMEM((2, 2, 2 * HEAD_W, tq), BF16),
            pltpu.VMEM((2, 2, 2 * HEAD_W, tq), BF16),
            pltpu.VMEM((2 * tq, HEAD_W), BF16),
            pltpu.VMEM((VT_ROWS, 2 * tq), BF16),
        ],
        compiler_params=pltpu.CompilerParams(
            dimension_semantics=("parallel", "parallel", "arbitrary"),
            vmem_limit_bytes=VMEM_LIMIT),
        name="diff_attn",
    )(slopes, lvec, proj3, proj3, proj3, proj3, lead_proj, lead_proj, qw, kw, sw)


def _gdn_kernel(tq_ref, tk_ref, tv_ref, hq_ref, hk_ref, hv_ref, lq_ref, lk_ref, lv_ref,
                tg_ref, lg_ref, cwq_ref, cwk_ref, cwv_ref, alog_ref, dtb_ref, z_ref, nw_ref,
                o_ref, xs_ref, gs_ref, s_ref):
    s = pl.program_id(1)
    is_lead = s == 0
    width = GROUP_W
    srcs = ((tq_ref, hq_ref, lq_ref), (tk_ref, hk_ref, lk_ref), (tv_ref, hv_ref, lv_ref))

    @pl.when(is_lead)
    def _():
        rowid = lax.broadcasted_iota(jnp.int32, (CHUNK, width), 0)
        for idx, (_, _, l_ref) in enumerate(srcs):
            cs = slice(idx * width, (idx + 1) * width)
            xs_ref[0:HIST, cs] = jnp.zeros((HIST, width), BF16)
            xs_ref[HIST:, cs] = jnp.where(rowid >= N_PAD, l_ref[...], jnp.zeros((), BF16))
        gs_ref[...] = lg_ref[...]
        s_ref[...] = jnp.zeros_like(s_ref)

    @pl.when(s == 1)
    def _():
        for idx, (t_ref, _, l_ref) in enumerate(srcs):
            cs = slice(idx * width, (idx + 1) * width)
            xs_ref[0:HIST, cs] = l_ref[LEAD - HIST:LEAD, :]
            xs_ref[HIST:, cs] = t_ref[...]
        gs_ref[...] = tg_ref[...]

    @pl.when(s > 1)
    def _():
        for idx, (t_ref, h_ref, _) in enumerate(srcs):
            cs = slice(idx * width, (idx + 1) * width)
            xs_ref[0:HIST, cs] = h_ref[...]
            xs_ref[HIST:, cs] = t_ref[...]
        gs_ref[...] = tg_ref[...]

    sel_r = lax.broadcasted_iota(jnp.int32, ((CONV_K - 1) * CHUNK, HIST + CHUNK), 0)
    sel_c = lax.broadcasted_iota(jnp.int32, ((CONV_K - 1) * CHUNK, HIST + CHUNK), 1)
    tap_of = sel_r // CHUNK
    shifted = _dot((sel_c == HIST + sel_r % CHUNK - (CONV_K - 1 - tap_of)).astype(BF16),
                   xs_ref[...])

    rowi = lax.broadcasted_iota(jnp.int32, (CHUNK, CHUNK), 0)
    lanei = lax.broadcasted_iota(jnp.int32, (CHUNK, CHUNK), 1)
    incl = rowi >= lanei
    eye = (rowi == lanei).astype(F32)

    g_t = gs_ref[...].T[0:2 * HEADS]
    vmask = (lax.broadcasted_iota(jnp.int32, (HEADS, CHUNK), 1)
             >= jnp.where(is_lead, N_PAD, 0)).astype(F32)
    beta_t = jax.nn.sigmoid(g_t[0:HEADS]) * vmask
    t = g_t[HEADS:] + dtb_ref[...]
    softplus = jnp.maximum(t, 0.0) + jnp.log(1.0 + jnp.exp(-jnp.abs(t)))
    decay_t = -jnp.exp(alog_ref[...]) * softplus * vmask
    gc_t = _dot(decay_t, (rowi <= lanei).astype(F32), HIGHEST)
    cols = jnp.concatenate(
        [beta_t, gc_t, jnp.zeros((CHUNK - 2 * HEADS, CHUNK), F32)], axis=0).T

    heads = range(HEADS)
    hcols = [slice(h * HEAD_W, (h + 1) * HEAD_W) for h in heads]
    mks, pks, rhs, qgs, qkds, decs = [], [], [], [], [], []
    for hh in heads:
        hs = hcols[hh]
        beta = cols[:, hh:hh + 1]
        gc = cols[:, HEADS + hh:HEADS + hh + 1]
        gc_row = gc_t[hh:hh + 1]
        g_last = gc_row[:, CHUNK - 1:CHUNK]

        def conv_silu(idx, cw_ref):
            c0 = idx * width + hh * HEAD_W
            y = xs_ref[HIST:, c0:c0 + HEAD_W].astype(F32) * cw_ref[CONV_K - 1:CONV_K, hs]
            for j in range(CONV_K - 1):
                y = y + shifted[j * CHUNK:(j + 1) * CHUNK, c0:c0 + HEAD_W] * cw_ref[j:j + 1, hs]
            return y * jax.nn.sigmoid(y)

        q = conv_silu(0, cwq_ref)
        k = conv_silu(1, cwk_ref)
        v = conv_silu(2, cwv_ref)
        q = q * lax.rsqrt(jnp.sum(q * q, axis=-1, keepdims=True) + EPS) * (HEAD_W ** -0.5)
        k = k * lax.rsqrt(jnp.sum(k * k, axis=-1, keepdims=True) + EPS)

        decay = jnp.where(incl, jnp.exp(jnp.where(incl, gc - gc_row, 0.0)), 0.0)
        kb = k * beta
        k_t = k.T
        kt16 = k_t.astype(BF16)
        lmat = jnp.where(rowi > lanei, _dot(kb.astype(BF16), kt16) * decay, 0.0)
        qkds.append(jnp.concatenate(
            [(_dot(q.astype(BF16), kt16) * decay).astype(BF16),
             (k_t * jnp.exp(g_last - gc_row)).astype(BF16)], axis=0))
        qgs.append((q * jnp.exp(gc)).astype(BF16))
        decs.append(jnp.exp(g_last))
        mks.append(-lmat)
        pks.append(eye - lmat)
        rhs.append(jnp.concatenate([v * beta, kb * jnp.exp(gc)], axis=1))

    ms = [_split(m) for m in mks]
    mks = [_dot3(m, m) for m in ms]
    for _ in range(5):
        ms = [_split(m) for m in mks]
        rs = [_dot3(m, [_split(p), m]) for m, p in zip(ms, pks)]
        pks = [p + r[:, :CHUNK] for p, r in zip(pks, rs)]
        mks = [r[:, CHUNK:] for r in rs]
    pks = [p + _dot3(_split(m), _split(p)) for m, p in zip(mks, pks)]
    uws = [_dot3(_split(p), _split(r)) for p, r in zip(pks, rhs)]

    states = [s_ref[h] for h in heads]
    ws_qs = [_dot(jnp.concatenate([uws[h][:, HEAD_W:].astype(BF16), qgs[h]], axis=0),
                  states[h].astype(BF16)) for h in heads]
    v16 = [(uws[h][:, :HEAD_W] - ws_qs[h][:CHUNK]).astype(BF16) for h in heads]
    qv_kv = [_dot(qkds[h], v16[h]) for h in heads]
    for h in heads:
        s_ref[h] = decs[h] * states[h] + qv_kv[h][CHUNK:]
        o = ws_qs[h][CHUNK:] + qv_kv[h][:CHUNK]
        z = z_ref[:, hcols[h]].astype(F32)
        o_ref[:, hcols[h]] = (
            _rms_rows(o, nw_ref[...]) * (z * jax.nn.sigmoid(z))).astype(o_ref.dtype)


def _gdn(proj3, lead_proj, gates3, lead_gates, conv_wt, alog_col, dtb_col, onw):
    b, seq, _ = proj3.shape
    nb = 1 + seq // CHUNK
    tok = lambda g: pl.BlockSpec(
        (None, CHUNK, GROUP_W), lambda b_, s: (b_, jnp.maximum(s - 1, 0), g))
    halo = lambda g: pl.BlockSpec(
        (None, 16, GROUP_W),
        lambda b_, s: (b_, jnp.maximum((s - 1) * (CHUNK // 16) - 1, 0), g))
    lead = lambda g: pl.BlockSpec((LEAD, GROUP_W), lambda b_, s: (0, g))
    cw = lambda g: pl.BlockSpec((CONV_K, GROUP_W), lambda b_, s: (0, g))
    col = pl.BlockSpec((HEADS, 1), lambda b_, s: (0, 0))
    return pl.pallas_call(
        _gdn_kernel,
        grid=(b, nb),
        in_specs=[
            tok(3), tok(4), tok(5), halo(3), halo(4), halo(5), lead(3), lead(4), lead(5),
            pl.BlockSpec((None, CHUNK, HEAD_W), lambda b_, s: (b_, jnp.maximum(s - 1, 0), 0)),
            pl.BlockSpec((LEAD, HEAD_W), lambda b_, s: (0, 0)),
            cw(0), cw(1), cw(2), col, col, tok(6),
            pl.BlockSpec((1, HEAD_W), lambda b_, s: (0, 0)),
        ],
        out_specs=pl.BlockSpec((None, CHUNK, GROUP_W), lambda b_, s: (b_, jnp.maximum(s - 1, 0), 0)),
        out_shape=jax.ShapeDtypeStruct((b, seq, GROUP_W), BF16),
        scratch_shapes=[pltpu.VMEM((HIST + CHUNK, 3 * GROUP_W), BF16),
                        pltpu.VMEM((CHUNK, HEAD_W), F32),
                        pltpu.VMEM((HEADS, HEAD_W, HEAD_W), F32)],
        compiler_params=pltpu.CompilerParams(
            dimension_semantics=("parallel", "arbitrary"), vmem_limit_bytes=VMEM_LIMIT),
        name="gdn",
    )(proj3, proj3, proj3, proj3, proj3, proj3, lead_proj, lead_proj, lead_proj,
      gates3, lead_gates, conv_wt, conv_wt, conv_wt, alog_col, dtb_col, proj3, onw)


def _outproj_kernel(x_ref, oa_ref, od_ref, wa_ref, wd_ref, o_ref):
    o_ref[...] = x_ref[...] + _dot(oa_ref[...], wa_ref[...]) + _dot(od_ref[...], wd_ref[...])


def _outproj(x2d, oa, od, w_out16, tm):
    m = x2d.shape[0]
    assert m % tm == 0
    return pl.pallas_call(
        _outproj_kernel,
        grid=(m // tm,),
        in_specs=[
            pl.BlockSpec((tm, D_MODEL), lambda i: (i, 0)),
            pl.BlockSpec((tm, GROUP_W), lambda i: (i, 0)),
            pl.BlockSpec((tm, GROUP_W), lambda i: (i, 0)),
            pl.BlockSpec((GROUP_W, D_MODEL), lambda i: (0, 0)),
            pl.BlockSpec((GROUP_W, D_MODEL), lambda i: (1, 0)),
        ],
        out_specs=pl.BlockSpec((tm, D_MODEL), lambda i: (i, 0)),
        out_shape=jax.ShapeDtypeStruct((m, D_MODEL), F32),
        compiler_params=pltpu.CompilerParams(
            dimension_semantics=("parallel",), vmem_limit_bytes=VMEM_LIMIT),
        name="outproj",
    )(x2d, oa, od, w_out16, w_out16)


def _ffn_kernel(h_ref, nw_ref, wg_ref, wu_ref, wd_ref, o_ref, u_ref, *, row_chunk):
    j = pl.program_id(1)

    @pl.when(j == 0)
    def _():
        def body(c, carry):
            r = pl.multiple_of(c * row_chunk, row_chunk)
            x = h_ref[pl.ds(r, row_chunk), :]
            u_ref[pl.ds(r, row_chunk), :] = _rms_rows(x, nw_ref[...]).astype(BF16)
            o_ref[pl.ds(r, row_chunk), :] = x
            return carry

        lax.fori_loop(0, h_ref.shape[0] // row_chunk, body, 0)

    u = u_ref[...]
    g = _dot(u, wg_ref[...])
    a = (g * jax.nn.sigmoid(g) * _dot(u, wu_ref[...])).astype(BF16)
    o_ref[...] += _dot(a, wd_ref[...])


def _ffn(h2d, norm_w, wg, wu, wd, tm, th):
    m = h2d.shape[0]
    assert m % tm == 0 and FFN_HIDDEN % th == 0
    return pl.pallas_call(
        functools.partial(_ffn_kernel, row_chunk=min(256, tm)),
        grid=(m // tm, FFN_HIDDEN // th),
        in_specs=[
            pl.BlockSpec((tm, D_MODEL), lambda i, j: (i, 0)),
            pl.BlockSpec((1, D_MODEL), lambda i, j: (0, 0)),
            pl.BlockSpec((D_MODEL, th), lambda i, j: (0, j)),
            pl.BlockSpec((D_MODEL, th), lambda i, j: (0, j)),
            pl.BlockSpec((th, D_MODEL), lambda i, j: (j, 0)),
        ],
        out_specs=pl.BlockSpec((tm, D_MODEL), lambda i, j: (i, 0)),
        out_shape=jax.ShapeDtypeStruct((m, D_MODEL), F32),
        scratch_shapes=[pltpu.VMEM((tm, D_MODEL), BF16)],
        compiler_params=pltpu.CompilerParams(
            dimension_semantics=("parallel", "arbitrary"), vmem_limit_bytes=VMEM_LIMIT),
        name="ffn",
    )(h2d, norm_w, wg, wu, wd)


def kernel(x, meta_tokens, attn_norm_w, w_in, q_norm_w, k_norm_w, lambda_q1, lambda_k1, lambda_q2,
           lambda_k2, subln_w, conv_w, a_log, dt_bias, o_norm_w, w_out, ffn_norm_w, w_gate, w_up,
           w_down):
    b, seq, _ = x.shape
    m = b * seq
    x2d = x.reshape(m, D_MODEL)
    lead = jnp.concatenate([jnp.zeros((N_PAD, D_MODEL), x.dtype), meta_tokens.astype(x.dtype)], 0)

    w_main = w_in[0, :, :MAIN_COLS].astype(BF16)
    w_gates = jnp.pad(w_in[0, :, MAIN_COLS:], ((0, 0), (0, HEAD_W - GATE_COLS))).astype(BF16)
    tm = min(1024, m)
    proj, gates = _inproj(x2d, attn_norm_w, w_main, w_gates, tm, 1792)
    lead_proj, lead_gates = _inproj(lead, attn_norm_w, w_main, w_gates, LEAD, 1024)
    proj3 = proj.reshape(b, seq, MAIN_COLS)

    slopes = 2.0 ** (-8.0 * jnp.arange(1, HEADS + 1, dtype=F32) / HEADS)
    lvec = jnp.concatenate([lambda_q1, lambda_k1, lambda_q2, lambda_k2], 0).astype(F32)
    o_a = _attn(proj3, lead_proj, slopes, lvec, jnp.tile(q_norm_w, (1, 2)),
                jnp.tile(k_norm_w, (1, 2)), subln_w.reshape(HEAD_W, 1), min(512, seq))

    o_d = _gdn(
        proj3, lead_proj, gates.reshape(b, seq, HEAD_W), lead_gates, conv_w[0].T,
        a_log.astype(F32).reshape(HEADS, 1), dt_bias.astype(F32).reshape(HEADS, 1), o_norm_w)

    h1 = _outproj(x2d, o_a.reshape(m, GROUP_W), o_d.reshape(m, GROUP_W), w_out[0].astype(BF16),
                  min(512, m))
    out = _ffn(h1, ffn_norm_w, w_gate[0].astype(BF16), w_up[0].astype(BF16),
               w_down[0].astype(BF16), min(1024, m), 512)
    return out.reshape(b, seq, D_MODEL)
```

```python
import functools

import jax
import jax.numpy as jnp
import numpy as np
from jax import lax
from jax.experimental import pallas as pl
from jax.experimental.pallas import tpu as pltpu

F32 = jnp.float32
BF16 = jnp.bfloat16
HIGHEST = lax.Precision.HIGHEST

D_MODEL = 2048
N_META = 16
LEAD = 128
N_PAD = LEAD - N_META
HEADS = 8
HEAD_W = 128
QK_DIM = 64
GROUP_W = HEADS * HEAD_W
MAIN_COLS = 7 * GROUP_W
GATE_COLS = 2 * HEADS
CONV_K = 4
HIST = 16
FFN_HIDDEN = 5632
EPS = 1e-6
NEG = -1e30
LAMBDA_INIT = 0.2
CHUNK = 128
VMEM_LIMIT = 56 * 1024 * 1024
LOG2E = 1.4426950408889634
ATTN_NSUB = 4
VT_ROWS = HEAD_W + 16


def _bf16_pieces(x, n):
    out = []
    for _ in range(n):
        bits = np.array(x, np.float32).view(np.uint32)
        bits = (bits + 0x7FFF + ((bits >> 16) & 1)) & 0xFFFF0000
        p = float(bits.view(np.float32))
        out.append(p)
        x -= p
    return tuple(out)


LOG2E_BF16_PIECES = _bf16_pieces(LOG2E, 3)


def _dot(a, b, precision=None):
    return jnp.dot(a, b, preferred_element_type=F32, precision=precision)


def _dot_nt(a, b):
    return lax.dot_general(a, b, (((1,), (1,)), ((), ())), preferred_element_type=F32)


def _split(x):
    hi = x.astype(BF16)
    return hi, (x - hi.astype(F32)).astype(BF16)


def _dot3(a, b):
    a_hi, a_lo = a
    if isinstance(b, list):
        b_hi = jnp.concatenate([x[0] for x in b], axis=1)
        b_lo = jnp.concatenate([x[1] for x in b], axis=1)
    else:
        b_hi, b_lo = b
    return _dot(jnp.concatenate([a_hi, a_lo, a_hi], axis=1),
                jnp.concatenate([b_hi, b_hi, b_lo], axis=0))


def _rms_rows(x, w_row):
    return x * lax.rsqrt(jnp.mean(x * x, axis=-1, keepdims=True) + EPS) * w_row


def _inproj_kernel(x_ref, nw_ref, w_ref, wg_ref, o_ref, g_ref, u_ref, *, row_chunk):
    j = pl.program_id(1)

    @pl.when(j == 0)
    def _():
        def body(c, carry):
            r = pl.multiple_of(c * row_chunk, row_chunk)
            u = _rms_rows(x_ref[pl.ds(r, row_chunk), :], nw_ref[...]).astype(BF16)
            u_ref[pl.ds(r, row_chunk), :] = u
            g_ref[pl.ds(r, row_chunk), :] = _dot(u, wg_ref[...])
            return carry

        lax.fori_loop(0, x_ref.shape[0] // row_chunk, body, 0)

    o_ref[...] = _dot(u_ref[...], w_ref[...]).astype(o_ref.dtype)


def _inproj(x2d, norm_w, w_main, w_gate, tm, tn):
    m = x2d.shape[0]
    assert m % tm == 0 and MAIN_COLS % tn == 0
    row_chunk = min(256, tm)
    return pl.pallas_call(
        functools.partial(_inproj_kernel, row_chunk=row_chunk),
        grid=(m // tm, MAIN_COLS // tn),
        in_specs=[
            pl.BlockSpec((tm, D_MODEL), lambda i, j: (i, 0)),
            pl.BlockSpec((1, D_MODEL), lambda i, j: (0, 0)),
            pl.BlockSpec((D_MODEL, tn), lambda i, j: (0, j)),
            pl.BlockSpec((D_MODEL, HEAD_W), lambda i, j: (0, 0)),
        ],
        out_specs=[
            pl.BlockSpec((tm, tn), lambda i, j: (i, j)),
            pl.BlockSpec((tm, HEAD_W), lambda i, j: (i, 0)),
        ],
        out_shape=[
            jax.ShapeDtypeStruct((m, MAIN_COLS), BF16),
            jax.ShapeDtypeStruct((m, HEAD_W), F32),
        ],
        scratch_shapes=[pltpu.VMEM((tm, D_MODEL), BF16)],
        compiler_params=pltpu.CompilerParams(
            dimension_semantics=("parallel", "arbitrary"), vmem_limit_bytes=VMEM_LIMIT),
        name="inproj",
    )(x2d, norm_w, w_main, w_gate)


def _halfnorm(x, w_row):
    lo = lax.broadcasted_iota(jnp.int32, x.shape, 1) < QK_DIM
    x2 = x * x
    s_lo = jnp.sum(jnp.where(lo, x2, 0.0), axis=-1, keepdims=True)
    s_hi = jnp.sum(jnp.where(lo, 0.0, x2), axis=-1, keepdims=True)
    ms = jnp.where(lo, s_lo, s_hi) * (1.0 / QK_DIM)
    return x * lax.rsqrt(ms + EPS) * w_row


def _attn_kernel(slopes_ref, lvec_ref, q_ref, qnext_ref, k_ref, v_ref, lk_ref, lv_ref, qw_ref,
                 kw_ref, swc_ref, o_ref, kn_ref, vt_ref, lkn_ref, lvt_ref, kaug_ref, acc_ref,
                 sa_ref, sb_ref, wn_ref, wc_ref, kstage_ref, vstage_ref, *, tq, seq, nsub):
    h = pl.program_id(1)
    g = pl.program_id(2)
    step_rows = nsub * tq
    slope = slopes_ref[h]

    def query_operands(src_ref, dst_ref):
        lo = lax.broadcasted_iota(jnp.int32, (tq, HEAD_W), 1) < QK_DIM
        sub = lax.broadcasted_iota(jnp.int32, (HEAD_W, tq), 0)
        aug = jnp.zeros((HEAD_W, tq), F32)
        for n, piece in enumerate(LOG2E_BF16_PIECES):
            aug = jnp.where((sub == n) | (sub == n + 3), piece, aug)
        for blk in range(nsub):
            q_blk = src_ref[blk * tq:(blk + 1) * tq, :].astype(F32)
            qn = _halfnorm(q_blk, qw_ref[...]) * (QK_DIM ** -0.5 * LOG2E)
            for mp, x in enumerate((jnp.where(lo, qn, 0.0), jnp.where(lo, 0.0, qn))):
                dst_ref[blk, mp] = jnp.concatenate([x.T, aug], axis=0).astype(BF16)

    def key_operands(row0, kdst_ref, kdst0, vdst_ref, vdst0):
        al = lambda x: x if isinstance(x, int) else pl.multiple_of(x, 256)
        for c in range(step_rows // 256):
            src = pl.ds(al(row0 + c * 256), 256)
            dst = pl.ds(al(kdst0 + c * 256), 256)
            dstv = pl.ds(al(vdst0 + c * 256), 256)
            kdst_ref[dst, :] = _halfnorm(k_ref[src, :].astype(F32), kw_ref[...]).astype(BF16)
            vdst_ref[0:HEAD_W, dstv] = v_ref[src, :].astype(F32).T.astype(BF16)
            vdst_ref[HEAD_W:, dstv] = ones_row(256)

    def ones_row(n):
        return (lax.broadcasted_iota(jnp.int32, (VT_ROWS - HEAD_W, n), 0) == 0).astype(BF16)

    @pl.when(g == 0)
    def _():
        query_operands(q_ref, wn_ref)
        key_operands(0, kn_ref, 0, vt_ref, 0)
        lkn_ref[...] = _halfnorm(lk_ref[...].astype(F32), kw_ref[...]).astype(BF16)
        lvt_ref[0:HEAD_W, :] = lv_ref[...].astype(F32).T.astype(BF16)
        lvt_ref[HEAD_W:, :] = ones_row(LEAD)
        kk = lax.broadcasted_iota(jnp.int32, (tq, HEAD_W), 0)
        ln = lax.broadcasted_iota(jnp.int32, (tq, HEAD_W), 1)
        hi = ((kk // 16) * 16).astype(F32)
        lo_ = (kk % 16).astype(F32)
        kaug_ref[...] = (slope * jnp.where(ln < 3, hi, jnp.where(ln < 6, lo_, 0.0))).astype(BF16)

    @pl.when(g > 0)
    def _():
        r = pl.multiple_of(g * step_rows, step_rows)
        kn_ref[pl.ds(r, step_rows), :] = kstage_ref[...]
        vt_ref[:, pl.ds(r, step_rows)] = vstage_ref[...]

    wc_ref[...] = wn_ref[...]
    query_operands(qnext_ref, wn_ref)
    key_operands(jnp.minimum(g + 1, seq // step_rows - 1) * step_rows, kstage_ref, 0,
                 vstage_ref, 0)

    slope2 = slope * LOG2E
    q_off = slope2 * lax.broadcasted_iota(jnp.int32, (1, tq), 1).astype(F32)
    key_ok = lax.broadcasted_iota(jnp.int32, (LEAD, tq), 0) >= N_PAD
    carries = []
    for blk in range(nsub):
        carry = []
        for mp in range(2):
            s = jnp.where(
                key_ok, _dot(lkn_ref[...], wc_ref[blk, mp, 0:HEAD_W, :]) + q_off, NEG)
            m = jnp.max(s, axis=0, keepdims=True)
            carry.append(m)
            acc_ref[blk, mp] = _dot(lvt_ref[...], jnp.exp2(s - m).astype(BF16))
        carries.append(tuple(carry))

    key_i = lax.broadcasted_iota(jnp.int32, (tq, tq), 0)
    qry_i = lax.broadcasted_iota(jnp.int32, (tq, tq), 1)

    def scores(j, dst_ref, blk, diag):
        r = pl.multiple_of(j * tq, tq)
        lhs = jnp.concatenate([kn_ref[pl.ds(r, tq), :], kaug_ref[...]], axis=1)
        bms = []
        for mp in range(2):
            raw = _dot(lhs, wc_ref[blk, mp])
            if diag:
                raw = jnp.where(key_i <= qry_i, raw, NEG)
            dst_ref[mp] = raw
            bms.append(jnp.max(raw, axis=0, keepdims=True))
        return tuple(bms)

    def accumulate(j, src_ref, bms, carry, blk):
        r = pl.multiple_of(j * tq, tq)
        vt = vt_ref[:, pl.ds(r, tq)]
        c = slope2 * ((j - (nsub * g + blk)) * tq).astype(F32)
        out = []
        for mp in range(2):
            m_new = jnp.maximum(carry[mp], bms[mp] + c)
            alpha = jnp.exp2(carry[mp] - m_new)
            p = jnp.exp2(src_ref[mp] - (m_new - c)).astype(BF16)
            out.append(m_new)
            acc_ref[blk, mp] = alpha * acc_ref[blk, mp] + _dot(vt, p)
        return tuple(out)

    base = nsub * g
    own = [(blk, kb) for blk in range(nsub) for kb in range(blk + 1)]
    bufs = (sa_ref, sb_ref)
    bm_prev = scores(base + own[0][1], bufs[0], own[0][0], own[0][1] == own[0][0])
    for t in range(1, len(own)):
        (blk, kb), (pblk, pkb) = own[t], own[t - 1]
        bm = scores(base + kb, bufs[t % 2], blk, kb == blk)
        carries[pblk] = accumulate(base + pkb, bufs[(t - 1) % 2], bm_prev, carries[pblk], pblk)
        bm_prev = bm
    pend = (len(own) - 1) % 2

    def earlier(j, state):
        j_pend, bm_pend, cs = state[0], state[1], list(state[2])
        for blk in range(nsub):
            bm = scores(j, bufs[(pend + 1 + blk) % 2], blk, False)
            pblk, pj = (nsub - 1, j_pend) if blk == 0 else (blk - 1, j)
            cs[pblk] = accumulate(pj, bufs[(pend + blk) % 2], bm_pend, cs[pblk], pblk)
            bm_pend = bm
        return j, bm_pend, tuple(cs)

    j_pend, bm_pend, cs = lax.fori_loop(
        0, base, earlier, (base + nsub - 1, bm_prev, tuple(carries)))
    accumulate(j_pend, bufs[pend], bm_pend, cs[nsub - 1], nsub - 1)

    lv4 = lvec_ref[...]
    lam = (jnp.exp(jnp.sum(lv4[0:1] * lv4[1:2], axis=-1, keepdims=True))
           - jnp.exp(jnp.sum(lv4[2:3] * lv4[3:4], axis=-1, keepdims=True)) + LAMBDA_INIT)
    for blk in range(nsub):
        a0, a1 = acc_ref[blk, 0], acc_ref[blk, 1]
        o = (a0[0:HEAD_W] / a0[HEAD_W:HEAD_W + 1]
             - lam * (a1[0:HEAD_W] / a1[HEAD_W:HEAD_W + 1]))
        o = o * lax.rsqrt(jnp.mean(o * o, axis=0, keepdims=True) + EPS) * swc_ref[...]
        o_ref[blk * tq:(blk + 1) * tq, :] = (o * (1.0 - LAMBDA_INIT)).T.astype(o_ref.dtype)


def _attn(proj3, lead_proj, slopes, lvec, qw, kw, sw, tq, nsub):
    b, seq, _ = proj3.shape
    rows = nsub * tq
    assert seq % rows == 0 and rows % 256 == 0 and tq % 16 == 0 and tq <= 512 and nsub % 2 == 0
    nsteps = seq // rows
    return pl.pallas_call(
        functools.partial(_attn_kernel, tq=tq, seq=seq, nsub=nsub),
        grid=(b, HEADS, nsteps),
        in_specs=[
            pl.BlockSpec(memory_space=pltpu.SMEM),
            pl.BlockSpec((4, QK_DIM), lambda b_, h, i: (0, 0)),
            pl.BlockSpec((None, rows, HEAD_W), lambda b_, h, i: (b_, i, h)),
            pl.BlockSpec((None, rows, HEAD_W),
                         lambda b_, h, i: (b_, jnp.minimum(i + 1, nsteps - 1), h)),
            pl.BlockSpec((None, seq, HEAD_W), lambda b_, h, i: (b_, 0, HEADS + h)),
            pl.BlockSpec((None, seq, HEAD_W), lambda b_, h, i: (b_, 0, 2 * HEADS + h)),
            pl.BlockSpec((LEAD, HEAD_W), lambda b_, h, i: (0, HEADS + h)),
            pl.BlockSpec((LEAD, HEAD_W), lambda b_, h, i: (0, 2 * HEADS + h)),
            pl.BlockSpec((1, HEAD_W), lambda b_, h, i: (0, 0)),
            pl.BlockSpec((1, HEAD_W), lambda b_, h, i: (0, 0)),
            pl.BlockSpec((HEAD_W, 1), lambda b_, h, i: (0, 0)),
        ],
        out_specs=pl.BlockSpec((None, rows, HEAD_W), lambda b_, h, i: (b_, i, h)),
        out_shape=jax.ShapeDtypeStruct((b, seq, GROUP_W), BF16),
        scratch_shapes=[
            pltpu.VMEM((seq, HEAD_W), BF16),
            pltpu.VMEM((VT_ROWS, seq), BF16),
            pltpu.VMEM((LEAD, HEAD_W), BF16),
            pltpu.VMEM((VT_ROWS, LEAD), BF16),
            pltpu.VMEM((tq, HEAD_W), BF16),
            pltpu.VMEM((nsub, 2, VT_ROWS, tq), F32),
            pltpu.VMEM((2, tq, tq), F32),
            pltpu.VMEM((2, tq, tq), F32),
            pltpu.VMEM((nsub, 2, 2 * HEAD_W, tq), BF16),
            pltpu.VMEM((nsub, 2, 2 * HEAD_W, tq), BF16),
            pltpu.VMEM((rows, HEAD_W), BF16),
            pltpu.VMEM((VT_ROWS, rows), BF16),
        ],
        compiler_params=pltpu.CompilerParams(
            dimension_semantics=("parallel", "parallel", "arbitrary"),
            vmem_limit_bytes=VMEM_LIMIT),
        name="diff_attn",
    )(slopes, lvec, proj3, proj3, proj3, proj3, lead_proj, lead_proj, qw, kw, sw)


def _gdn_kernel(tq_ref, tk_ref, tv_ref, hq_ref, hk_ref, hv_ref, lq_ref, lk_ref, lv_ref,
                tg_ref, lg_ref, cwq_ref, cwk_ref, cwv_ref, alog_ref, dtb_ref, z_ref, nw_ref,
                o_ref, xs_ref, gs_ref, s_ref):
    s = pl.program_id(1)
    is_lead = s == 0
    width = GROUP_W
    srcs = ((tq_ref, hq_ref, lq_ref), (tk_ref, hk_ref, lk_ref), (tv_ref, hv_ref, lv_ref))

    @pl.when(is_lead)
    def _():
        rowid = lax.broadcasted_iota(jnp.int32, (CHUNK, width), 0)
        for idx, (_, _, l_ref) in enumerate(srcs):
            cs = slice(idx * width, (idx + 1) * width)
            xs_ref[0:HIST, cs] = jnp.zeros((HIST, width), BF16)
            xs_ref[HIST:, cs] = jnp.where(rowid >= N_PAD, l_ref[...], jnp.zeros((), BF16))
        gs_ref[...] = lg_ref[...]
        s_ref[...] = jnp.zeros_like(s_ref)

    @pl.when(s == 1)
    def _():
        for idx, (t_ref, _, l_ref) in enumerate(srcs):
            cs = slice(idx * width, (idx + 1) * width)
            xs_ref[0:HIST, cs] = l_ref[LEAD - HIST:LEAD, :]
            xs_ref[HIST:, cs] = t_ref[...]
        gs_ref[...] = tg_ref[...]

    @pl.when(s > 1)
    def _():
        for idx, (t_ref, h_ref, _) in enumerate(srcs):
            cs = slice(idx * width, (idx + 1) * width)
            xs_ref[0:HIST, cs] = h_ref[...]
            xs_ref[HIST:, cs] = t_ref[...]
        gs_ref[...] = tg_ref[...]

    sel_r = lax.broadcasted_iota(jnp.int32, ((CONV_K - 1) * CHUNK, HIST + CHUNK), 0)
    sel_c = lax.broadcasted_iota(jnp.int32, ((CONV_K - 1) * CHUNK, HIST + CHUNK), 1)
    tap_of = sel_r // CHUNK
    shifted = _dot((sel_c == HIST + sel_r % CHUNK - (CONV_K - 1 - tap_of)).astype(BF16),
                   xs_ref[...])

    rowi = lax.broadcasted_iota(jnp.int32, (CHUNK, CHUNK), 0)
    lanei = lax.broadcasted_iota(jnp.int32, (CHUNK, CHUNK), 1)
    incl = rowi >= lanei
    eye = (rowi == lanei).astype(F32)

    g_t = gs_ref[...].T[0:2 * HEADS]
    vmask = (lax.broadcasted_iota(jnp.int32, (HEADS, CHUNK), 1)
             >= jnp.where(is_lead, N_PAD, 0)).astype(F32)
    beta_t = jax.nn.sigmoid(g_t[0:HEADS]) * vmask
    t = g_t[HEADS:] + dtb_ref[...]
    softplus = jnp.maximum(t, 0.0) + jnp.log(1.0 + jnp.exp(-jnp.abs(t)))
    decay_t = -jnp.exp(alog_ref[...]) * softplus * vmask
    gc_t = _dot(decay_t, (rowi <= lanei).astype(F32), HIGHEST)
    cols = jnp.concatenate(
        [beta_t, gc_t, jnp.zeros((CHUNK - 2 * HEADS, CHUNK), F32)], axis=0).T

    heads = range(HEADS)
    hcols = [slice(h * HEAD_W, (h + 1) * HEAD_W) for h in heads]
    mks, pks, rhs, qgs, qkds, decs = [], [], [], [], [], []
    for hh in heads:
        hs = hcols[hh]
        beta = cols[:, hh:hh + 1]
        gc = cols[:, HEADS + hh:HEADS + hh + 1]
        gc_row = gc_t[hh:hh + 1]
        g_last = gc_row[:, CHUNK - 1:CHUNK]

        def conv_silu(idx, cw_ref):
            c0 = idx * width + hh * HEAD_W
            y = xs_ref[HIST:, c0:c0 + HEAD_W].astype(F32) * cw_ref[CONV_K - 1:CONV_K, hs]
            for j in range(CONV_K - 1):
                y = y + shifted[j * CHUNK:(j + 1) * CHUNK, c0:c0 + HEAD_W] * cw_ref[j:j + 1, hs]
            return y * jax.nn.sigmoid(y)

        q = conv_silu(0, cwq_ref)
        k = conv_silu(1, cwk_ref)
        v = conv_silu(2, cwv_ref)
        q = q * lax.rsqrt(jnp.sum(q * q, axis=-1, keepdims=True) + EPS) * (HEAD_W ** -0.5)
        k = k * lax.rsqrt(jnp.sum(k * k, axis=-1, keepdims=True) + EPS)

        decay = jnp.where(incl, jnp.exp(jnp.where(incl, gc - gc_row, 0.0)), 0.0)
        kb = k * beta
        k_t = k.T
        kt16 = k_t.astype(BF16)
        lmat = jnp.where(rowi > lanei, _dot(kb.astype(BF16), kt16) * decay, 0.0)
        qkds.append(jnp.concatenate(
            [(_dot(q.astype(BF16), kt16) * decay).astype(BF16),
             (k_t * jnp.exp(g_last - gc_row)).astype(BF16)], axis=0))
        qgs.append((q * jnp.exp(gc)).astype(BF16))
        decs.append(jnp.exp(g_last))
        mks.append(-lmat)
        pks.append(eye - lmat)
        rhs.append(jnp.concatenate([v * beta, kb * jnp.exp(gc)], axis=1))

    ms = [_split(m) for m in mks]
    mks = [_dot3(m, m) for m in ms]
    for _ in range(5):
        ms = [_split(m) for m in mks]
        rs = [_dot3(m, [_split(p), m]) for m, p in zip(ms, pks)]
        pks = [p + r[:, :CHUNK] for p, r in zip(pks, rs)]
        mks = [r[:, CHUNK:] for r in rs]
    pks = [p + _dot3(_split(m), _split(p)) for m, p in zip(mks, pks)]
    uws = [_dot3(_split(p), _split(r)) for p, r in zip(pks, rhs)]

    states = [s_ref[h] for h in heads]
    ws_qs = [_dot(jnp.concatenate([uws[h][:, HEAD_W:].astype(BF16), qgs[h]], axis=0),
                  states[h].astype(BF16)) for h in heads]
    v16 = [(uws[h][:, :HEAD_W] - ws_qs[h][:CHUNK]).astype(BF16) for h in heads]
    qv_kv = [_dot(qkds[h], v16[h]) for h in heads]
    for h in heads:
        s_ref[h] = decs[h] * states[h] + qv_kv[h][CHUNK:]
        o = ws_qs[h][CHUNK:] + qv_kv[h][:CHUNK]
        z = z_ref[:, hcols[h]].astype(F32)
        o_ref[:, hcols[h]] = (
            _rms_rows(o, nw_ref[...]) * (z * jax.nn.sigmoid(z))).astype(o_ref.dtype)


def _gdn(proj3, lead_proj, gates3, lead_gates, conv_wt, alog_col, dtb_col, onw):
    b, seq, _ = proj3.shape
    nb = 1 + seq // CHUNK
    tok = lambda g: pl.BlockSpec(
        (None, CHUNK, GROUP_W), lambda b_, s: (b_, jnp.maximum(s - 1, 0), g))
    halo = lambda g: pl.BlockSpec(
        (None, 16, GROUP_W),
        lambda b_, s: (b_, jnp.maximum((s - 1) * (CHUNK // 16) - 1, 0), g))
    lead = lambda g: pl.BlockSpec((LEAD, GROUP_W), lambda b_, s: (0, g))
    cw = lambda g: pl.BlockSpec((CONV_K, GROUP_W), lambda b_, s: (0, g))
    col = pl.BlockSpec((HEADS, 1), lambda b_, s: (0, 0))
    return pl.pallas_call(
        _gdn_kernel,
        grid=(b, nb),
        in_specs=[
            tok(3), tok(4), tok(5), halo(3), halo(4), halo(5), lead(3), lead(4), lead(5),
            pl.BlockSpec((None, CHUNK, HEAD_W), lambda b_, s: (b_, jnp.maximum(s - 1, 0), 0)),
            pl.BlockSpec((LEAD, HEAD_W), lambda b_, s: (0, 0)),
            cw(0), cw(1), cw(2), col, col, tok(6),
            pl.BlockSpec((1, HEAD_W), lambda b_, s: (0, 0)),
        ],
        out_specs=pl.BlockSpec((None, CHUNK, GROUP_W), lambda b_, s: (b_, jnp.maximum(s - 1, 0), 0)),
        out_shape=jax.ShapeDtypeStruct((b, seq, GROUP_W), BF16),
        scratch_shapes=[pltpu.VMEM((HIST + CHUNK, 3 * GROUP_W), BF16),
                        pltpu.VMEM((CHUNK, HEAD_W), F32),
                        pltpu.VMEM((HEADS, HEAD_W, HEAD_W), F32)],
        compiler_params=pltpu.CompilerParams(
            dimension_semantics=("parallel", "arbitrary"), vmem_limit_bytes=VMEM_LIMIT),
        name="gdn",
    )(proj3, proj3, proj3, proj3, proj3, proj3, lead_proj, lead_proj, lead_proj,
      gates3, lead_gates, conv_wt, conv_wt, conv_wt, alog_col, dtb_col, proj3, onw)


def _outproj_kernel(x_ref, oa_ref, od_ref, wa_ref, wd_ref, o_ref):
    o_ref[...] = x_ref[...] + _dot(oa_ref[...], wa_ref[...]) + _dot(od_ref[...], wd_ref[...])


def _outproj(x2d, oa, od, w_out16, tm):
    m = x2d.shape[0]
    assert m % tm == 0
    return pl.pallas_call(
        _outproj_kernel,
        grid=(m // tm,),
        in_specs=[
            pl.BlockSpec((tm, D_MODEL), lambda i: (i, 0)),
            pl.BlockSpec((tm, GROUP_W), lambda i: (i, 0)),
            pl.BlockSpec((tm, GROUP_W), lambda i: (i, 0)),
            pl.BlockSpec((GROUP_W, D_MODEL), lambda i: (0, 0)),
            pl.BlockSpec((GROUP_W, D_MODEL), lambda i: (1, 0)),
        ],
        out_specs=pl.BlockSpec((tm, D_MODEL), lambda i: (i, 0)),
        out_shape=jax.ShapeDtypeStruct((m, D_MODEL), F32),
        compiler_params=pltpu.CompilerParams(
            dimension_semantics=("parallel",), vmem_limit_bytes=VMEM_LIMIT),
        name="outproj",
    )(x2d, oa, od, w_out16, w_out16)


def _ffn_kernel(h_ref, nw_ref, wg_ref, wu_ref, wd_ref, o_ref, u_ref, *, row_chunk):
    j = pl.program_id(1)

    @pl.when(j == 0)
    def _():
        def body(c, carry):
            r = pl.multiple_of(c * row_chunk, row_chunk)
            x = h_ref[pl.ds(r, row_chunk), :]
            u_ref[pl.ds(r, row_chunk), :] = _rms_rows(x, nw_ref[...]).astype(BF16)
            o_ref[pl.ds(r, row_chunk), :] = x
            return carry

        lax.fori_loop(0, h_ref.shape[0] // row_chunk, body, 0)

    u = u_ref[...]
    g = _dot(u, wg_ref[...])
    a = (g * jax.nn.sigmoid(g) * _dot(u, wu_ref[...])).astype(BF16)
    o_ref[...] += _dot(a, wd_ref[...])


def _ffn(h2d, norm_w, wg, wu, wd, tm, th):
    m = h2d.shape[0]
    assert m % tm == 0 and FFN_HIDDEN % th == 0
    return pl.pallas_call(
        functools.partial(_ffn_kernel, row_chunk=min(256, tm)),
        grid=(m // tm, FFN_HIDDEN // th),
        in_specs=[
            pl.BlockSpec((tm, D_MODEL), lambda i, j: (i, 0)),
            pl.BlockSpec((1, D_MODEL), lambda i, j: (0, 0)),
            pl.BlockSpec((D_MODEL, th), lambda i, j: (0, j)),
            pl.BlockSpec((D_MODEL, th), lambda i, j: (0, j)),
            pl.BlockSpec((th, D_MODEL), lambda i, j: (j, 0)),
        ],
        out_specs=pl.BlockSpec((tm, D_MODEL), lambda i, j: (i, 0)),
        out_shape=jax.ShapeDtypeStruct((m, D_MODEL), F32),
        scratch_shapes=[pltpu.VMEM((tm, D_MODEL), BF16)],
        compiler_params=pltpu.CompilerParams(
            dimension_semantics=("parallel", "arbitrary"), vmem_limit_bytes=VMEM_LIMIT),
        name="ffn",
    )(h2d, norm_w, wg, wu, wd)


def kernel(x, meta_tokens, attn_norm_w, w_in, q_norm_w, k_norm_w, lambda_q1, lambda_k1, lambda_q2,
           lambda_k2, subln_w, conv_w, a_log, dt_bias, o_norm_w, w_out, ffn_norm_w, w_gate, w_up,
           w_down):
    b, seq, _ = x.shape
    m = b * seq
    x2d = x.reshape(m, D_MODEL)
    lead = jnp.concatenate([jnp.zeros((N_PAD, D_MODEL), x.dtype), meta_tokens.astype(x.dtype)], 0)

    w_main = w_in[0, :, :MAIN_COLS].astype(BF16)
    w_gates = jnp.pad(w_in[0, :, MAIN_COLS:], ((0, 0), (0, HEAD_W - GATE_COLS))).astype(BF16)
    tm = min(1024, m)
    proj, gates = _inproj(x2d, attn_norm_w, w_main, w_gates, tm, 1792)
    lead_proj, lead_gates = _inproj(lead, attn_norm_w, w_main, w_gates, LEAD, 1024)
    proj3 = proj.reshape(b, seq, MAIN_COLS)

    slopes = 2.0 ** (-8.0 * jnp.arange(1, HEADS + 1, dtype=F32) / HEADS)
    lvec = jnp.concatenate([lambda_q1, lambda_k1, lambda_q2, lambda_k2], 0).astype(F32)
    o_a = _attn(proj3, lead_proj, slopes, lvec, jnp.tile(q_norm_w, (1, 2)),
                jnp.tile(k_norm_w, (1, 2)), subln_w.reshape(HEAD_W, 1), 512, ATTN_NSUB)

    o_d = _gdn(
        proj3, lead_proj, gates.reshape(b, seq, HEAD_W), lead_gates, conv_w[0].T,
        a_log.astype(F32).reshape(HEADS, 1), dt_bias.astype(F32).reshape(HEADS, 1), o_norm_w)

    h1 = _outproj(x2d, o_a.reshape(m, GROUP_W), o_d.reshape(m, GROUP_W), w_out[0].astype(BF16),
                  min(512, m))
    out = _ffn(h1, ffn_norm_w, w_gate[0].astype(BF16), w_up[0].astype(BF16),
               w_down[0].astype(BF16), min(1024, m), 512)
    return out.reshape(b, seq, D_MODEL)
```

```python
import functools

import jax
import jax.numpy as jnp
import numpy as np
from jax import lax
from jax.experimental import pallas as pl
from jax.experimental.pallas import tpu as pltpu

F32 = jnp.float32
BF16 = jnp.bfloat16
HIGHEST = lax.Precision.HIGHEST

D_MODEL = 2048
N_META = 16
LEAD = 128
N_PAD = LEAD - N_META
HEADS = 8
HEAD_W = 128
QK_DIM = 64
GROUP_W = HEADS * HEAD_W
MAIN_COLS = 7 * GROUP_W
GATE_COLS = 2 * HEADS
CONV_K = 4
HIST = 16
FFN_HIDDEN = 5632
EPS = 1e-6
NEG = -1e30
LAMBDA_INIT = 0.2
CHUNK = 128
VMEM_LIMIT = 56 * 1024 * 1024
LOG2E = 1.4426950408889634
ATTN_NSUB = 4
VT_ROWS = HEAD_W + 16


def _bf16_pieces(x, n):
    out = []
    for _ in range(n):
        bits = np.array(x, np.float32).view(np.uint32)
        bits = (bits + 0x7FFF + ((bits >> 16) & 1)) & 0xFFFF0000
        p = float(bits.view(np.float32))
        out.append(p)
        x -= p
    return tuple(out)


LOG2E_BF16_PIECES = _bf16_pieces(LOG2E, 3)


def _dot(a, b, precision=None):
    return jnp.dot(a, b, preferred_element_type=F32, precision=precision)


def _dot_nt(a, b):
    return lax.dot_general(a, b, (((1,), (1,)), ((), ())), preferred_element_type=F32)


def _split(x):
    hi = x.astype(BF16)
    return hi, (x - hi.astype(F32)).astype(BF16)


def _dot3(a, b):
    a_hi, a_lo = a
    if isinstance(b, list):
        b_hi = jnp.concatenate([x[0] for x in b], axis=1)
        b_lo = jnp.concatenate([x[1] for x in b], axis=1)
    else:
        b_hi, b_lo = b
    return _dot(jnp.concatenate([a_hi, a_lo, a_hi], axis=1),
                jnp.concatenate([b_hi, b_hi, b_lo], axis=0))


def _rms_rows(x, w_row):
    return x * lax.rsqrt(jnp.mean(x * x, axis=-1, keepdims=True) + EPS) * w_row


def _inproj_kernel(x_ref, nw_ref, w_ref, wg_ref, o_ref, g_ref, u_ref, *, row_chunk):
    j = pl.program_id(1)

    @pl.when(j == 0)
    def _():
        def body(c, carry):
            r = pl.multiple_of(c * row_chunk, row_chunk)
            u = _rms_rows(x_ref[pl.ds(r, row_chunk), :], nw_ref[...]).astype(BF16)
            u_ref[pl.ds(r, row_chunk), :] = u
            g_ref[pl.ds(r, row_chunk), :] = _dot(u, wg_ref[...])
            return carry

        lax.fori_loop(0, x_ref.shape[0] // row_chunk, body, 0)

    o_ref[...] = _dot(u_ref[...], w_ref[...]).astype(o_ref.dtype)


def _inproj(x2d, norm_w, w_main, w_gate, tm, tn):
    m = x2d.shape[0]
    assert m % tm == 0 and MAIN_COLS % tn == 0
    row_chunk = min(256, tm)
    return pl.pallas_call(
        functools.partial(_inproj_kernel, row_chunk=row_chunk),
        grid=(m // tm, MAIN_COLS // tn),
        in_specs=[
            pl.BlockSpec((tm, D_MODEL), lambda i, j: (i, 0)),
            pl.BlockSpec((1, D_MODEL), lambda i, j: (0, 0)),
            pl.BlockSpec((D_MODEL, tn), lambda i, j: (0, j)),
            pl.BlockSpec((D_MODEL, HEAD_W), lambda i, j: (0, 0)),
        ],
        out_specs=[
            pl.BlockSpec((tm, tn), lambda i, j: (i, j)),
            pl.BlockSpec((tm, HEAD_W), lambda i, j: (i, 0)),
        ],
        out_shape=[
            jax.ShapeDtypeStruct((m, MAIN_COLS), BF16),
            jax.ShapeDtypeStruct((m, HEAD_W), F32),
        ],
        scratch_shapes=[pltpu.VMEM((tm, D_MODEL), BF16)],
        compiler_params=pltpu.CompilerParams(
            dimension_semantics=("parallel", "arbitrary"), vmem_limit_bytes=VMEM_LIMIT),
        name="inproj",
    )(x2d, norm_w, w_main, w_gate)


def _halfnorm(x, w_row):
    lo = lax.broadcasted_iota(jnp.int32, x.shape, 1) < QK_DIM
    x2 = x * x
    s_lo = jnp.sum(jnp.where(lo, x2, 0.0), axis=-1, keepdims=True)
    s_hi = jnp.sum(jnp.where(lo, 0.0, x2), axis=-1, keepdims=True)
    ms = jnp.where(lo, s_lo, s_hi) * (1.0 / QK_DIM)
    return x * lax.rsqrt(ms + EPS) * w_row


def _attn_kernel(slopes_ref, lvec_ref, q_ref, qnext_ref, k_ref, v_ref, lk_ref, lv_ref, qw_ref,
                 kw_ref, swc_ref, o_ref, kn_ref, vt_ref, lkn_ref, lvt_ref, kaug_ref, acc_ref,
                 sa_ref, sb_ref, wn_ref, wc_ref, kstage_ref, vstage_ref, *, tq, seq, nsub):
    h = pl.program_id(1)
    g = pl.program_id(2)
    step_rows = nsub * tq
    slope = slopes_ref[h]

    def query_operands(src_ref, dst_ref):
        lo = lax.broadcasted_iota(jnp.int32, (tq, HEAD_W), 1) < QK_DIM
        sub = lax.broadcasted_iota(jnp.int32, (HEAD_W, tq), 0)
        aug = jnp.zeros((HEAD_W, tq), F32)
        for n, piece in enumerate(LOG2E_BF16_PIECES):
            aug = jnp.where((sub == n) | (sub == n + 3), piece, aug)
        for blk in range(nsub):
            q_blk = src_ref[blk * tq:(blk + 1) * tq, :].astype(F32)
            qn = _halfnorm(q_blk, qw_ref[...]) * (QK_DIM ** -0.5 * LOG2E)
            for mp, x in enumerate((jnp.where(lo, qn, 0.0), jnp.where(lo, 0.0, qn))):
                dst_ref[blk, mp] = jnp.concatenate([x.T, aug], axis=0).astype(BF16)

    def key_operands(row0, kdst_ref, kdst0, vdst_ref, vdst0):
        al = lambda x: x if isinstance(x, int) else pl.multiple_of(x, 256)
        for c in range(step_rows // 256):
            src = pl.ds(al(row0 + c * 256), 256)
            dst = pl.ds(al(kdst0 + c * 256), 256)
            dstv = pl.ds(al(vdst0 + c * 256), 256)
            kdst_ref[dst, :] = _halfnorm(k_ref[src, :].astype(F32), kw_ref[...]).astype(BF16)
            vdst_ref[0:HEAD_W, dstv] = v_ref[src, :].astype(F32).T.astype(BF16)
            vdst_ref[HEAD_W:, dstv] = ones_row(256)

    def ones_row(n):
        return (lax.broadcasted_iota(jnp.int32, (VT_ROWS - HEAD_W, n), 0) == 0).astype(BF16)

    @pl.when(g == 0)
    def _():
        query_operands(q_ref, wn_ref)
        key_operands(0, kn_ref, 0, vt_ref, 0)
        lkn_ref[...] = _halfnorm(lk_ref[...].astype(F32), kw_ref[...]).astype(BF16)
        lvt_ref[0:HEAD_W, :] = lv_ref[...].astype(F32).T.astype(BF16)
        lvt_ref[HEAD_W:, :] = ones_row(LEAD)
        kk = lax.broadcasted_iota(jnp.int32, (tq, HEAD_W), 0)
        ln = lax.broadcasted_iota(jnp.int32, (tq, HEAD_W), 1)
        hi = ((kk // 16) * 16).astype(F32)
        lo_ = (kk % 16).astype(F32)
        kaug_ref[...] = (slope * jnp.where(ln < 3, hi, jnp.where(ln < 6, lo_, 0.0))).astype(BF16)

    @pl.when(g > 0)
    def _():
        r = pl.multiple_of(g * step_rows, step_rows)
        kn_ref[pl.ds(r, step_rows), :] = kstage_ref[...]
        vt_ref[:, pl.ds(r, step_rows)] = vstage_ref[...]

    wc_ref[...] = wn_ref[...]
    query_operands(qnext_ref, wn_ref)
    key_operands(jnp.minimum(g + 1, seq // step_rows - 1) * step_rows, kstage_ref, 0,
                 vstage_ref, 0)

    slope2 = slope * LOG2E
    q_off = slope2 * lax.broadcasted_iota(jnp.int32, (1, tq), 1).astype(F32)
    key_ok = lax.broadcasted_iota(jnp.int32, (LEAD, tq), 0) >= N_PAD
    pairs = [(blk, mp) for blk in range(nsub) for mp in range(2)]
    lead_s = [jnp.where(key_ok, _dot(lkn_ref[...], wc_ref[blk, mp, 0:HEAD_W, :]) + q_off, NEG)
              for blk, mp in pairs]
    lead_m = [jnp.max(s, axis=0, keepdims=True) for s in lead_s]
    lead_p = [jnp.exp2(s - m).astype(BF16) for s, m in zip(lead_s, lead_m)]
    for (blk, mp), p in zip(pairs, lead_p):
        acc_ref[blk, mp] = _dot(lvt_ref[...], p)
    carries = [(lead_m[2 * blk], lead_m[2 * blk + 1]) for blk in range(nsub)]

    key_i = lax.broadcasted_iota(jnp.int32, (tq, tq), 0)
    qry_i = lax.broadcasted_iota(jnp.int32, (tq, tq), 1)

    def scores(j, dst_ref, blk, diag):
        r = pl.multiple_of(j * tq, tq)
        lhs = jnp.concatenate([kn_ref[pl.ds(r, tq), :], kaug_ref[...]], axis=1)
        bms = []
        for mp in range(2):
            raw = _dot(lhs, wc_ref[blk, mp])
            if diag:
                raw = jnp.where(key_i <= qry_i, raw, NEG)
            dst_ref[mp] = raw
            bms.append(jnp.max(raw, axis=0, keepdims=True))
        return tuple(bms)

    def accumulate(j, src_ref, bms, carry, blk):
        r = pl.multiple_of(j * tq, tq)
        vt = vt_ref[:, pl.ds(r, tq)]
        c = slope2 * ((j - (nsub * g + blk)) * tq).astype(F32)
        out = []
        for mp in range(2):
            m_new = jnp.maximum(carry[mp], bms[mp] + c)
            alpha = jnp.exp2(carry[mp] - m_new)
            p = jnp.exp2(src_ref[mp] - (m_new - c)).astype(BF16)
            out.append(m_new)
            acc_ref[blk, mp] = alpha * acc_ref[blk, mp] + _dot(vt, p)
        return tuple(out)

    base = nsub * g
    own = [(blk, kb) for blk in range(nsub) for kb in range(blk + 1)]
    bufs = (sa_ref, sb_ref)
    bm_prev = scores(base + own[0][1], bufs[0], own[0][0], own[0][1] == own[0][0])
    for t in range(1, len(own)):
        (blk, kb), (pblk, pkb) = own[t], own[t - 1]
        bm = scores(base + kb, bufs[t % 2], blk, kb == blk)
        carries[pblk] = accumulate(base + pkb, bufs[(t - 1) % 2], bm_prev, carries[pblk], pblk)
        bm_prev = bm
    pend = (len(own) - 1) % 2

    def earlier(j, state):
        j_pend, bm_pend, cs = state[0], state[1], list(state[2])
        for blk in range(nsub):
            bm = scores(j, bufs[(pend + 1 + blk) % 2], blk, False)
            pblk, pj = (nsub - 1, j_pend) if blk == 0 else (blk - 1, j)
            cs[pblk] = accumulate(pj, bufs[(pend + blk) % 2], bm_pend, cs[pblk], pblk)
            bm_pend = bm
        return j, bm_pend, tuple(cs)

    j_pend, bm_pend, cs = lax.fori_loop(
        0, base, earlier, (base + nsub - 1, bm_prev, tuple(carries)))
    accumulate(j_pend, bufs[pend], bm_pend, cs[nsub - 1], nsub - 1)

    lv4 = lvec_ref[...]
    lam = (jnp.exp(jnp.sum(lv4[0:1] * lv4[1:2], axis=-1, keepdims=True))
           - jnp.exp(jnp.sum(lv4[2:3] * lv4[3:4], axis=-1, keepdims=True)) + LAMBDA_INIT)
    for blk in range(nsub):
        a0, a1 = acc_ref[blk, 0], acc_ref[blk, 1]
        o = (a0[0:HEAD_W] / a0[HEAD_W:HEAD_W + 1]
             - lam * (a1[0:HEAD_W] / a1[HEAD_W:HEAD_W + 1]))
        o = o * lax.rsqrt(jnp.mean(o * o, axis=0, keepdims=True) + EPS) * swc_ref[...]
        o_ref[blk * tq:(blk + 1) * tq, :] = (o * (1.0 - LAMBDA_INIT)).T.astype(o_ref.dtype)


def _attn(proj3, lead_proj, slopes, lvec, qw, kw, sw, tq, nsub):
    b, seq, _ = proj3.shape
    rows = nsub * tq
    assert seq % rows == 0 and rows % 256 == 0 and tq % 16 == 0 and tq <= 512 and nsub % 2 == 0
    nsteps = seq // rows
    return pl.pallas_call(
        functools.partial(_attn_kernel, tq=tq, seq=seq, nsub=nsub),
        grid=(b, HEADS, nsteps),
        in_specs=[
            pl.BlockSpec(memory_space=pltpu.SMEM),
            pl.BlockSpec((4, QK_DIM), lambda b_, h, i: (0, 0)),
            pl.BlockSpec((None, rows, HEAD_W), lambda b_, h, i: (b_, i, h)),
            pl.BlockSpec((None, rows, HEAD_W),
                         lambda b_, h, i: (b_, jnp.minimum(i + 1, nsteps - 1), h)),
            pl.BlockSpec((None, seq, HEAD_W), lambda b_, h, i: (b_, 0, HEADS + h)),
            pl.BlockSpec((None, seq, HEAD_W), lambda b_, h, i: (b_, 0, 2 * HEADS + h)),
            pl.BlockSpec((LEAD, HEAD_W), lambda b_, h, i: (0, HEADS + h)),
            pl.BlockSpec((LEAD, HEAD_W), lambda b_, h, i: (0, 2 * HEADS + h)),
            pl.BlockSpec((1, HEAD_W), lambda b_, h, i: (0, 0)),
            pl.BlockSpec((1, HEAD_W), lambda b_, h, i: (0, 0)),
            pl.BlockSpec((HEAD_W, 1), lambda b_, h, i: (0, 0)),
        ],
        out_specs=pl.BlockSpec((None, rows, HEAD_W), lambda b_, h, i: (b_, i, h)),
        out_shape=jax.ShapeDtypeStruct((b, seq, GROUP_W), BF16),
        scratch_shapes=[
            pltpu.VMEM((seq, HEAD_W), BF16),
            pltpu.VMEM((VT_ROWS, seq), BF16),
            pltpu.VMEM((LEAD, HEAD_W), BF16),
            pltpu.VMEM((VT_ROWS, LEAD), BF16),
            pltpu.VMEM((tq, HEAD_W), BF16),
            pltpu.VMEM((nsub, 2, VT_ROWS, tq), F32),
            pltpu.VMEM((2, tq, tq), F32),
            pltpu.VMEM((2, tq, tq), F32),
            pltpu.VMEM((nsub, 2, 2 * HEAD_W, tq), BF16),
            pltpu.VMEM((nsub, 2, 2 * HEAD_W, tq), BF16),
            pltpu.VMEM((rows, HEAD_W), BF16),
            pltpu.VMEM((VT_ROWS, rows), BF16),
        ],
        compiler_params=pltpu.CompilerParams(
            dimension_semantics=("parallel", "parallel", "arbitrary"),
            vmem_limit_bytes=VMEM_LIMIT),
        name="diff_attn",
    )(slopes, lvec, proj3, proj3, proj3, proj3, lead_proj, lead_proj, qw, kw, sw)


def _gdn_kernel(tq_ref, tk_ref, tv_ref, hq_ref, hk_ref, hv_ref, lq_ref, lk_ref, lv_ref,
                tg_ref, lg_ref, cwq_ref, cwk_ref, cwv_ref, alog_ref, dtb_ref, z_ref, nw_ref,
                o_ref, xs_ref, gs_ref, s_ref):
    s = pl.program_id(1)
    is_lead = s == 0
    width = GROUP_W
    srcs = ((tq_ref, hq_ref, lq_ref), (tk_ref, hk_ref, lk_ref), (tv_ref, hv_ref, lv_ref))

    @pl.when(is_lead)
    def _():
        rowid = lax.broadcasted_iota(jnp.int32, (CHUNK, width), 0)
        for idx, (_, _, l_ref) in enumerate(srcs):
            cs = slice(idx * width, (idx + 1) * width)
            xs_ref[0:HIST, cs] = jnp.zeros((HIST, width), BF16)
            xs_ref[HIST:, cs] = jnp.where(rowid >= N_PAD, l_ref[...], jnp.zeros((), BF16))
        gs_ref[...] = lg_ref[...]
        s_ref[...] = jnp.zeros_like(s_ref)

    @pl.when(s == 1)
    def _():
        for idx, (t_ref, _, l_ref) in enumerate(srcs):
            cs = slice(idx * width, (idx + 1) * width)
            xs_ref[0:HIST, cs] = l_ref[LEAD - HIST:LEAD, :]
            xs_ref[HIST:, cs] = t_ref[...]
        gs_ref[...] = tg_ref[...]

    @pl.when(s > 1)
    def _():
        for idx, (t_ref, h_ref, _) in enumerate(srcs):
            cs = slice(idx * width, (idx + 1) * width)
            xs_ref[0:HIST, cs] = h_ref[...]
            xs_ref[HIST:, cs] = t_ref[...]
        gs_ref[...] = tg_ref[...]

    sel_r = lax.broadcasted_iota(jnp.int32, ((CONV_K - 1) * CHUNK, HIST + CHUNK), 0)
    sel_c = lax.broadcasted_iota(jnp.int32, ((CONV_K - 1) * CHUNK, HIST + CHUNK), 1)
    tap_of = sel_r // CHUNK
    shifted = _dot((sel_c == HIST + sel_r % CHUNK - (CONV_K - 1 - tap_of)).astype(BF16),
                   xs_ref[...])

    rowi = lax.broadcasted_iota(jnp.int32, (CHUNK, CHUNK), 0)
    lanei = lax.broadcasted_iota(jnp.int32, (CHUNK, CHUNK), 1)
    incl = rowi >= lanei
    eye = (rowi == lanei).astype(F32)

    g_t = gs_ref[...].T[0:2 * HEADS]
    vmask = (lax.broadcasted_iota(jnp.int32, (HEADS, CHUNK), 1)
             >= jnp.where(is_lead, N_PAD, 0)).astype(F32)
    beta_t = jax.nn.sigmoid(g_t[0:HEADS]) * vmask
    t = g_t[HEADS:] + dtb_ref[...]
    softplus = jnp.maximum(t, 0.0) + jnp.log(1.0 + jnp.exp(-jnp.abs(t)))
    decay_t = -jnp.exp(alog_ref[...]) * softplus * vmask
    gc_t = _dot(decay_t, (rowi <= lanei).astype(F32), HIGHEST)
    cols = jnp.concatenate(
        [beta_t, gc_t, jnp.zeros((CHUNK - 2 * HEADS, CHUNK), F32)], axis=0).T

    heads = range(HEADS)
    hcols = [slice(h * HEAD_W, (h + 1) * HEAD_W) for h in heads]
    mks, pks, rhs, qgs, qkds, decs = [], [], [], [], [], []
    for hh in heads:
        hs = hcols[hh]
        beta = cols[:, hh:hh + 1]
        gc = cols[:, HEADS + hh:HEADS + hh + 1]
        gc_row = gc_t[hh:hh + 1]
        g_last = gc_row[:, CHUNK - 1:CHUNK]

        def conv_silu(idx, cw_ref):
            c0 = idx * width + hh * HEAD_W
            y = xs_ref[HIST:, c0:c0 + HEAD_W].astype(F32) * cw_ref[CONV_K - 1:CONV_K, hs]
            for j in range(CONV_K - 1):
                y = y + shifted[j * CHUNK:(j + 1) * CHUNK, c0:c0 + HEAD_W] * cw_ref[j:j + 1, hs]
            return y * jax.nn.sigmoid(y)

        q = conv_silu(0, cwq_ref)
        k = conv_silu(1, cwk_ref)
        v = conv_silu(2, cwv_ref)
        q = q * lax.rsqrt(jnp.sum(q * q, axis=-1, keepdims=True) + EPS) * (HEAD_W ** -0.5)
        k = k * lax.rsqrt(jnp.sum(k * k, axis=-1, keepdims=True) + EPS)

        decay = jnp.where(incl, jnp.exp(jnp.where(incl, gc - gc_row, 0.0)), 0.0)
        kb = k * beta
        k_t = k.T
        kt16 = k_t.astype(BF16)
        lmat = jnp.where(rowi > lanei, _dot(kb.astype(BF16), kt16) * decay, 0.0)
        qkds.append(jnp.concatenate(
            [(_dot(q.astype(BF16), kt16) * decay).astype(BF16),
             (k_t * jnp.exp(g_last - gc_row)).astype(BF16)], axis=0))
        qgs.append((q * jnp.exp(gc)).astype(BF16))
        decs.append(jnp.exp(g_last))
        mks.append(-lmat)
        pks.append(eye - lmat)
        rhs.append(jnp.concatenate([v * beta, kb * jnp.exp(gc)], axis=1))

    ms = [_split(m) for m in mks]
    mks = [_dot3(m, m) for m in ms]
    for _ in range(5):
        ms = [_split(m) for m in mks]
        rs = [_dot3(m, [_split(p), m]) for m, p in zip(ms, pks)]
        pks = [p + r[:, :CHUNK] for p, r in zip(pks, rs)]
        mks = [r[:, CHUNK:] for r in rs]
    pks = [p + _dot3(_split(m), _split(p)) for m, p in zip(mks, pks)]
    uws = [_dot3(_split(p), _split(r)) for p, r in zip(pks, rhs)]

    states = [s_ref[h] for h in heads]
    ws_qs = [_dot(jnp.concatenate([uws[h][:, HEAD_W:].astype(BF16), qgs[h]], axis=0),
                  states[h].astype(BF16)) for h in heads]
    v16 = [(uws[h][:, :HEAD_W] - ws_qs[h][:CHUNK]).astype(BF16) for h in heads]
    qv_kv = [_dot(qkds[h], v16[h]) for h in heads]
    for h in heads:
        s_ref[h] = decs[h] * states[h] + qv_kv[h][CHUNK:]
        o = ws_qs[h][CHUNK:] + qv_kv[h][:CHUNK]
        z = z_ref[:, hcols[h]].astype(F32)
        o_ref[:, hcols[h]] = (
            _rms_rows(o, nw_ref[...]) * (z * jax.nn.sigmoid(z))).astype(o_ref.dtype)


def _gdn(proj3, lead_proj, gates3, lead_gates, conv_wt, alog_col, dtb_col, onw):
    b, seq, _ = proj3.shape
    nb = 1 + seq // CHUNK
    tok = lambda g: pl.BlockSpec(
        (None, CHUNK, GROUP_W), lambda b_, s: (b_, jnp.maximum(s - 1, 0), g))
    halo = lambda g: pl.BlockSpec(
        (None, 16, GROUP_W),
        lambda b_, s: (b_, jnp.maximum((s - 1) * (CHUNK // 16) - 1, 0), g))
    lead = lambda g: pl.BlockSpec((LEAD, GROUP_W), lambda b_, s: (0, g))
    cw = lambda g: pl.BlockSpec((CONV_K, GROUP_W), lambda b_, s: (0, g))
    col = pl.BlockSpec((HEADS, 1), lambda b_, s: (0, 0))
    return pl.pallas_call(
        _gdn_kernel,
        grid=(b, nb),
        in_specs=[
            tok(3), tok(4), tok(5), halo(3), halo(4), halo(5), lead(3), lead(4), lead(5),
            pl.BlockSpec((None, CHUNK, HEAD_W), lambda b_, s: (b_, jnp.maximum(s - 1, 0), 0)),
            pl.BlockSpec((LEAD, HEAD_W), lambda b_, s: (0, 0)),
            cw(0), cw(1), cw(2), col, col, tok(6),
            pl.BlockSpec((1, HEAD_W), lambda b_, s: (0, 0)),
        ],
        out_specs=pl.BlockSpec((None, CHUNK, GROUP_W), lambda b_, s: (b_, jnp.maximum(s - 1, 0), 0)),
        out_shape=jax.ShapeDtypeStruct((b, seq, GROUP_W), BF16),
        scratch_shapes=[pltpu.VMEM((HIST + CHUNK, 3 * GROUP_W), BF16),
                        pltpu.VMEM((CHUNK, HEAD_W), F32),
                        pltpu.VMEM((HEADS, HEAD_W, HEAD_W), F32)],
        compiler_params=pltpu.CompilerParams(
            dimension_semantics=("parallel", "arbitrary"), vmem_limit_bytes=VMEM_LIMIT),
        name="gdn",
    )(proj3, proj3, proj3, proj3, proj3, proj3, lead_proj, lead_proj, lead_proj,
      gates3, lead_gates, conv_wt, conv_wt, conv_wt, alog_col, dtb_col, proj3, onw)


def _outproj_kernel(x_ref, oa_ref, od_ref, wa_ref, wd_ref, o_ref):
    o_ref[...] = x_ref[...] + _dot(oa_ref[...], wa_ref[...]) + _dot(od_ref[...], wd_ref[...])


def _outproj(x2d, oa, od, w_out16, tm):
    m = x2d.shape[0]
    assert m % tm == 0
    return pl.pallas_call(
        _outproj_kernel,
        grid=(m // tm,),
        in_specs=[
            pl.BlockSpec((tm, D_MODEL), lambda i: (i, 0)),
            pl.BlockSpec((tm, GROUP_W), lambda i: (i, 0)),
            pl.BlockSpec((tm, GROUP_W), lambda i: (i, 0)),
            pl.BlockSpec((GROUP_W, D_MODEL), lambda i: (0, 0)),
            pl.BlockSpec((GROUP_W, D_MODEL), lambda i: (1, 0)),
        ],
        out_specs=pl.BlockSpec((tm, D_MODEL), lambda i: (i, 0)),
        out_shape=jax.ShapeDtypeStruct((m, D_MODEL), F32),
        compiler_params=pltpu.CompilerParams(
            dimension_semantics=("parallel",), vmem_limit_bytes=VMEM_LIMIT),
        name="outproj",
    )(x2d, oa, od, w_out16, w_out16)


def _ffn_kernel(h_ref, nw_ref, wg_ref, wu_ref, wd_ref, o_ref, u_ref, *, row_chunk):
    j = pl.program_id(1)

    @pl.when(j == 0)
    def _():
        def body(c, carry):
            r = pl.multiple_of(c * row_chunk, row_chunk)
            x = h_ref[pl.ds(r, row_chunk), :]
            u_ref[pl.ds(r, row_chunk), :] = _rms_rows(x, nw_ref[...]).astype(BF16)
            o_ref[pl.ds(r, row_chunk), :] = x
            return carry

        lax.fori_loop(0, h_ref.shape[0] // row_chunk, body, 0)

    u = u_ref[...]
    g = _dot(u, wg_ref[...])
    a = (g * jax.nn.sigmoid(g) * _dot(u, wu_ref[...])).astype(BF16)
    o_ref[...] += _dot(a, wd_ref[...])


def _ffn(h2d, norm_w, wg, wu, wd, tm, th):
    m = h2d.shape[0]
    assert m % tm == 0 and FFN_HIDDEN % th == 0
    return pl.pallas_call(
        functools.partial(_ffn_kernel, row_chunk=min(256, tm)),
        grid=(m // tm, FFN_HIDDEN // th),
        in_specs=[
            pl.BlockSpec((tm, D_MODEL), lambda i, j: (i, 0)),
            pl.BlockSpec((1, D_MODEL), lambda i, j: (0, 0)),
            pl.BlockSpec((D_MODEL, th), lambda i, j: (0, j)),
            pl.BlockSpec((D_MODEL, th), lambda i, j: (0, j)),
            pl.BlockSpec((th, D_MODEL), lambda i, j: (j, 0)),
        ],
        out_specs=pl.BlockSpec((tm, D_MODEL), lambda i, j: (i, 0)),
        out_shape=jax.ShapeDtypeStruct((m, D_MODEL), F32),
        scratch_shapes=[pltpu.VMEM((tm, D_MODEL), BF16)],
        compiler_params=pltpu.CompilerParams(
            dimension_semantics=("parallel", "arbitrary"), vmem_limit_bytes=VMEM_LIMIT),
        name="ffn",
    )(h2d, norm_w, wg, wu, wd)


def kernel(x, meta_tokens, attn_norm_w, w_in, q_norm_w, k_norm_w, lambda_q1, lambda_k1, lambda_q2,
           lambda_k2, subln_w, conv_w, a_log, dt_bias, o_norm_w, w_out, ffn_norm_w, w_gate, w_up,
           w_down):
    b, seq, _ = x.shape
    m = b * seq
    x2d = x.reshape(m, D_MODEL)
    lead = jnp.concatenate([jnp.zeros((N_PAD, D_MODEL), x.dtype), meta_tokens.astype(x.dtype)], 0)

    w_main = w_in[0, :, :MAIN_COLS].astype(BF16)
    w_gates = jnp.pad(w_in[0, :, MAIN_COLS:], ((0, 0), (0, HEAD_W - GATE_COLS))).astype(BF16)
    tm = min(1024, m)
    proj, gates = _inproj(x2d, attn_norm_w, w_main, w_gates, tm, 1792)
    lead_proj, lead_gates = _inproj(lead, attn_norm_w, w_main, w_gates, LEAD, 1024)
    proj3 = proj.reshape(b, seq, MAIN_COLS)

    slopes = 2.0 ** (-8.0 * jnp.arange(1, HEADS + 1, dtype=F32) / HEADS)
    lvec = jnp.concatenate([lambda_q1, lambda_k1, lambda_q2, lambda_k2], 0).astype(F32)
    o_a = _attn(proj3, lead_proj, slopes, lvec, jnp.tile(q_norm_w, (1, 2)),
                jnp.tile(k_norm_w, (1, 2)), subln_w.reshape(HEAD_W, 1), 512, ATTN_NSUB)

    o_d = _gdn(
        proj3, lead_proj, gates.reshape(b, seq, HEAD_W), lead_gates, conv_w[0].T,
        a_log.astype(F32).reshape(HEADS, 1), dt_bias.astype(F32).reshape(HEADS, 1), o_norm_w)

    h1 = _outproj(x2d, o_a.reshape(m, GROUP_W), o_d.reshape(m, GROUP_W), w_out[0].astype(BF16),
                  min(512, m))
    out = _ffn(h1, ffn_norm_w, w_gate[0].astype(BF16), w_up[0].astype(BF16),
               w_down[0].astype(BF16), min(1024, m), 512)
    return out.reshape(b, seq, D_MODEL)
```

```python
import functools

import jax
import jax.numpy as jnp
import numpy as np
from jax import lax
from jax.experimental import pallas as pl
from jax.experimental.pallas import tpu as pltpu

F32 = jnp.float32
BF16 = jnp.bfloat16
HIGHEST = lax.Precision.HIGHEST

D_MODEL = 2048
N_META = 16
LEAD = 128
N_PAD = LEAD - N_META
HEADS = 8
HEAD_W = 128
QK_DIM = 64
GROUP_W = HEADS * HEAD_W
MAIN_COLS = 7 * GROUP_W
GATE_COLS = 2 * HEADS
CONV_K = 4
HIST = 16
FFN_HIDDEN = 5632
EPS = 1e-6
NEG = -1e30
LAMBDA_INIT = 0.2
CHUNK = 128
VMEM_LIMIT = 56 * 1024 * 1024
LOG2E = 1.4426950408889634
ATTN_NSUB = 4
VT_ROWS = HEAD_W + 16


def _bf16_pieces(x, n):
    out = []
    for _ in range(n):
        bits = np.array(x, np.float32).view(np.uint32)
        bits = (bits + 0x7FFF + ((bits >> 16) & 1)) & 0xFFFF0000
        p = float(bits.view(np.float32))
        out.append(p)
        x -= p
    return tuple(out)


LOG2E_BF16_PIECES = _bf16_pieces(LOG2E, 3)


def _dot(a, b, precision=None):
    return jnp.dot(a, b, preferred_element_type=F32, precision=precision)


def _dot_nt(a, b):
    return lax.dot_general(a, b, (((1,), (1,)), ((), ())), preferred_element_type=F32)


def _split(x):
    hi = x.astype(BF16)
    return hi, (x - hi.astype(F32)).astype(BF16)


def _dot3(a, b):
    a_hi, a_lo = a
    if isinstance(b, list):
        b_hi = jnp.concatenate([x[0] for x in b], axis=1)
        b_lo = jnp.concatenate([x[1] for x in b], axis=1)
    else:
        b_hi, b_lo = b
    return _dot(jnp.concatenate([a_hi, a_lo, a_hi], axis=1),
                jnp.concatenate([b_hi, b_hi, b_lo], axis=0))


def _rms_rows(x, w_row):
    return x * lax.rsqrt(jnp.mean(x * x, axis=-1, keepdims=True) + EPS) * w_row


def _inproj_kernel(x_ref, nw_ref, w_ref, wg_ref, o_ref, g_ref, u_ref, *, row_chunk):
    j = pl.program_id(1)

    @pl.when(j == 0)
    def _():
        def body(c, carry):
            r = pl.multiple_of(c * row_chunk, row_chunk)
            u = _rms_rows(x_ref[pl.ds(r, row_chunk), :], nw_ref[...]).astype(BF16)
            u_ref[pl.ds(r, row_chunk), :] = u
            g_ref[pl.ds(r, row_chunk), :] = _dot(u, wg_ref[...])
            return carry

        lax.fori_loop(0, x_ref.shape[0] // row_chunk, body, 0)

    o_ref[...] = _dot(u_ref[...], w_ref[...]).astype(o_ref.dtype)


def _inproj(x2d, norm_w, w_main, w_gate, tm, tn):
    m = x2d.shape[0]
    assert m % tm == 0 and MAIN_COLS % tn == 0
    row_chunk = min(256, tm)
    return pl.pallas_call(
        functools.partial(_inproj_kernel, row_chunk=row_chunk),
        grid=(m // tm, MAIN_COLS // tn),
        in_specs=[
            pl.BlockSpec((tm, D_MODEL), lambda i, j: (i, 0)),
            pl.BlockSpec((1, D_MODEL), lambda i, j: (0, 0)),
            pl.BlockSpec((D_MODEL, tn), lambda i, j: (0, j)),
            pl.BlockSpec((D_MODEL, HEAD_W), lambda i, j: (0, 0)),
        ],
        out_specs=[
            pl.BlockSpec((tm, tn), lambda i, j: (i, j)),
            pl.BlockSpec((tm, HEAD_W), lambda i, j: (i, 0)),
        ],
        out_shape=[
            jax.ShapeDtypeStruct((m, MAIN_COLS), BF16),
            jax.ShapeDtypeStruct((m, HEAD_W), F32),
        ],
        scratch_shapes=[pltpu.VMEM((tm, D_MODEL), BF16)],
        compiler_params=pltpu.CompilerParams(
            dimension_semantics=("parallel", "arbitrary"), vmem_limit_bytes=VMEM_LIMIT),
        name="inproj",
    )(x2d, norm_w, w_main, w_gate)


def _halfnorm(x, w_row):
    lo = lax.broadcasted_iota(jnp.int32, x.shape, 1) < QK_DIM
    x2 = x * x
    s_lo = jnp.sum(jnp.where(lo, x2, 0.0), axis=-1, keepdims=True)
    s_hi = jnp.sum(jnp.where(lo, 0.0, x2), axis=-1, keepdims=True)
    ms = jnp.where(lo, s_lo, s_hi) * (1.0 / QK_DIM)
    return x * lax.rsqrt(ms + EPS) * w_row


def _attn_kernel(slopes_ref, lvec_ref, q_ref, qnext_ref, k_ref, v_ref, lk_ref, lv_ref, qw_ref,
                 kw_ref, swc_ref, o_ref, kn_ref, vt_ref, lkn_ref, lvt_ref, kaug_ref, acc_ref,
                 sa_ref, sb_ref, wn_ref, wc_ref, kstage_ref, vstage_ref, *, tq, seq, nsub):
    h = pl.program_id(1)
    g = pl.program_id(2)
    step_rows = nsub * tq
    slope = slopes_ref[h]

    def query_operands(src_ref, dst_ref):
        lo = lax.broadcasted_iota(jnp.int32, (tq, HEAD_W), 1) < QK_DIM
        sub = lax.broadcasted_iota(jnp.int32, (HEAD_W, tq), 0)
        aug = jnp.zeros((HEAD_W, tq), F32)
        for n, piece in enumerate(LOG2E_BF16_PIECES):
            aug = jnp.where((sub == n) | (sub == n + 3), piece, aug)
        for blk in range(nsub):
            q_blk = src_ref[blk * tq:(blk + 1) * tq, :].astype(F32)
            qn = _halfnorm(q_blk, qw_ref[...]) * (QK_DIM ** -0.5 * LOG2E)
            for mp, x in enumerate((jnp.where(lo, qn, 0.0), jnp.where(lo, 0.0, qn))):
                dst_ref[blk, mp] = jnp.concatenate([x.T, aug], axis=0).astype(BF16)

    def key_operands(row0, kdst_ref, kdst0, vdst_ref, vdst0):
        al = lambda x: x if isinstance(x, int) else pl.multiple_of(x, 256)
        for c in range(step_rows // 256):
            src = pl.ds(al(row0 + c * 256), 256)
            dst = pl.ds(al(kdst0 + c * 256), 256)
            dstv = pl.ds(al(vdst0 + c * 256), 256)
            kdst_ref[dst, :] = _halfnorm(k_ref[src, :].astype(F32), kw_ref[...]).astype(BF16)
            vdst_ref[0:HEAD_W, dstv] = v_ref[src, :].astype(F32).T.astype(BF16)
            vdst_ref[HEAD_W:, dstv] = ones_row(256)

    def ones_row(n):
        return (lax.broadcasted_iota(jnp.int32, (VT_ROWS - HEAD_W, n), 0) == 0).astype(BF16)

    @pl.when(g == 0)
    def _():
        query_operands(q_ref, wn_ref)
        key_operands(0, kn_ref, 0, vt_ref, 0)
        lkn_ref[...] = _halfnorm(lk_ref[...].astype(F32), kw_ref[...]).astype(BF16)
        lvt_ref[0:HEAD_W, :] = lv_ref[...].astype(F32).T.astype(BF16)
        lvt_ref[HEAD_W:, :] = ones_row(LEAD)
        kk = lax.broadcasted_iota(jnp.int32, (tq, HEAD_W), 0)
        ln = lax.broadcasted_iota(jnp.int32, (tq, HEAD_W), 1)
        hi = ((kk // 16) * 16).astype(F32)
        lo_ = (kk % 16).astype(F32)
        kaug_ref[...] = (slope * jnp.where(ln < 3, hi, jnp.where(ln < 6, lo_, 0.0))).astype(BF16)

    @pl.when(g > 0)
    def _():
        r = pl.multiple_of(g * step_rows, step_rows)
        kn_ref[pl.ds(r, step_rows), :] = kstage_ref[...]
        vt_ref[:, pl.ds(r, step_rows)] = vstage_ref[...]

    wc_ref[...] = wn_ref[...]
    query_operands(qnext_ref, wn_ref)
    key_operands(jnp.minimum(g + 1, seq // step_rows - 1) * step_rows, kstage_ref, 0,
                 vstage_ref, 0)

    slope2 = slope * LOG2E
    q_off = slope2 * lax.broadcasted_iota(jnp.int32, (1, tq), 1).astype(F32)
    key_ok = lax.broadcasted_iota(jnp.int32, (LEAD, tq), 0) >= N_PAD
    pairs = [(blk, mp) for blk in range(nsub) for mp in range(2)]
    lead_s = [jnp.where(key_ok, _dot(lkn_ref[...], wc_ref[blk, mp, 0:HEAD_W, :]) + q_off, NEG)
              for blk, mp in pairs]
    lead_m = [jnp.max(s, axis=0, keepdims=True) for s in lead_s]
    lead_p = [jnp.exp2(s - m).astype(BF16) for s, m in zip(lead_s, lead_m)]
    for (blk, mp), p in zip(pairs, lead_p):
        acc_ref[blk, mp] = _dot(lvt_ref[...], p)
    carries = [(lead_m[2 * blk], lead_m[2 * blk + 1]) for blk in range(nsub)]

    key_i = lax.broadcasted_iota(jnp.int32, (tq, tq), 0)
    qry_i = lax.broadcasted_iota(jnp.int32, (tq, tq), 1)

    def scores(j, dst_ref, blk, diag):
        r = pl.multiple_of(j * tq, tq)
        lhs = jnp.concatenate([kn_ref[pl.ds(r, tq), :], kaug_ref[...]], axis=1)
        bms = []
        for mp in range(2):
            raw = _dot(lhs, wc_ref[blk, mp])
            if diag:
                raw = jnp.where(key_i <= qry_i, raw, NEG)
            dst_ref[mp] = raw
            bms.append(jnp.max(raw, axis=0, keepdims=True))
        return tuple(bms)

    def accumulate(j, src_ref, bms, carry, blk):
        r = pl.multiple_of(j * tq, tq)
        vt = vt_ref[:, pl.ds(r, tq)]
        c = slope2 * ((j - (nsub * g + blk)) * tq).astype(F32)
        out = []
        for mp in range(2):
            m_new = jnp.maximum(carry[mp], bms[mp] + c)
            alpha = jnp.exp2(carry[mp] - m_new)
            p = jnp.exp2(src_ref[mp] - (m_new - c)).astype(BF16)
            out.append(m_new)
            acc_ref[blk, mp] = alpha * acc_ref[blk, mp] + _dot(vt, p)
        return tuple(out)

    base = nsub * g
    own = [(blk, kb) for blk in range(nsub) for kb in range(blk + 1)]
    bufs = (sa_ref, sb_ref)
    bm_prev = scores(base + own[0][1], bufs[0], own[0][0], own[0][1] == own[0][0])
    for t in range(1, len(own)):
        (blk, kb), (pblk, pkb) = own[t], own[t - 1]
        bm = scores(base + kb, bufs[t % 2], blk, kb == blk)
        carries[pblk] = accumulate(base + pkb, bufs[(t - 1) % 2], bm_prev, carries[pblk], pblk)
        bm_prev = bm
    pend = (len(own) - 1) % 2

    def earlier(j, state):
        j_pend, bm_pend, cs = state[0], state[1], list(state[2])
        for blk in range(nsub):
            bm = scores(j, bufs[(pend + 1 + blk) % 2], blk, False)
            pblk, pj = (nsub - 1, j_pend) if blk == 0 else (blk - 1, j)
            cs[pblk] = accumulate(pj, bufs[(pend + blk) % 2], bm_pend, cs[pblk], pblk)
            bm_pend = bm
        return j, bm_pend, tuple(cs)

    j_pend, bm_pend, cs = lax.fori_loop(
        0, base, earlier, (base + nsub - 1, bm_prev, tuple(carries)))
    accumulate(j_pend, bufs[pend], bm_pend, cs[nsub - 1], nsub - 1)

    lv4 = lvec_ref[...]
    lam = (jnp.exp(jnp.sum(lv4[0:1] * lv4[1:2], axis=-1, keepdims=True))
           - jnp.exp(jnp.sum(lv4[2:3] * lv4[3:4], axis=-1, keepdims=True)) + LAMBDA_INIT)
    for blk in range(nsub):
        a0, a1 = acc_ref[blk, 0], acc_ref[blk, 1]
        o = (a0[0:HEAD_W] / a0[HEAD_W:HEAD_W + 1]
             - lam * (a1[0:HEAD_W] / a1[HEAD_W:HEAD_W + 1]))
        o = o * lax.rsqrt(jnp.mean(o * o, axis=0, keepdims=True) + EPS) * swc_ref[...]
        o_ref[blk * tq:(blk + 1) * tq, :] = (o * (1.0 - LAMBDA_INIT)).T.astype(o_ref.dtype)


def _attn(proj3, lead_proj, slopes, lvec, qw, kw, sw, tq, nsub):
    b, seq, _ = proj3.shape
    rows = nsub * tq
    assert seq % rows == 0 and rows % 256 == 0 and tq % 16 == 0 and tq <= 512 and nsub % 2 == 0
    nsteps = seq // rows
    return pl.pallas_call(
        functools.partial(_attn_kernel, tq=tq, seq=seq, nsub=nsub),
        grid=(b, HEADS, nsteps),
        in_specs=[
            pl.BlockSpec(memory_space=pltpu.SMEM),
            pl.BlockSpec((4, QK_DIM), lambda b_, h, i: (0, 0)),
            pl.BlockSpec((None, rows, HEAD_W), lambda b_, h, i: (b_, i, h)),
            pl.BlockSpec((None, rows, HEAD_W),
                         lambda b_, h, i: (b_, jnp.minimum(i + 1, nsteps - 1), h)),
            pl.BlockSpec((None, seq, HEAD_W), lambda b_, h, i: (b_, 0, HEADS + h)),
            pl.BlockSpec((None, seq, HEAD_W), lambda b_, h, i: (b_, 0, 2 * HEADS + h)),
            pl.BlockSpec((LEAD, HEAD_W), lambda b_, h, i: (0, HEADS + h)),
            pl.BlockSpec((LEAD, HEAD_W), lambda b_, h, i: (0, 2 * HEADS + h)),
            pl.BlockSpec((1, HEAD_W), lambda b_, h, i: (0, 0)),
            pl.BlockSpec((1, HEAD_W), lambda b_, h, i: (0, 0)),
            pl.BlockSpec((HEAD_W, 1), lambda b_, h, i: (0, 0)),
        ],
        out_specs=pl.BlockSpec((None, rows, HEAD_W), lambda b_, h, i: (b_, i, h)),
        out_shape=jax.ShapeDtypeStruct((b, seq, GROUP_W), BF16),
        scratch_shapes=[
            pltpu.VMEM((seq, HEAD_W), BF16),
            pltpu.VMEM((VT_ROWS, seq), BF16),
            pltpu.VMEM((LEAD, HEAD_W), BF16),
            pltpu.VMEM((VT_ROWS, LEAD), BF16),
            pltpu.VMEM((tq, HEAD_W), BF16),
            pltpu.VMEM((nsub, 2, VT_ROWS, tq), F32),
            pltpu.VMEM((2, tq, tq), F32),
            pltpu.VMEM((2, tq, tq), F32),
            pltpu.VMEM((nsub, 2, 2 * HEAD_W, tq), BF16),
            pltpu.VMEM((nsub, 2, 2 * HEAD_W, tq), BF16),
            pltpu.VMEM((rows, HEAD_W), BF16),
            pltpu.VMEM((VT_ROWS, rows), BF16),
        ],
        compiler_params=pltpu.CompilerParams(
            dimension_semantics=("parallel", "parallel", "arbitrary"),
            vmem_limit_bytes=VMEM_LIMIT),
        name="diff_attn",
    )(slopes, lvec, proj3, proj3, proj3, proj3, lead_proj, lead_proj, qw, kw, sw)


def _gdn_kernel(tq_ref, tk_ref, tv_ref, hq_ref, hk_ref, hv_ref, lq_ref, lk_ref, lv_ref,
                tg_ref, lg_ref, cwq_ref, cwk_ref, cwv_ref, alog_ref, dtb_ref, z_ref, nw_ref,
                o_ref, xs_ref, gs_ref, s_ref):
    s = pl.program_id(1)
    is_lead = s == 0
    width = GROUP_W
    srcs = ((tq_ref, hq_ref, lq_ref), (tk_ref, hk_ref, lk_ref), (tv_ref, hv_ref, lv_ref))

    @pl.when(is_lead)
    def _():
        rowid = lax.broadcasted_iota(jnp.int32, (CHUNK, width), 0)
        for idx, (_, _, l_ref) in enumerate(srcs):
            cs = slice(idx * width, (idx + 1) * width)
            xs_ref[0:HIST, cs] = jnp.zeros((HIST, width), BF16)
            xs_ref[HIST:, cs] = jnp.where(rowid >= N_PAD, l_ref[...], jnp.zeros((), BF16))
        gs_ref[...] = lg_ref[...]
        s_ref[...] = jnp.zeros_like(s_ref)

    @pl.when(s == 1)
    def _():
        for idx, (t_ref, _, l_ref) in enumerate(srcs):
            cs = slice(idx * width, (idx + 1) * width)
            xs_ref[0:HIST, cs] = l_ref[LEAD - HIST:LEAD, :]
            xs_ref[HIST:, cs] = t_ref[...]
        gs_ref[...] = tg_ref[...]

    @pl.when(s > 1)
    def _():
        for idx, (t_ref, h_ref, _) in enumerate(srcs):
            cs = slice(idx * width, (idx + 1) * width)
            xs_ref[0:HIST, cs] = h_ref[...]
            xs_ref[HIST:, cs] = t_ref[...]
        gs_ref[...] = tg_ref[...]

    sel_r = lax.broadcasted_iota(jnp.int32, ((CONV_K - 1) * CHUNK, HIST + CHUNK), 0)
    sel_c = lax.broadcasted_iota(jnp.int32, ((CONV_K - 1) * CHUNK, HIST + CHUNK), 1)
    tap_of = sel_r // CHUNK
    shifted = _dot((sel_c == HIST + sel_r % CHUNK - (CONV_K - 1 - tap_of)).astype(BF16),
                   xs_ref[...])

    rowi = lax.broadcasted_iota(jnp.int32, (CHUNK, CHUNK), 0)
    lanei = lax.broadcasted_iota(jnp.int32, (CHUNK, CHUNK), 1)
    incl = rowi >= lanei
    eye = (rowi == lanei).astype(F32)

    g_t = gs_ref[...].T[0:2 * HEADS]
    vmask = (lax.broadcasted_iota(jnp.int32, (HEADS, CHUNK), 1)
             >= jnp.where(is_lead, N_PAD, 0)).astype(F32)
    beta_t = jax.nn.sigmoid(g_t[0:HEADS]) * vmask
    t = g_t[HEADS:] + dtb_ref[...]
    softplus = jnp.maximum(t, 0.0) + jnp.log(1.0 + jnp.exp(-jnp.abs(t)))
    decay_t = -jnp.exp(alog_ref[...]) * softplus * vmask
    gc_t = _dot(decay_t, (rowi <= lanei).astype(F32), HIGHEST)
    cols = jnp.concatenate(
        [beta_t, gc_t, jnp.zeros((CHUNK - 2 * HEADS, CHUNK), F32)], axis=0).T

    heads = range(HEADS)
    hcols = [slice(h * HEAD_W, (h + 1) * HEAD_W) for h in heads]
    lmats, rhs, qgs, qkds, decs = [], [], [], [], []
    for hh in heads:
        hs = hcols[hh]
        beta = cols[:, hh:hh + 1]
        gc = cols[:, HEADS + hh:HEADS + hh + 1]
        gc_row = gc_t[hh:hh + 1]
        g_last = gc_row[:, CHUNK - 1:CHUNK]

        def conv_silu(idx, cw_ref):
            c0 = idx * width + hh * HEAD_W
            y = xs_ref[HIST:, c0:c0 + HEAD_W].astype(F32) * cw_ref[CONV_K - 1:CONV_K, hs]
            for j in range(CONV_K - 1):
                y = y + shifted[j * CHUNK:(j + 1) * CHUNK, c0:c0 + HEAD_W] * cw_ref[j:j + 1, hs]
            return y * jax.nn.sigmoid(y)

        q = conv_silu(0, cwq_ref)
        k = conv_silu(1, cwk_ref)
        v = conv_silu(2, cwv_ref)
        q = q * lax.rsqrt(jnp.sum(q * q, axis=-1, keepdims=True) + EPS) * (HEAD_W ** -0.5)
        k = k * lax.rsqrt(jnp.sum(k * k, axis=-1, keepdims=True) + EPS)

        decay = jnp.where(incl, jnp.exp(jnp.where(incl, gc - gc_row, 0.0)), 0.0)
        kb = k * beta
        k_t = k.T
        kt16 = k_t.astype(BF16)
        lmat = jnp.where(rowi > lanei, _dot(kb.astype(BF16), kt16) * decay, 0.0)
        qkds.append(jnp.concatenate(
            [(_dot(q.astype(BF16), kt16) * decay).astype(BF16),
             (k_t * jnp.exp(g_last - gc_row)).astype(BF16)], axis=0))
        qgs.append((q * jnp.exp(gc)).astype(BF16))
        decs.append(jnp.exp(g_last))
        lmats.append(lmat)
        rhs.append(jnp.concatenate([v * beta, kb * jnp.exp(gc)], axis=1))

    def same_block(size):
        return (rowi // size) == (lanei // size)

    l_sp = [_split(l) for l in lmats]
    mks = [jnp.where(same_block(8), -l, 0.0) for l in lmats]
    pks = [eye + m for m in mks]
    m_sp = [_split(m) for m in mks]
    m_sp = [_split(_dot3(m, m)) for m in m_sp]
    rs = [_dot3(m, [_split(p), m]) for m, p in zip(m_sp, pks)]
    pks = [p + r[:, :CHUNK] for p, r in zip(pks, rs)]
    pks = [p + _dot3(_split(r[:, CHUNK:]), _split(p)) for p, r in zip(pks, rs)]
    size = 8
    while size < CHUNK:
        join = same_block(2 * size) & jnp.logical_not(same_block(size))
        zero = jnp.zeros((), BF16)
        c_sp = [(jnp.where(join, hi, zero), jnp.where(join, lo, zero)) for hi, lo in l_sp]
        p_sp = [_split(p) for p in pks]
        xs = [_dot3(c, p) for c, p in zip(c_sp, p_sp)]
        pks = [p - _dot3(ps, _split(x)) for p, ps, x in zip(pks, p_sp, xs)]
        size *= 2
    uws = [_dot3(_split(p), _split(r)) for p, r in zip(pks, rhs)]

    states = [s_ref[h] for h in heads]
    ws_qs = [_dot(jnp.concatenate([uws[h][:, HEAD_W:].astype(BF16), qgs[h]], axis=0),
                  states[h].astype(BF16)) for h in heads]
    v16 = [(uws[h][:, :HEAD_W] - ws_qs[h][:CHUNK]).astype(BF16) for h in heads]
    qv_kv = [_dot(qkds[h], v16[h]) for h in heads]
    for h in heads:
        s_ref[h] = decs[h] * states[h] + qv_kv[h][CHUNK:]
        o = ws_qs[h][CHUNK:] + qv_kv[h][:CHUNK]
        z = z_ref[:, hcols[h]].astype(F32)
        o_ref[:, hcols[h]] = (
            _rms_rows(o, nw_ref[...]) * (z * jax.nn.sigmoid(z))).astype(o_ref.dtype)


def _gdn(proj3, lead_proj, gates3, lead_gates, conv_wt, alog_col, dtb_col, onw):
    b, seq, _ = proj3.shape
    nb = 1 + seq // CHUNK
    tok = lambda g: pl.BlockSpec(
        (None, CHUNK, GROUP_W), lambda b_, s: (b_, jnp.maximum(s - 1, 0), g))
    halo = lambda g: pl.BlockSpec(
        (None, 16, GROUP_W),
        lambda b_, s: (b_, jnp.maximum((s - 1) * (CHUNK // 16) - 1, 0), g))
    lead = lambda g: pl.BlockSpec((LEAD, GROUP_W), lambda b_, s: (0, g))
    cw = lambda g: pl.BlockSpec((CONV_K, GROUP_W), lambda b_, s: (0, g))
    col = pl.BlockSpec((HEADS, 1), lambda b_, s: (0, 0))
    return pl.pallas_call(
        _gdn_kernel,
        grid=(b, nb),
        in_specs=[
            tok(3), tok(4), tok(5), halo(3), halo(4), halo(5), lead(3), lead(4), lead(5),
            pl.BlockSpec((None, CHUNK, HEAD_W), lambda b_, s: (b_, jnp.maximum(s - 1, 0), 0)),
            pl.BlockSpec((LEAD, HEAD_W), lambda b_, s: (0, 0)),
            cw(0), cw(1), cw(2), col, col, tok(6),
            pl.BlockSpec((1, HEAD_W), lambda b_, s: (0, 0)),
        ],
        out_specs=pl.BlockSpec((None, CHUNK, GROUP_W), lambda b_, s: (b_, jnp.maximum(s - 1, 0), 0)),
        out_shape=jax.ShapeDtypeStruct((b, seq, GROUP_W), BF16),
        scratch_shapes=[pltpu.VMEM((HIST + CHUNK, 3 * GROUP_W), BF16),
                        pltpu.VMEM((CHUNK, HEAD_W), F32),
                        pltpu.VMEM((HEADS, HEAD_W, HEAD_W), F32)],
        compiler_params=pltpu.CompilerParams(
            dimension_semantics=("parallel", "arbitrary"), vmem_limit_bytes=VMEM_LIMIT),
        name="gdn",
    )(proj3, proj3, proj3, proj3, proj3, proj3, lead_proj, lead_proj, lead_proj,
      gates3, lead_gates, conv_wt, conv_wt, conv_wt, alog_col, dtb_col, proj3, onw)


def _outproj_kernel(x_ref, oa_ref, od_ref, wa_ref, wd_ref, o_ref):
    o_ref[...] = x_ref[...] + _dot(oa_ref[...], wa_ref[...]) + _dot(od_ref[...], wd_ref[...])


def _outproj(x2d, oa, od, w_out16, tm):
    m = x2d.shape[0]
    assert m % tm == 0
    return pl.pallas_call(
        _outproj_kernel,
        grid=(m // tm,),
        in_specs=[
            pl.BlockSpec((tm, D_MODEL), lambda i: (i, 0)),
            pl.BlockSpec((tm, GROUP_W), lambda i: (i, 0)),
            pl.BlockSpec((tm, GROUP_W), lambda i: (i, 0)),
            pl.BlockSpec((GROUP_W, D_MODEL), lambda i: (0, 0)),
            pl.BlockSpec((GROUP_W, D_MODEL), lambda i: (1, 0)),
        ],
        out_specs=pl.BlockSpec((tm, D_MODEL), lambda i: (i, 0)),
        out_shape=jax.ShapeDtypeStruct((m, D_MODEL), F32),
        compiler_params=pltpu.CompilerParams(
            dimension_semantics=("parallel",), vmem_limit_bytes=VMEM_LIMIT),
        name="outproj",
    )(x2d, oa, od, w_out16, w_out16)


def _ffn_kernel(h_ref, nw_ref, wg_ref, wu_ref, wd_ref, o_ref, u_ref, *, row_chunk):
    j = pl.program_id(1)

    @pl.when(j == 0)
    def _():
        def body(c, carry):
            r = pl.multiple_of(c * row_chunk, row_chunk)
            x = h_ref[pl.ds(r, row_chunk), :]
            u_ref[pl.ds(r, row_chunk), :] = _rms_rows(x, nw_ref[...]).astype(BF16)
            o_ref[pl.ds(r, row_chunk), :] = x
            return carry

        lax.fori_loop(0, h_ref.shape[0] // row_chunk, body, 0)

    u = u_ref[...]
    g = _dot(u, wg_ref[...])
    a = (g * jax.nn.sigmoid(g) * _dot(u, wu_ref[...])).astype(BF16)
    o_ref[...] += _dot(a, wd_ref[...])


def _ffn(h2d, norm_w, wg, wu, wd, tm, th):
    m = h2d.shape[0]
    assert m % tm == 0 and FFN_HIDDEN % th == 0
    return pl.pallas_call(
        functools.partial(_ffn_kernel, row_chunk=min(256, tm)),
        grid=(m // tm, FFN_HIDDEN // th),
        in_specs=[
            pl.BlockSpec((tm, D_MODEL), lambda i, j: (i, 0)),
            pl.BlockSpec((1, D_MODEL), lambda i, j: (0, 0)),
            pl.BlockSpec((D_MODEL, th), lambda i, j: (0, j)),
            pl.BlockSpec((D_MODEL, th), lambda i, j: (0, j)),
            pl.BlockSpec((th, D_MODEL), lambda i, j: (j, 0)),
        ],
        out_specs=pl.BlockSpec((tm, D_MODEL), lambda i, j: (i, 0)),
        out_shape=jax.ShapeDtypeStruct((m, D_MODEL), F32),
        scratch_shapes=[pltpu.VMEM((tm, D_MODEL), BF16)],
        compiler_params=pltpu.CompilerParams(
            dimension_semantics=("parallel", "arbitrary"), vmem_limit_bytes=VMEM_LIMIT),
        name="ffn",
    )(h2d, norm_w, wg, wu, wd)


def kernel(x, meta_tokens, attn_norm_w, w_in, q_norm_w, k_norm_w, lambda_q1, lambda_k1, lambda_q2,
           lambda_k2, subln_w, conv_w, a_log, dt_bias, o_norm_w, w_out, ffn_norm_w, w_gate, w_up,
           w_down):
    b, seq, _ = x.shape
    m = b * seq
    x2d = x.reshape(m, D_MODEL)
    lead = jnp.concatenate([jnp.zeros((N_PAD, D_MODEL), x.dtype), meta_tokens.astype(x.dtype)], 0)

    w_main = w_in[0, :, :MAIN_COLS].astype(BF16)
    w_gates = jnp.pad(w_in[0, :, MAIN_COLS:], ((0, 0), (0, HEAD_W - GATE_COLS))).astype(BF16)
    tm = min(1024, m)
    proj, gates = _inproj(x2d, attn_norm_w, w_main, w_gates, tm, 1792)
    lead_proj, lead_gates = _inproj(lead, attn_norm_w, w_main, w_gates, LEAD, 1024)
    proj3 = proj.reshape(b, seq, MAIN_COLS)

    slopes = 2.0 ** (-8.0 * jnp.arange(1, HEADS + 1, dtype=F32) / HEADS)
    lvec = jnp.concatenate([lambda_q1, lambda_k1, lambda_q2, lambda_k2], 0).astype(F32)
    o_a = _attn(proj3, lead_proj, slopes, lvec, jnp.tile(q_norm_w, (1, 2)),
                jnp.tile(k_norm_w, (1, 2)), subln_w.reshape(HEAD_W, 1), 512, ATTN_NSUB)

    o_d = _gdn(
        proj3, lead_proj, gates.reshape(b, seq, HEAD_W), lead_gates, conv_w[0].T,
        a_log.astype(F32).reshape(HEADS, 1), dt_bias.astype(F32).reshape(HEADS, 1), o_norm_w)

    h1 = _outproj(x2d, o_a.reshape(m, GROUP_W), o_d.reshape(m, GROUP_W), w_out[0].astype(BF16),
                  min(512, m))
    out = _ffn(h1, ffn_norm_w, w_gate[0].astype(BF16), w_up[0].astype(BF16),
               w_down[0].astype(BF16), min(1024, m), 512)
    return out.reshape(b, seq, D_MODEL)
```

```python
import functools

import jax
import jax.numpy as jnp
import numpy as np
from jax import lax
from jax.experimental import pallas as pl
from jax.experimental.pallas import tpu as pltpu

F32 = jnp.float32
BF16 = jnp.bfloat16
HIGHEST = lax.Precision.HIGHEST

D_MODEL = 2048
N_META = 16
LEAD = 128
N_PAD = LEAD - N_META
HEADS = 8
HEAD_W = 128
QK_DIM = 64
GROUP_W = HEADS * HEAD_W
MAIN_COLS = 7 * GROUP_W
GATE_COLS = 2 * HEADS
CONV_K = 4
HIST = 16
FFN_HIDDEN = 5632
EPS = 1e-6
NEG = -1e30
LAMBDA_INIT = 0.2
CHUNK = 128
VMEM_LIMIT = 56 * 1024 * 1024
LOG2E = 1.4426950408889634
INPROJ_ROWS, INPROJ_COLS = 1024, 1792
ATTN_BLOCK = 512
ATTN_NSUB = 4
OUTPROJ_ROWS = 512
FFN_ROWS, FFN_COLS = 1024, 512
VT_ROWS = HEAD_W + 16


def _bf16_pieces(x, n):
    out = []
    for _ in range(n):
        bits = np.array(x, np.float32).view(np.uint32)
        bits = (bits + 0x7FFF + ((bits >> 16) & 1)) & 0xFFFF0000
        p = float(bits.view(np.float32))
        out.append(p)
        x -= p
    return tuple(out)


LOG2E_BF16_PIECES = _bf16_pieces(LOG2E, 3)


def _dot(a, b, precision=None):
    return jnp.dot(a, b, preferred_element_type=F32, precision=precision)


def _dot_nt(a, b):
    return lax.dot_general(a, b, (((1,), (1,)), ((), ())), preferred_element_type=F32)


def _split(x):
    hi = x.astype(BF16)
    return hi, (x - hi.astype(F32)).astype(BF16)


def _dot3(a, b):
    a_hi, a_lo = a
    if isinstance(b, list):
        b_hi = jnp.concatenate([x[0] for x in b], axis=1)
        b_lo = jnp.concatenate([x[1] for x in b], axis=1)
    else:
        b_hi, b_lo = b
    return _dot(jnp.concatenate([a_hi, a_lo, a_hi], axis=1),
                jnp.concatenate([b_hi, b_hi, b_lo], axis=0))


def _rms_rows(x, w_row):
    return x * lax.rsqrt(jnp.mean(x * x, axis=-1, keepdims=True) + EPS) * w_row


def _inproj_kernel(x_ref, nw_ref, w_ref, wg_ref, o_ref, g_ref, u_ref, *, row_chunk):
    j = pl.program_id(1)

    @pl.when(j == 0)
    def _():
        def body(c, carry):
            r = pl.multiple_of(c * row_chunk, row_chunk)
            u = _rms_rows(x_ref[pl.ds(r, row_chunk), :], nw_ref[...]).astype(BF16)
            u_ref[pl.ds(r, row_chunk), :] = u
            g_ref[pl.ds(r, row_chunk), :] = _dot(u, wg_ref[...])
            return carry

        lax.fori_loop(0, x_ref.shape[0] // row_chunk, body, 0)

    o_ref[...] = _dot(u_ref[...], w_ref[...]).astype(o_ref.dtype)


def _inproj(x2d, norm_w, w_main, w_gate, tm, tn):
    m = x2d.shape[0]
    assert m % tm == 0 and MAIN_COLS % tn == 0
    row_chunk = min(256, tm)
    return pl.pallas_call(
        functools.partial(_inproj_kernel, row_chunk=row_chunk),
        grid=(m // tm, MAIN_COLS // tn),
        in_specs=[
            pl.BlockSpec((tm, D_MODEL), lambda i, j: (i, 0)),
            pl.BlockSpec((1, D_MODEL), lambda i, j: (0, 0)),
            pl.BlockSpec((D_MODEL, tn), lambda i, j: (0, j)),
            pl.BlockSpec((D_MODEL, HEAD_W), lambda i, j: (0, 0)),
        ],
        out_specs=[
            pl.BlockSpec((tm, tn), lambda i, j: (i, j)),
            pl.BlockSpec((tm, HEAD_W), lambda i, j: (i, 0)),
        ],
        out_shape=[
            jax.ShapeDtypeStruct((m, MAIN_COLS), BF16),
            jax.ShapeDtypeStruct((m, HEAD_W), F32),
        ],
        scratch_shapes=[pltpu.VMEM((tm, D_MODEL), BF16)],
        compiler_params=pltpu.CompilerParams(
            dimension_semantics=("parallel", "arbitrary"), vmem_limit_bytes=VMEM_LIMIT),
        name="inproj",
    )(x2d, norm_w, w_main, w_gate)


def _halfnorm(x, w_row):
    lo = lax.broadcasted_iota(jnp.int32, x.shape, 1) < QK_DIM
    x2 = x * x
    s_lo = jnp.sum(jnp.where(lo, x2, 0.0), axis=-1, keepdims=True)
    s_hi = jnp.sum(jnp.where(lo, 0.0, x2), axis=-1, keepdims=True)
    ms = jnp.where(lo, s_lo, s_hi) * (1.0 / QK_DIM)
    return x * lax.rsqrt(ms + EPS) * w_row


def _attn_kernel(slopes_ref, lvec_ref, q_ref, qnext_ref, k_ref, v_ref, lk_ref, lv_ref, qw_ref,
                 kw_ref, swc_ref, o_ref, kn_ref, vt_ref, lkn_ref, lvt_ref, kaug_ref, acc_ref,
                 sa_ref, sb_ref, wn_ref, wc_ref, kstage_ref, vstage_ref, *, tq, seq, nsub):
    h = pl.program_id(1)
    g = pl.program_id(2)
    step_rows = nsub * tq
    slope = slopes_ref[h]

    def query_operands(src_ref, dst_ref):
        lo = lax.broadcasted_iota(jnp.int32, (tq, HEAD_W), 1) < QK_DIM
        sub = lax.broadcasted_iota(jnp.int32, (HEAD_W, tq), 0)
        aug = jnp.zeros((HEAD_W, tq), F32)
        for n, piece in enumerate(LOG2E_BF16_PIECES):
            aug = jnp.where((sub == n) | (sub == n + 3), piece, aug)
        for blk in range(nsub):
            q_blk = src_ref[blk * tq:(blk + 1) * tq, :].astype(F32)
            qn = _halfnorm(q_blk, qw_ref[...]) * (QK_DIM ** -0.5 * LOG2E)
            for mp, x in enumerate((jnp.where(lo, qn, 0.0), jnp.where(lo, 0.0, qn))):
                dst_ref[blk, mp] = jnp.concatenate([x.T, aug], axis=0).astype(BF16)

    def key_operands(row0, kdst_ref, kdst0, vdst_ref, vdst0):
        al = lambda x: x if isinstance(x, int) else pl.multiple_of(x, 256)
        for c in range(step_rows // 256):
            src = pl.ds(al(row0 + c * 256), 256)
            dst = pl.ds(al(kdst0 + c * 256), 256)
            dstv = pl.ds(al(vdst0 + c * 256), 256)
            kdst_ref[dst, :] = _halfnorm(k_ref[src, :].astype(F32), kw_ref[...]).astype(BF16)
            vdst_ref[0:HEAD_W, dstv] = v_ref[src, :].astype(F32).T.astype(BF16)
            vdst_ref[HEAD_W:, dstv] = ones_row(256)

    def ones_row(n):
        return (lax.broadcasted_iota(jnp.int32, (VT_ROWS - HEAD_W, n), 0) == 0).astype(BF16)

    @pl.when(g == 0)
    def _():
        query_operands(q_ref, wn_ref)
        key_operands(0, kn_ref, 0, vt_ref, 0)
        lkn_ref[...] = _halfnorm(lk_ref[...].astype(F32), kw_ref[...]).astype(BF16)
        lvt_ref[0:HEAD_W, :] = lv_ref[...].astype(F32).T.astype(BF16)
        lvt_ref[HEAD_W:, :] = ones_row(LEAD)
        kk = lax.broadcasted_iota(jnp.int32, (tq, HEAD_W), 0)
        ln = lax.broadcasted_iota(jnp.int32, (tq, HEAD_W), 1)
        hi = ((kk // 16) * 16).astype(F32)
        lo_ = (kk % 16).astype(F32)
        kaug_ref[...] = (slope * jnp.where(ln < 3, hi, jnp.where(ln < 6, lo_, 0.0))).astype(BF16)

    @pl.when(g > 0)
    def _():
        r = pl.multiple_of(g * step_rows, step_rows)
        kn_ref[pl.ds(r, step_rows), :] = kstage_ref[...]
        vt_ref[:, pl.ds(r, step_rows)] = vstage_ref[...]

    wc_ref[...] = wn_ref[...]
    query_operands(qnext_ref, wn_ref)
    key_operands(jnp.minimum(g + 1, seq // step_rows - 1) * step_rows, kstage_ref, 0,
                 vstage_ref, 0)

    slope2 = slope * LOG2E
    q_off = slope2 * lax.broadcasted_iota(jnp.int32, (1, tq), 1).astype(F32)
    key_ok = lax.broadcasted_iota(jnp.int32, (LEAD, tq), 0) >= N_PAD
    pairs = [(blk, mp) for blk in range(nsub) for mp in range(2)]
    lead_s = [jnp.where(key_ok, _dot(lkn_ref[...], wc_ref[blk, mp, 0:HEAD_W, :]) + q_off, NEG)
              for blk, mp in pairs]
    lead_m = [jnp.max(s, axis=0, keepdims=True) for s in lead_s]
    lead_p = [jnp.exp2(s - m).astype(BF16) for s, m in zip(lead_s, lead_m)]
    for (blk, mp), p in zip(pairs, lead_p):
        acc_ref[blk, mp] = _dot(lvt_ref[...], p)
    carries = [(lead_m[2 * blk], lead_m[2 * blk + 1]) for blk in range(nsub)]

    key_i = lax.broadcasted_iota(jnp.int32, (tq, tq), 0)
    qry_i = lax.broadcasted_iota(jnp.int32, (tq, tq), 1)

    def scores(j, dst_ref, blk, diag):
        r = pl.multiple_of(j * tq, tq)
        lhs = jnp.concatenate([kn_ref[pl.ds(r, tq), :], kaug_ref[...]], axis=1)
        bms = []
        for mp in range(2):
            raw = _dot(lhs, wc_ref[blk, mp])
            if diag:
                raw = jnp.where(key_i <= qry_i, raw, NEG)
            dst_ref[mp] = raw
            bms.append(jnp.max(raw, axis=0, keepdims=True))
        return tuple(bms)

    def accumulate(j, src_ref, bms, carry, blk):
        r = pl.multiple_of(j * tq, tq)
        vt = vt_ref[:, pl.ds(r, tq)]
        c = slope2 * ((j - (nsub * g + blk)) * tq).astype(F32)
        out = []
        for mp in range(2):
            m_new = jnp.maximum(carry[mp], bms[mp] + c)
            alpha = jnp.exp2(carry[mp] - m_new)
            p = jnp.exp2(src_ref[mp] - (m_new - c)).astype(BF16)
            out.append(m_new)
            acc_ref[blk, mp] = alpha * acc_ref[blk, mp] + _dot(vt, p)
        return tuple(out)

    base = nsub * g
    own = [(blk, kb) for blk in range(nsub) for kb in range(blk + 1)]
    bufs = (sa_ref, sb_ref)
    bm_prev = scores(base + own[0][1], bufs[0], own[0][0], own[0][1] == own[0][0])
    for t in range(1, len(own)):
        (blk, kb), (pblk, pkb) = own[t], own[t - 1]
        bm = scores(base + kb, bufs[t % 2], blk, kb == blk)
        carries[pblk] = accumulate(base + pkb, bufs[(t - 1) % 2], bm_prev, carries[pblk], pblk)
        bm_prev = bm
    pend = (len(own) - 1) % 2

    def earlier(j, state):
        j_pend, bm_pend, cs = state[0], state[1], list(state[2])
        for blk in range(nsub):
            bm = scores(j, bufs[(pend + 1 + blk) % 2], blk, False)
            pblk, pj = (nsub - 1, j_pend) if blk == 0 else (blk - 1, j)
            cs[pblk] = accumulate(pj, bufs[(pend + blk) % 2], bm_pend, cs[pblk], pblk)
            bm_pend = bm
        return j, bm_pend, tuple(cs)

    j_pend, bm_pend, cs = lax.fori_loop(
        0, base, earlier, (base + nsub - 1, bm_prev, tuple(carries)))
    accumulate(j_pend, bufs[pend], bm_pend, cs[nsub - 1], nsub - 1)

    lv4 = lvec_ref[...]
    lam = (jnp.exp(jnp.sum(lv4[0:1] * lv4[1:2], axis=-1, keepdims=True))
           - jnp.exp(jnp.sum(lv4[2:3] * lv4[3:4], axis=-1, keepdims=True)) + LAMBDA_INIT)
    for blk in range(nsub):
        a0, a1 = acc_ref[blk, 0], acc_ref[blk, 1]
        o = (a0[0:HEAD_W] / a0[HEAD_W:HEAD_W + 1]
             - lam * (a1[0:HEAD_W] / a1[HEAD_W:HEAD_W + 1]))
        o = o * lax.rsqrt(jnp.mean(o * o, axis=0, keepdims=True) + EPS) * swc_ref[...]
        o_ref[blk * tq:(blk + 1) * tq, :] = (o * (1.0 - LAMBDA_INIT)).T.astype(o_ref.dtype)


def _attn(proj3, lead_proj, slopes, lvec, qw, kw, sw, tq, nsub):
    b, seq, _ = proj3.shape
    rows = nsub * tq
    assert seq % rows == 0 and rows % 256 == 0 and tq % 16 == 0 and tq <= 512 and nsub % 2 == 0
    nsteps = seq // rows
    return pl.pallas_call(
        functools.partial(_attn_kernel, tq=tq, seq=seq, nsub=nsub),
        grid=(b, HEADS, nsteps),
        in_specs=[
            pl.BlockSpec(memory_space=pltpu.SMEM),
            pl.BlockSpec((4, QK_DIM), lambda b_, h, i: (0, 0)),
            pl.BlockSpec((None, rows, HEAD_W), lambda b_, h, i: (b_, i, h)),
            pl.BlockSpec((None, rows, HEAD_W),
                         lambda b_, h, i: (b_, jnp.minimum(i + 1, nsteps - 1), h)),
            pl.BlockSpec((None, seq, HEAD_W), lambda b_, h, i: (b_, 0, HEADS + h)),
            pl.BlockSpec((None, seq, HEAD_W), lambda b_, h, i: (b_, 0, 2 * HEADS + h)),
            pl.BlockSpec((LEAD, HEAD_W), lambda b_, h, i: (0, HEADS + h)),
            pl.BlockSpec((LEAD, HEAD_W), lambda b_, h, i: (0, 2 * HEADS + h)),
            pl.BlockSpec((1, HEAD_W), lambda b_, h, i: (0, 0)),
            pl.BlockSpec((1, HEAD_W), lambda b_, h, i: (0, 0)),
            pl.BlockSpec((HEAD_W, 1), lambda b_, h, i: (0, 0)),
        ],
        out_specs=pl.BlockSpec((None, rows, HEAD_W), lambda b_, h, i: (b_, i, h)),
        out_shape=jax.ShapeDtypeStruct((b, seq, GROUP_W), BF16),
        scratch_shapes=[
            pltpu.VMEM((seq, HEAD_W), BF16),
            pltpu.VMEM((VT_ROWS, seq), BF16),
            pltpu.VMEM((LEAD, HEAD_W), BF16),
            pltpu.VMEM((VT_ROWS, LEAD), BF16),
            pltpu.VMEM((tq, HEAD_W), BF16),
            pltpu.VMEM((nsub, 2, VT_ROWS, tq), F32),
            pltpu.VMEM((2, tq, tq), F32),
            pltpu.VMEM((2, tq, tq), F32),
            pltpu.VMEM((nsub, 2, 2 * HEAD_W, tq), BF16),
            pltpu.VMEM((nsub, 2, 2 * HEAD_W, tq), BF16),
            pltpu.VMEM((rows, HEAD_W), BF16),
            pltpu.VMEM((VT_ROWS, rows), BF16),
        ],
        compiler_params=pltpu.CompilerParams(
            dimension_semantics=("parallel", "parallel", "arbitrary"),
            vmem_limit_bytes=VMEM_LIMIT),
        name="diff_attn",
    )(slopes, lvec, proj3, proj3, proj3, proj3, lead_proj, lead_proj, qw, kw, sw)


def _gdn_kernel(tq_ref, tk_ref, tv_ref, hq_ref, hk_ref, hv_ref, lq_ref, lk_ref, lv_ref,
                tg_ref, lg_ref, cwq_ref, cwk_ref, cwv_ref, alog_ref, dtb_ref, z_ref, nw_ref,
                o_ref, xs_ref, gs_ref, s_ref):
    s = pl.program_id(1)
    is_lead = s == 0
    width = GROUP_W
    srcs = ((tq_ref, hq_ref, lq_ref), (tk_ref, hk_ref, lk_ref), (tv_ref, hv_ref, lv_ref))

    @pl.when(is_lead)
    def _():
        rowid = lax.broadcasted_iota(jnp.int32, (CHUNK, width), 0)
        for idx, (_, _, l_ref) in enumerate(srcs):
            cs = slice(idx * width, (idx + 1) * width)
            xs_ref[0:HIST, cs] = jnp.zeros((HIST, width), BF16)
            xs_ref[HIST:, cs] = jnp.where(rowid >= N_PAD, l_ref[...], jnp.zeros((), BF16))
        gs_ref[...] = lg_ref[...]
        s_ref[...] = jnp.zeros_like(s_ref)

    @pl.when(s == 1)
    def _():
        for idx, (t_ref, _, l_ref) in enumerate(srcs):
            cs = slice(idx * width, (idx + 1) * width)
            xs_ref[0:HIST, cs] = l_ref[LEAD - HIST:LEAD, :]
            xs_ref[HIST:, cs] = t_ref[...]
        gs_ref[...] = tg_ref[...]

    @pl.when(s > 1)
    def _():
        for idx, (t_ref, h_ref, _) in enumerate(srcs):
            cs = slice(idx * width, (idx + 1) * width)
            xs_ref[0:HIST, cs] = h_ref[...]
            xs_ref[HIST:, cs] = t_ref[...]
        gs_ref[...] = tg_ref[...]

    sel_r = lax.broadcasted_iota(jnp.int32, ((CONV_K - 1) * CHUNK, HIST + CHUNK), 0)
    sel_c = lax.broadcasted_iota(jnp.int32, ((CONV_K - 1) * CHUNK, HIST + CHUNK), 1)
    tap_of = sel_r // CHUNK
    shifted = _dot((sel_c == HIST + sel_r % CHUNK - (CONV_K - 1 - tap_of)).astype(BF16),
                   xs_ref[...])

    rowi = lax.broadcasted_iota(jnp.int32, (CHUNK, CHUNK), 0)
    lanei = lax.broadcasted_iota(jnp.int32, (CHUNK, CHUNK), 1)
    incl = rowi >= lanei
    eye = (rowi == lanei).astype(F32)

    g_t = gs_ref[...].T[0:2 * HEADS]
    vmask = (lax.broadcasted_iota(jnp.int32, (HEADS, CHUNK), 1)
             >= jnp.where(is_lead, N_PAD, 0)).astype(F32)
    beta_t = jax.nn.sigmoid(g_t[0:HEADS]) * vmask
    t = g_t[HEADS:] + dtb_ref[...]
    softplus = jnp.maximum(t, 0.0) + jnp.log(1.0 + jnp.exp(-jnp.abs(t)))
    decay_t = -jnp.exp(alog_ref[...]) * softplus * vmask
    gc_t = _dot(decay_t, (rowi <= lanei).astype(F32), HIGHEST)
    cols = jnp.concatenate(
        [beta_t, gc_t, jnp.zeros((CHUNK - 2 * HEADS, CHUNK), F32)], axis=0).T

    heads = range(HEADS)
    hcols = [slice(h * HEAD_W, (h + 1) * HEAD_W) for h in heads]
    lmats, rhs, qgs, qkds, decs = [], [], [], [], []
    for hh in heads:
        hs = hcols[hh]
        beta = cols[:, hh:hh + 1]
        gc = cols[:, HEADS + hh:HEADS + hh + 1]
        gc_row = gc_t[hh:hh + 1]
        g_last = gc_row[:, CHUNK - 1:CHUNK]

        def conv_silu(idx, cw_ref):
            c0 = idx * width + hh * HEAD_W
            y = xs_ref[HIST:, c0:c0 + HEAD_W].astype(F32) * cw_ref[CONV_K - 1:CONV_K, hs]
            for j in range(CONV_K - 1):
                y = y + shifted[j * CHUNK:(j + 1) * CHUNK, c0:c0 + HEAD_W] * cw_ref[j:j + 1, hs]
            return y * jax.nn.sigmoid(y)

        q = conv_silu(0, cwq_ref)
        k = conv_silu(1, cwk_ref)
        v = conv_silu(2, cwv_ref)
        q = q * lax.rsqrt(jnp.sum(q * q, axis=-1, keepdims=True) + EPS) * (HEAD_W ** -0.5)
        k = k * lax.rsqrt(jnp.sum(k * k, axis=-1, keepdims=True) + EPS)

        decay = jnp.where(incl, jnp.exp(jnp.where(incl, gc - gc_row, 0.0)), 0.0)
        kb = k * beta
        k_t = k.T
        kt16 = k_t.astype(BF16)
        lmat = jnp.where(rowi > lanei, _dot(kb.astype(BF16), kt16) * decay, 0.0)
        qkds.append(jnp.concatenate(
            [(_dot(q.astype(BF16), kt16) * decay).astype(BF16),
             (k_t * jnp.exp(g_last - gc_row)).astype(BF16)], axis=0))
        qgs.append((q * jnp.exp(gc)).astype(BF16))
        decs.append(jnp.exp(g_last))
        lmats.append(lmat)
        rhs.append(jnp.concatenate([v * beta, kb * jnp.exp(gc)], axis=1))

    def same_block(size):
        return (rowi // size) == (lanei // size)

    l_sp = [_split(l) for l in lmats]
    mks = [jnp.where(same_block(8), -l, 0.0) for l in lmats]
    pks = [eye + m for m in mks]
    m_sp = [_split(m) for m in mks]
    m_sp = [_split(_dot3(m, m)) for m in m_sp]
    rs = [_dot3(m, [_split(p), m]) for m, p in zip(m_sp, pks)]
    pks = [p + r[:, :CHUNK] for p, r in zip(pks, rs)]
    pks = [p + _dot3(_split(r[:, CHUNK:]), _split(p)) for p, r in zip(pks, rs)]
    size = 8
    while size < CHUNK:
        join = same_block(2 * size) & jnp.logical_not(same_block(size))
        zero = jnp.zeros((), BF16)
        c_sp = [(jnp.where(join, hi, zero), jnp.where(join, lo, zero)) for hi, lo in l_sp]
        p_sp = [_split(p) for p in pks]
        xs = [_dot3(c, p) for c, p in zip(c_sp, p_sp)]
        pks = [p - _dot3(ps, _split(x)) for p, ps, x in zip(pks, p_sp, xs)]
        size *= 2
    uws = [_dot3(_split(p), _split(r)) for p, r in zip(pks, rhs)]

    states = [s_ref[h] for h in heads]
    ws_qs = [_dot(jnp.concatenate([uws[h][:, HEAD_W:].astype(BF16), qgs[h]], axis=0),
                  states[h].astype(BF16)) for h in heads]
    v16 = [(uws[h][:, :HEAD_W] - ws_qs[h][:CHUNK]).astype(BF16) for h in heads]
    qv_kv = [_dot(qkds[h], v16[h]) for h in heads]
    for h in heads:
        s_ref[h] = decs[h] * states[h] + qv_kv[h][CHUNK:]
        o = ws_qs[h][CHUNK:] + qv_kv[h][:CHUNK]
        z = z_ref[:, hcols[h]].astype(F32)
        o_ref[:, hcols[h]] = (
            _rms_rows(o, nw_ref[...]) * (z * jax.nn.sigmoid(z))).astype(o_ref.dtype)


def _gdn(proj3, lead_proj, gates3, lead_gates, conv_wt, alog_col, dtb_col, onw):
    b, seq, _ = proj3.shape
    nb = 1 + seq // CHUNK
    tok = lambda g: pl.BlockSpec(
        (None, CHUNK, GROUP_W), lambda b_, s: (b_, jnp.maximum(s - 1, 0), g))
    halo = lambda g: pl.BlockSpec(
        (None, 16, GROUP_W),
        lambda b_, s: (b_, jnp.maximum((s - 1) * (CHUNK // 16) - 1, 0), g))
    lead = lambda g: pl.BlockSpec((LEAD, GROUP_W), lambda b_, s: (0, g))
    cw = lambda g: pl.BlockSpec((CONV_K, GROUP_W), lambda b_, s: (0, g))
    col = pl.BlockSpec((HEADS, 1), lambda b_, s: (0, 0))
    return pl.pallas_call(
        _gdn_kernel,
        grid=(b, nb),
        in_specs=[
            tok(3), tok(4), tok(5), halo(3), halo(4), halo(5), lead(3), lead(4), lead(5),
            pl.BlockSpec((None, CHUNK, HEAD_W), lambda b_, s: (b_, jnp.maximum(s - 1, 0), 0)),
            pl.BlockSpec((LEAD, HEAD_W), lambda b_, s: (0, 0)),
            cw(0), cw(1), cw(2), col, col, tok(6),
            pl.BlockSpec((1, HEAD_W), lambda b_, s: (0, 0)),
        ],
        out_specs=pl.BlockSpec((None, CHUNK, GROUP_W), lambda b_, s: (b_, jnp.maximum(s - 1, 0), 0)),
        out_shape=jax.ShapeDtypeStruct((b, seq, GROUP_W), BF16),
        scratch_shapes=[pltpu.VMEM((HIST + CHUNK, 3 * GROUP_W), BF16),
                        pltpu.VMEM((CHUNK, HEAD_W), F32),
                        pltpu.VMEM((HEADS, HEAD_W, HEAD_W), F32)],
        compiler_params=pltpu.CompilerParams(
            dimension_semantics=("parallel", "arbitrary"), vmem_limit_bytes=VMEM_LIMIT),
        name="gdn",
    )(proj3, proj3, proj3, proj3, proj3, proj3, lead_proj, lead_proj, lead_proj,
      gates3, lead_gates, conv_wt, conv_wt, conv_wt, alog_col, dtb_col, proj3, onw)


def _outproj_kernel(x_ref, oa_ref, od_ref, wa_ref, wd_ref, o_ref):
    o_ref[...] = x_ref[...] + _dot(oa_ref[...], wa_ref[...]) + _dot(od_ref[...], wd_ref[...])


def _outproj(x2d, oa, od, w_out16, tm):
    m = x2d.shape[0]
    assert m % tm == 0
    return pl.pallas_call(
        _outproj_kernel,
        grid=(m // tm,),
        in_specs=[
            pl.BlockSpec((tm, D_MODEL), lambda i: (i, 0)),
            pl.BlockSpec((tm, GROUP_W), lambda i: (i, 0)),
            pl.BlockSpec((tm, GROUP_W), lambda i: (i, 0)),
            pl.BlockSpec((GROUP_W, D_MODEL), lambda i: (0, 0)),
            pl.BlockSpec((GROUP_W, D_MODEL), lambda i: (1, 0)),
        ],
        out_specs=pl.BlockSpec((tm, D_MODEL), lambda i: (i, 0)),
        out_shape=jax.ShapeDtypeStruct((m, D_MODEL), F32),
        compiler_params=pltpu.CompilerParams(
            dimension_semantics=("parallel",), vmem_limit_bytes=VMEM_LIMIT),
        name="outproj",
    )(x2d, oa, od, w_out16, w_out16)


def _ffn_kernel(h_ref, nw_ref, wg_ref, wu_ref, wd_ref, o_ref, u_ref, *, row_chunk):
    j = pl.program_id(1)

    @pl.when(j == 0)
    def _():
        def body(c, carry):
            r = pl.multiple_of(c * row_chunk, row_chunk)
            x = h_ref[pl.ds(r, row_chunk), :]
            u_ref[pl.ds(r, row_chunk), :] = _rms_rows(x, nw_ref[...]).astype(BF16)
            o_ref[pl.ds(r, row_chunk), :] = x
            return carry

        lax.fori_loop(0, h_ref.shape[0] // row_chunk, body, 0)

    u = u_ref[...]
    g = _dot(u, wg_ref[...])
    a = (g * jax.nn.sigmoid(g) * _dot(u, wu_ref[...])).astype(BF16)
    o_ref[...] += _dot(a, wd_ref[...])


def _ffn(h2d, norm_w, wg, wu, wd, tm, th):
    m = h2d.shape[0]
    assert m % tm == 0 and FFN_HIDDEN % th == 0
    return pl.pallas_call(
        functools.partial(_ffn_kernel, row_chunk=min(256, tm)),
        grid=(m // tm, FFN_HIDDEN // th),
        in_specs=[
            pl.BlockSpec((tm, D_MODEL), lambda i, j: (i, 0)),
            pl.BlockSpec((1, D_MODEL), lambda i, j: (0, 0)),
            pl.BlockSpec((D_MODEL, th), lambda i, j: (0, j)),
            pl.BlockSpec((D_MODEL, th), lambda i, j: (0, j)),
            pl.BlockSpec((th, D_MODEL), lambda i, j: (j, 0)),
        ],
        out_specs=pl.BlockSpec((tm, D_MODEL), lambda i, j: (i, 0)),
        out_shape=jax.ShapeDtypeStruct((m, D_MODEL), F32),
        scratch_shapes=[pltpu.VMEM((tm, D_MODEL), BF16)],
        compiler_params=pltpu.CompilerParams(
            dimension_semantics=("parallel", "arbitrary"), vmem_limit_bytes=VMEM_LIMIT),
        name="ffn",
    )(h2d, norm_w, wg, wu, wd)


def kernel(x, meta_tokens, attn_norm_w, w_in, q_norm_w, k_norm_w, lambda_q1, lambda_k1, lambda_q2,
           lambda_k2, subln_w, conv_w, a_log, dt_bias, o_norm_w, w_out, ffn_norm_w, w_gate, w_up,
           w_down):
    b, seq, _ = x.shape
    m = b * seq
    x2d = x.reshape(m, D_MODEL)
    lead = jnp.concatenate([jnp.zeros((N_PAD, D_MODEL), x.dtype), meta_tokens.astype(x.dtype)], 0)

    w_main = w_in[0, :, :MAIN_COLS].astype(BF16)
    w_gates = jnp.pad(w_in[0, :, MAIN_COLS:], ((0, 0), (0, HEAD_W - GATE_COLS))).astype(BF16)
    proj, gates = _inproj(x2d, attn_norm_w, w_main, w_gates, min(INPROJ_ROWS, m), INPROJ_COLS)
    lead_proj, lead_gates = _inproj(lead, attn_norm_w, w_main, w_gates, LEAD, INPROJ_COLS)
    proj3 = proj.reshape(b, seq, MAIN_COLS)

    slopes = 2.0 ** (-8.0 * jnp.arange(1, HEADS + 1, dtype=F32) / HEADS)
    lvec = jnp.concatenate([lambda_q1, lambda_k1, lambda_q2, lambda_k2], 0).astype(F32)
    o_a = _attn(proj3, lead_proj, slopes, lvec, jnp.tile(q_norm_w, (1, 2)),
                jnp.tile(k_norm_w, (1, 2)), subln_w.reshape(HEAD_W, 1), ATTN_BLOCK, ATTN_NSUB)

    o_d = _gdn(
        proj3, lead_proj, gates.reshape(b, seq, HEAD_W), lead_gates, conv_w[0].T,
        a_log.astype(F32).reshape(HEADS, 1), dt_bias.astype(F32).reshape(HEADS, 1), o_norm_w)

    h1 = _outproj(x2d, o_a.reshape(m, GROUP_W), o_d.reshape(m, GROUP_W), w_out[0].astype(BF16),
                  min(OUTPROJ_ROWS, m))
    out = _ffn(h1, ffn_norm_w, w_gate[0].astype(BF16), w_up[0].astype(BF16),
               w_down[0].astype(BF16), min(FFN_ROWS, m), FFN_COLS)
    return out.reshape(b, seq, D_MODEL)
```

```python
import functools

import jax
import jax.numpy as jnp
import numpy as np
from jax import lax
from jax.experimental import pallas as pl
from jax.experimental.pallas import tpu as pltpu

F32 = jnp.float32
BF16 = jnp.bfloat16
HIGHEST = lax.Precision.HIGHEST

D_MODEL = 2048
N_META = 16
LEAD = 128
N_PAD = LEAD - N_META
HEADS = 8
HEAD_W = 128
QK_DIM = 64
GROUP_W = HEADS * HEAD_W
MAIN_COLS = 7 * GROUP_W
GATE_COLS = 2 * HEADS
CONV_K = 4
HIST = 16
FFN_HIDDEN = 5632
EPS = 1e-6
NEG = -1e30
LAMBDA_INIT = 0.2
CHUNK = 128
VMEM_LIMIT = 56 * 1024 * 1024
LOG2E = 1.4426950408889634
INPROJ_ROWS, INPROJ_COLS = 1024, 1792
ATTN_BLOCK = 512
ATTN_NSUB = 4
OUTPROJ_ROWS = 512
FFN_ROWS, FFN_COLS = 1024, 512
VT_ROWS = HEAD_W + 16


def _bf16_pieces(x, n):
    out = []
    for _ in range(n):
        bits = np.array(x, np.float32).view(np.uint32)
        bits = (bits + 0x7FFF + ((bits >> 16) & 1)) & 0xFFFF0000
        p = float(bits.view(np.float32))
        out.append(p)
        x -= p
    return tuple(out)


LOG2E_BF16_PIECES = _bf16_pieces(LOG2E, 3)


def _dot(a, b, precision=None):
    return jnp.dot(a, b, preferred_element_type=F32, precision=precision)


def _dot_nt(a, b):
    return lax.dot_general(a, b, (((1,), (1,)), ((), ())), preferred_element_type=F32)


def _split(x):
    hi = x.astype(BF16)
    return hi, (x - hi.astype(F32)).astype(BF16)


def _dot3(a, b):
    a_hi, a_lo = a
    if isinstance(b, list):
        b_hi = jnp.concatenate([x[0] for x in b], axis=1)
        b_lo = jnp.concatenate([x[1] for x in b], axis=1)
    else:
        b_hi, b_lo = b
    return _dot(jnp.concatenate([a_hi, a_lo, a_hi], axis=1),
                jnp.concatenate([b_hi, b_hi, b_lo], axis=0))


def _rms_rows(x, w_row):
    return x * lax.rsqrt(jnp.mean(x * x, axis=-1, keepdims=True) + EPS) * w_row


def _inproj_kernel(x_ref, nw_ref, w_ref, wg_ref, o_ref, g_ref, u_ref, *, row_chunk):
    j = pl.program_id(1)

    @pl.when(j == 0)
    def _():
        def body(c, carry):
            r = pl.multiple_of(c * row_chunk, row_chunk)
            u = _rms_rows(x_ref[pl.ds(r, row_chunk), :], nw_ref[...]).astype(BF16)
            u_ref[pl.ds(r, row_chunk), :] = u
            g_ref[pl.ds(r, row_chunk), :] = _dot(u, wg_ref[...])
            return carry

        lax.fori_loop(0, x_ref.shape[0] // row_chunk, body, 0)

    o_ref[...] = _dot(u_ref[...], w_ref[...]).astype(o_ref.dtype)


def _inproj(x2d, norm_w, w_main, w_gate, tm, tn):
    m = x2d.shape[0]
    assert m % tm == 0 and MAIN_COLS % tn == 0
    row_chunk = min(256, tm)
    return pl.pallas_call(
        functools.partial(_inproj_kernel, row_chunk=row_chunk),
        grid=(m // tm, MAIN_COLS // tn),
        in_specs=[
            pl.BlockSpec((tm, D_MODEL), lambda i, j: (i, 0)),
            pl.BlockSpec((1, D_MODEL), lambda i, j: (0, 0)),
            pl.BlockSpec((D_MODEL, tn), lambda i, j: (0, j)),
            pl.BlockSpec((D_MODEL, HEAD_W), lambda i, j: (0, 0)),
        ],
        out_specs=[
            pl.BlockSpec((tm, tn), lambda i, j: (i, j)),
            pl.BlockSpec((tm, HEAD_W), lambda i, j: (i, 0)),
        ],
        out_shape=[
            jax.ShapeDtypeStruct((m, MAIN_COLS), BF16),
            jax.ShapeDtypeStruct((m, HEAD_W), F32),
        ],
        scratch_shapes=[pltpu.VMEM((tm, D_MODEL), BF16)],
        compiler_params=pltpu.CompilerParams(
            dimension_semantics=("parallel", "arbitrary"), vmem_limit_bytes=VMEM_LIMIT),
        name="inproj",
    )(x2d, norm_w, w_main, w_gate)


def _halfnorm(x, w_row):
    lo = lax.broadcasted_iota(jnp.int32, x.shape, 1) < QK_DIM
    x2 = x * x
    s_lo = jnp.sum(jnp.where(lo, x2, 0.0), axis=-1, keepdims=True)
    s_hi = jnp.sum(jnp.where(lo, 0.0, x2), axis=-1, keepdims=True)
    ms = jnp.where(lo, s_lo, s_hi) * (1.0 / QK_DIM)
    return x * lax.rsqrt(ms + EPS) * w_row


def _attn_kernel(slopes_ref, lvec_ref, q_ref, qnext_ref, k_ref, v_ref, lk_ref, lv_ref, qw_ref,
                 kw_ref, swc_ref, o_ref, kn_ref, vt_ref, lkn_ref, lvt_ref, kaug_ref, acc_ref,
                 sa_ref, sb_ref, wn_ref, wc_ref, kstage_ref, vstage_ref, *, tq, seq, nsub):
    h = pl.program_id(1)
    g = pl.program_id(2)
    step_rows = nsub * tq
    slope = slopes_ref[h]

    def query_operands(src_ref, dst_ref):
        lo = lax.broadcasted_iota(jnp.int32, (tq, HEAD_W), 1) < QK_DIM
        sub = lax.broadcasted_iota(jnp.int32, (HEAD_W, tq), 0)
        aug = jnp.zeros((HEAD_W, tq), F32)
        for n, piece in enumerate(LOG2E_BF16_PIECES):
            aug = jnp.where((sub == n) | (sub == n + 3), piece, aug)
        for blk in range(nsub):
            q_blk = src_ref[blk * tq:(blk + 1) * tq, :].astype(F32)
            qn = _halfnorm(q_blk, qw_ref[...]) * (QK_DIM ** -0.5 * LOG2E)
            for mp, x in enumerate((jnp.where(lo, qn, 0.0), jnp.where(lo, 0.0, qn))):
                dst_ref[blk, mp] = jnp.concatenate([x.T, aug], axis=0).astype(BF16)

    def key_operands(row0, kdst_ref, kdst0, vdst_ref, vdst0):
        al = lambda x: x if isinstance(x, int) else pl.multiple_of(x, 256)
        for c in range(step_rows // 256):
            src = pl.ds(al(row0 + c * 256), 256)
            dst = pl.ds(al(kdst0 + c * 256), 256)
            dstv = pl.ds(al(vdst0 + c * 256), 256)
            kdst_ref[dst, :] = _halfnorm(k_ref[src, :].astype(F32), kw_ref[...]).astype(BF16)
            vdst_ref[0:HEAD_W, dstv] = v_ref[src, :].astype(F32).T.astype(BF16)
            vdst_ref[HEAD_W:, dstv] = ones_row(256)

    def ones_row(n):
        return (lax.broadcasted_iota(jnp.int32, (VT_ROWS - HEAD_W, n), 0) == 0).astype(BF16)

    @pl.when(g == 0)
    def _():
        query_operands(q_ref, wn_ref)
        key_operands(0, kn_ref, 0, vt_ref, 0)
        lkn_ref[...] = _halfnorm(lk_ref[...].astype(F32), kw_ref[...]).astype(BF16)
        lvt_ref[0:HEAD_W, :] = lv_ref[...].astype(F32).T.astype(BF16)
        lvt_ref[HEAD_W:, :] = ones_row(LEAD)
        kk = lax.broadcasted_iota(jnp.int32, (tq, HEAD_W), 0)
        ln = lax.broadcasted_iota(jnp.int32, (tq, HEAD_W), 1)
        hi = ((kk // 16) * 16).astype(F32)
        lo_ = (kk % 16).astype(F32)
        kaug_ref[...] = (slope * jnp.where(ln < 3, hi, jnp.where(ln < 6, lo_, 0.0))).astype(BF16)

    @pl.when(g > 0)
    def _():
        r = pl.multiple_of(g * step_rows, step_rows)
        kn_ref[pl.ds(r, step_rows), :] = kstage_ref[...]
        vt_ref[:, pl.ds(r, step_rows)] = vstage_ref[...]

    wc_ref[...] = wn_ref[...]
    query_operands(qnext_ref, wn_ref)
    key_operands(jnp.minimum(g + 1, seq // step_rows - 1) * step_rows, kstage_ref, 0,
                 vstage_ref, 0)

    slope2 = slope * LOG2E
    q_off = slope2 * lax.broadcasted_iota(jnp.int32, (1, tq), 1).astype(F32)
    key_ok = lax.broadcasted_iota(jnp.int32, (LEAD, tq), 0) >= N_PAD
    pairs = [(blk, mp) for blk in range(nsub) for mp in range(2)]
    lead_s = [jnp.where(key_ok, _dot(lkn_ref[...], wc_ref[blk, mp, 0:HEAD_W, :]) + q_off, NEG)
              for blk, mp in pairs]
    lead_m = [jnp.max(s, axis=0, keepdims=True) for s in lead_s]
    lead_p = [jnp.exp2(s - m).astype(BF16) for s, m in zip(lead_s, lead_m)]
    for (blk, mp), p in zip(pairs, lead_p):
        acc_ref[blk, mp] = _dot(lvt_ref[...], p)
    carries = [(lead_m[2 * blk], lead_m[2 * blk + 1]) for blk in range(nsub)]

    key_i = lax.broadcasted_iota(jnp.int32, (tq, tq), 0)
    qry_i = lax.broadcasted_iota(jnp.int32, (tq, tq), 1)

    def scores(j, dst_ref, blk, diag):
        r = pl.multiple_of(j * tq, tq)
        lhs = jnp.concatenate([kn_ref[pl.ds(r, tq), :], kaug_ref[...]], axis=1)
        bms = []
        for mp in range(2):
            raw = _dot(lhs, wc_ref[blk, mp])
            if diag:
                raw = jnp.where(key_i <= qry_i, raw, NEG)
            dst_ref[mp] = raw
            bms.append(jnp.max(raw, axis=0, keepdims=True))
        return tuple(bms)

    def accumulate(j, src_ref, bms, carry, blk):
        r = pl.multiple_of(j * tq, tq)
        vt = vt_ref[:, pl.ds(r, tq)]
        c = slope2 * ((j - (nsub * g + blk)) * tq).astype(F32)
        out = []
        for mp in range(2):
            m_new = jnp.maximum(carry[mp], bms[mp] + c)
            alpha = jnp.exp2(carry[mp] - m_new)
            p = jnp.exp2(src_ref[mp] - (m_new - c)).astype(BF16)
            out.append(m_new)
            acc_ref[blk, mp] = alpha * acc_ref[blk, mp] + _dot(vt, p)
        return tuple(out)

    base = nsub * g
    own = [(blk, kb) for blk in range(nsub) for kb in range(blk + 1)]
    bufs = (sa_ref, sb_ref)
    bm_prev = scores(base + own[0][1], bufs[0], own[0][0], own[0][1] == own[0][0])
    for t in range(1, len(own)):
        (blk, kb), (pblk, pkb) = own[t], own[t - 1]
        bm = scores(base + kb, bufs[t % 2], blk, kb == blk)
        carries[pblk] = accumulate(base + pkb, bufs[(t - 1) % 2], bm_prev, carries[pblk], pblk)
        bm_prev = bm
    pend = (len(own) - 1) % 2

    def earlier(j, state):
        j_pend, bm_pend, cs = state[0], state[1], list(state[2])
        for blk in range(nsub):
            bm = scores(j, bufs[(pend + 1 + blk) % 2], blk, False)
            pblk, pj = (nsub - 1, j_pend) if blk == 0 else (blk - 1, j)
            cs[pblk] = accumulate(pj, bufs[(pend + blk) % 2], bm_pend, cs[pblk], pblk)
            bm_pend = bm
        return j, bm_pend, tuple(cs)

    j_pend, bm_pend, cs = lax.fori_loop(
        0, base, earlier, (base + nsub - 1, bm_prev, tuple(carries)))
    accumulate(j_pend, bufs[pend], bm_pend, cs[nsub - 1], nsub - 1)

    lv4 = lvec_ref[...]
    lam = (jnp.exp(jnp.sum(lv4[0:1] * lv4[1:2], axis=-1, keepdims=True))
           - jnp.exp(jnp.sum(lv4[2:3] * lv4[3:4], axis=-1, keepdims=True)) + LAMBDA_INIT)
    for blk in range(nsub):
        a0, a1 = acc_ref[blk, 0], acc_ref[blk, 1]
        o = (a0[0:HEAD_W] / a0[HEAD_W:HEAD_W + 1]
             - lam * (a1[0:HEAD_W] / a1[HEAD_W:HEAD_W + 1]))
        o = o * lax.rsqrt(jnp.mean(o * o, axis=0, keepdims=True) + EPS) * swc_ref[...]
        o_ref[blk * tq:(blk + 1) * tq, :] = (o * (1.0 - LAMBDA_INIT)).T.astype(o_ref.dtype)


def _attn(proj3, lead_proj, slopes, lvec, qw, kw, sw, tq, nsub):
    b, seq, _ = proj3.shape
    rows = nsub * tq
    assert seq % rows == 0 and rows % 256 == 0 and tq % 16 == 0 and tq <= 512 and nsub % 2 == 0
    nsteps = seq // rows
    return pl.pallas_call(
        functools.partial(_attn_kernel, tq=tq, seq=seq, nsub=nsub),
        grid=(b, HEADS, nsteps),
        in_specs=[
            pl.BlockSpec(memory_space=pltpu.SMEM),
            pl.BlockSpec((4, QK_DIM), lambda b_, h, i: (0, 0)),
            pl.BlockSpec((None, rows, HEAD_W), lambda b_, h, i: (b_, i, h)),
            pl.BlockSpec((None, rows, HEAD_W),
                         lambda b_, h, i: (b_, jnp.minimum(i + 1, nsteps - 1), h)),
            pl.BlockSpec((None, seq, HEAD_W), lambda b_, h, i: (b_, 0, HEADS + h)),
            pl.BlockSpec((None, seq, HEAD_W), lambda b_, h, i: (b_, 0, 2 * HEADS + h)),
            pl.BlockSpec((LEAD, HEAD_W), lambda b_, h, i: (0, HEADS + h)),
            pl.BlockSpec((LEAD, HEAD_W), lambda b_, h, i: (0, 2 * HEADS + h)),
            pl.BlockSpec((1, HEAD_W), lambda b_, h, i: (0, 0)),
            pl.BlockSpec((1, HEAD_W), lambda b_, h, i: (0, 0)),
            pl.BlockSpec((HEAD_W, 1), lambda b_, h, i: (0, 0)),
        ],
        out_specs=pl.BlockSpec((None, rows, HEAD_W), lambda b_, h, i: (b_, i, h)),
        out_shape=jax.ShapeDtypeStruct((b, seq, GROUP_W), BF16),
        scratch_shapes=[
            pltpu.VMEM((seq, HEAD_W), BF16),
            pltpu.VMEM((VT_ROWS, seq), BF16),
            pltpu.VMEM((LEAD, HEAD_W), BF16),
            pltpu.VMEM((VT_ROWS, LEAD), BF16),
            pltpu.VMEM((tq, HEAD_W), BF16),
            pltpu.VMEM((nsub, 2, VT_ROWS, tq), F32),
            pltpu.VMEM((2, tq, tq), F32),
            pltpu.VMEM((2, tq, tq), F32),
            pltpu.VMEM((nsub, 2, 2 * HEAD_W, tq), BF16),
            pltpu.VMEM((nsub, 2, 2 * HEAD_W, tq), BF16),
            pltpu.VMEM((rows, HEAD_W), BF16),
            pltpu.VMEM((VT_ROWS, rows), BF16),
        ],
        compiler_params=pltpu.CompilerParams(
            dimension_semantics=("parallel", "parallel", "arbitrary"),
            vmem_limit_bytes=VMEM_LIMIT),
        name="diff_attn",
    )(slopes, lvec, proj3, proj3, proj3, proj3, lead_proj, lead_proj, qw, kw, sw)


def _gdn_kernel(tq_ref, tk_ref, tv_ref, hq_ref, hk_ref, hv_ref, lq_ref, lk_ref, lv_ref,
                tg_ref, lg_ref, cwq_ref, cwk_ref, cwv_ref, alog_ref, dtb_ref, z_ref, nw_ref,
                o_ref, xs_ref, gs_ref, s_ref):
    s = pl.program_id(1)
    is_lead = s == 0
    width = GROUP_W
    srcs = ((tq_ref, hq_ref, lq_ref), (tk_ref, hk_ref, lk_ref), (tv_ref, hv_ref, lv_ref))

    @pl.when(is_lead)
    def _():
        rowid = lax.broadcasted_iota(jnp.int32, (CHUNK, width), 0)
        for idx, (_, _, l_ref) in enumerate(srcs):
            cs = slice(idx * width, (idx + 1) * width)
            xs_ref[0:HIST, cs] = jnp.zeros((HIST, width), BF16)
            xs_ref[HIST:, cs] = jnp.where(rowid >= N_PAD, l_ref[...], jnp.zeros((), BF16))
        gs_ref[...] = lg_ref[...]
        s_ref[...] = jnp.zeros_like(s_ref)

    @pl.when(s == 1)
    def _():
        for idx, (t_ref, _, l_ref) in enumerate(srcs):
            cs = slice(idx * width, (idx + 1) * width)
            xs_ref[0:HIST, cs] = l_ref[LEAD - HIST:LEAD, :]
            xs_ref[HIST:, cs] = t_ref[...]
        gs_ref[...] = tg_ref[...]

    @pl.when(s > 1)
    def _():
        for idx, (t_ref, h_ref, _) in enumerate(srcs):
            cs = slice(idx * width, (idx + 1) * width)
            xs_ref[0:HIST, cs] = h_ref[...]
            xs_ref[HIST:, cs] = t_ref[...]
        gs_ref[...] = tg_ref[...]

    sel_r = lax.broadcasted_iota(jnp.int32, ((CONV_K - 1) * CHUNK, HIST + CHUNK), 0)
    sel_c = lax.broadcasted_iota(jnp.int32, ((CONV_K - 1) * CHUNK, HIST + CHUNK), 1)
    tap_of = sel_r // CHUNK
    shifted = _dot((sel_c == HIST + sel_r % CHUNK - (CONV_K - 1 - tap_of)).astype(BF16),
                   xs_ref[...])

    rowi = lax.broadcasted_iota(jnp.int32, (CHUNK, CHUNK), 0)
    lanei = lax.broadcasted_iota(jnp.int32, (CHUNK, CHUNK), 1)
    incl = rowi >= lanei
    eye = (rowi == lanei).astype(F32)

    g_t = gs_ref[...].T[0:2 * HEADS]
    vmask = (lax.broadcasted_iota(jnp.int32, (HEADS, CHUNK), 1)
             >= jnp.where(is_lead, N_PAD, 0)).astype(F32)
    beta_t = jax.nn.sigmoid(g_t[0:HEADS]) * vmask
    t = g_t[HEADS:] + dtb_ref[...]
    softplus = jnp.maximum(t, 0.0) + jnp.log(1.0 + jnp.exp(-jnp.abs(t)))
    decay_t = -jnp.exp(alog_ref[...]) * softplus * vmask
    gc_t = _dot(decay_t, (rowi <= lanei).astype(F32), HIGHEST)
    cols = jnp.concatenate(
        [beta_t, gc_t, jnp.zeros((CHUNK - 2 * HEADS, CHUNK), F32)], axis=0).T

    heads = range(HEADS)
    hcols = [slice(h * HEAD_W, (h + 1) * HEAD_W) for h in heads]
    lmats, rhs, qgs, qkds, decs = [], [], [], [], []
    for hh in heads:
        hs = hcols[hh]
        beta = cols[:, hh:hh + 1]
        gc = cols[:, HEADS + hh:HEADS + hh + 1]
        gc_row = gc_t[hh:hh + 1]
        g_last = gc_row[:, CHUNK - 1:CHUNK]

        def conv_silu(idx, cw_ref):
            c0 = idx * width + hh * HEAD_W
            y = xs_ref[HIST:, c0:c0 + HEAD_W].astype(F32) * cw_ref[CONV_K - 1:CONV_K, hs]
            for j in range(CONV_K - 1):
                y = y + shifted[j * CHUNK:(j + 1) * CHUNK, c0:c0 + HEAD_W] * cw_ref[j:j + 1, hs]
            return y * jax.nn.sigmoid(y)

        q = conv_silu(0, cwq_ref)
        k = conv_silu(1, cwk_ref)
        v = conv_silu(2, cwv_ref)
        q = q * lax.rsqrt(jnp.sum(q * q, axis=-1, keepdims=True) + EPS) * (HEAD_W ** -0.5)
        k = k * lax.rsqrt(jnp.sum(k * k, axis=-1, keepdims=True) + EPS)

        decay = jnp.where(incl, jnp.exp(jnp.where(incl, gc - gc_row, 0.0)), 0.0)
        kb = k * beta
        k_t = k.T
        kt16 = k_t.astype(BF16)
        lmat = jnp.where(rowi > lanei, _dot(kb.astype(BF16), kt16) * decay, 0.0)
        qkds.append(jnp.concatenate(
            [(_dot(q.astype(BF16), kt16) * decay).astype(BF16),
             (k_t * jnp.exp(g_last - gc_row)).astype(BF16)], axis=0))
        qgs.append((q * jnp.exp(gc)).astype(BF16))
        decs.append(jnp.exp(g_last))
        lmats.append(lmat)
        rhs.append(jnp.concatenate([v * beta, kb * jnp.exp(gc)], axis=1))

    def same_block(size):
        return (rowi // size) == (lanei // size)

    l_sp = [_split(l) for l in lmats]
    mks = [jnp.where(same_block(8), -l, 0.0) for l in lmats]
    pks = [eye + m for m in mks]
    m_sp = [_split(m) for m in mks]
    m_sp = [_split(_dot3(m, m)) for m in m_sp]
    rs = [_dot3(m, [_split(p), m]) for m, p in zip(m_sp, pks)]
    pks = [p + r[:, :CHUNK] for p, r in zip(pks, rs)]
    pks = [p + _dot3(_split(r[:, CHUNK:]), _split(p)) for p, r in zip(pks, rs)]
    size = 8
    while size < CHUNK:
        join = same_block(2 * size) & jnp.logical_not(same_block(size))
        zero = jnp.zeros((), BF16)
        c_sp = [(jnp.where(join, hi, zero), jnp.where(join, lo, zero)) for hi, lo in l_sp]
        p_sp = [_split(p) for p in pks]
        if size < 32:
            xs = [_dot3(c, p) for c, p in zip(c_sp, p_sp)]
            pks = [p - _dot3(ps, _split(x)) for p, ps, x in zip(pks, p_sp, xs)]
        else:
            starts = range(size, CHUNK, 2 * size)
            low = lambda a: jnp.concatenate([a[r:r + size] for r in starts], axis=0)
            gap = jnp.zeros((size, CHUNK), F32)

            def spread(rows):
                parts = []
                for n in range(len(starts)):
                    parts += [gap, rows[n * size:(n + 1) * size]]
                return jnp.concatenate(parts, axis=0)

            xs = [_dot3((low(c[0]), low(c[1])), p) for c, p in zip(c_sp, p_sp)]
            ys = [_dot3((low(ps[0]), low(ps[1])), _split(spread(x))) for ps, x in zip(p_sp, xs)]
            pks = [p - spread(y) for p, y in zip(pks, ys)]
        size *= 2
    uws = [_dot3(_split(p), _split(r)) for p, r in zip(pks, rhs)]

    states = [s_ref[h] for h in heads]
    ws_qs = [_dot(jnp.concatenate([uws[h][:, HEAD_W:].astype(BF16), qgs[h]], axis=0),
                  states[h].astype(BF16)) for h in heads]
    v16 = [(uws[h][:, :HEAD_W] - ws_qs[h][:CHUNK]).astype(BF16) for h in heads]
    qv_kv = [_dot(qkds[h], v16[h]) for h in heads]
    for h in heads:
        s_ref[h] = decs[h] * states[h] + qv_kv[h][CHUNK:]
        o = ws_qs[h][CHUNK:] + qv_kv[h][:CHUNK]
        z = z_ref[:, hcols[h]].astype(F32)
        o_ref[:, hcols[h]] = (
            _rms_rows(o, nw_ref[...]) * (z * jax.nn.sigmoid(z))).astype(o_ref.dtype)


def _gdn(proj3, lead_proj, gates3, lead_gates, conv_wt, alog_col, dtb_col, onw):
    b, seq, _ = proj3.shape
    nb = 1 + seq // CHUNK
    tok = lambda g: pl.BlockSpec(
        (None, CHUNK, GROUP_W), lambda b_, s: (b_, jnp.maximum(s - 1, 0), g))
    halo = lambda g: pl.BlockSpec(
        (None, 16, GROUP_W),
        lambda b_, s: (b_, jnp.maximum((s - 1) * (CHUNK // 16) - 1, 0), g))
    lead = lambda g: pl.BlockSpec((LEAD, GROUP_W), lambda b_, s: (0, g))
    cw = lambda g: pl.BlockSpec((CONV_K, GROUP_W), lambda b_, s: (0, g))
    col = pl.BlockSpec((HEADS, 1), lambda b_, s: (0, 0))
    return pl.pallas_call(
        _gdn_kernel,
        grid=(b, nb),
        in_specs=[
            tok(3), tok(4), tok(5), halo(3), halo(4), halo(5), lead(3), lead(4), lead(5),
            pl.BlockSpec((None, CHUNK, HEAD_W), lambda b_, s: (b_, jnp.maximum(s - 1, 0), 0)),
            pl.BlockSpec((LEAD, HEAD_W), lambda b_, s: (0, 0)),
            cw(0), cw(1), cw(2), col, col, tok(6),
            pl.BlockSpec((1, HEAD_W), lambda b_, s: (0, 0)),
        ],
        out_specs=pl.BlockSpec((None, CHUNK, GROUP_W), lambda b_, s: (b_, jnp.maximum(s - 1, 0), 0)),
        out_shape=jax.ShapeDtypeStruct((b, seq, GROUP_W), BF16),
        scratch_shapes=[pltpu.VMEM((HIST + CHUNK, 3 * GROUP_W), BF16),
                        pltpu.VMEM((CHUNK, HEAD_W), F32),
                        pltpu.VMEM((HEADS, HEAD_W, HEAD_W), F32)],
        compiler_params=pltpu.CompilerParams(
            dimension_semantics=("parallel", "arbitrary"), vmem_limit_bytes=VMEM_LIMIT),
        name="gdn",
    )(proj3, proj3, proj3, proj3, proj3, proj3, lead_proj, lead_proj, lead_proj,
      gates3, lead_gates, conv_wt, conv_wt, conv_wt, alog_col, dtb_col, proj3, onw)


def _outproj_kernel(x_ref, oa_ref, od_ref, wa_ref, wd_ref, o_ref):
    o_ref[...] = x_ref[...] + _dot(oa_ref[...], wa_ref[...]) + _dot(od_ref[...], wd_ref[...])


def _outproj(x2d, oa, od, w_out16, tm):
    m = x2d.shape[0]
    assert m % tm == 0
    return pl.pallas_call(
        _outproj_kernel,
        grid=(m // tm,),
        in_specs=[
            pl.BlockSpec((tm, D_MODEL), lambda i: (i, 0)),
            pl.BlockSpec((tm, GROUP_W), lambda i: (i, 0)),
            pl.BlockSpec((tm, GROUP_W), lambda i: (i, 0)),
            pl.BlockSpec((GROUP_W, D_MODEL), lambda i: (0, 0)),
            pl.BlockSpec((GROUP_W, D_MODEL), lambda i: (1, 0)),
        ],
        out_specs=pl.BlockSpec((tm, D_MODEL), lambda i: (i, 0)),
        out_shape=jax.ShapeDtypeStruct((m, D_MODEL), F32),
        compiler_params=pltpu.CompilerParams(
            dimension_semantics=("parallel",), vmem_limit_bytes=VMEM_LIMIT),
        name="outproj",
    )(x2d, oa, od, w_out16, w_out16)


def _ffn_kernel(h_ref, nw_ref, wg_ref, wu_ref, wd_ref, o_ref, u_ref, *, row_chunk):
    j = pl.program_id(1)

    @pl.when(j == 0)
    def _():
        def body(c, carry):
            r = pl.multiple_of(c * row_chunk, row_chunk)
            x = h_ref[pl.ds(r, row_chunk), :]
            u_ref[pl.ds(r, row_chunk), :] = _rms_rows(x, nw_ref[...]).astype(BF16)
            o_ref[pl.ds(r, row_chunk), :] = x
            return carry

        lax.fori_loop(0, h_ref.shape[0] // row_chunk, body, 0)

    u = u_ref[...]
    g = _dot(u, wg_ref[...])
    a = (g * jax.nn.sigmoid(g) * _dot(u, wu_ref[...])).astype(BF16)
    o_ref[...] += _dot(a, wd_ref[...])


def _ffn(h2d, norm_w, wg, wu, wd, tm, th):
    m = h2d.shape[0]
    assert m % tm == 0 and FFN_HIDDEN % th == 0
    return pl.pallas_call(
        functools.partial(_ffn_kernel, row_chunk=min(256, tm)),
        grid=(m // tm, FFN_HIDDEN // th),
        in_specs=[
            pl.BlockSpec((tm, D_MODEL), lambda i, j: (i, 0)),
            pl.BlockSpec((1, D_MODEL), lambda i, j: (0, 0)),
            pl.BlockSpec((D_MODEL, th), lambda i, j: (0, j)),
            pl.BlockSpec((D_MODEL, th), lambda i, j: (0, j)),
            pl.BlockSpec((th, D_MODEL), lambda i, j: (j, 0)),
        ],
        out_specs=pl.BlockSpec((tm, D_MODEL), lambda i, j: (i, 0)),
        out_shape=jax.ShapeDtypeStruct((m, D_MODEL), F32),
        scratch_shapes=[pltpu.VMEM((tm, D_MODEL), BF16)],
        compiler_params=pltpu.CompilerParams(
            dimension_semantics=("parallel", "arbitrary"), vmem_limit_bytes=VMEM_LIMIT),
        name="ffn",
    )(h2d, norm_w, wg, wu, wd)


def kernel(x, meta_tokens, attn_norm_w, w_in, q_norm_w, k_norm_w, lambda_q1, lambda_k1, lambda_q2,
           lambda_k2, subln_w, conv_w, a_log, dt_bias, o_norm_w, w_out, ffn_norm_w, w_gate, w_up,
           w_down):
    b, seq, _ = x.shape
    m = b * seq
    x2d = x.reshape(m, D_MODEL)
    lead = jnp.concatenate([jnp.zeros((N_PAD, D_MODEL), x.dtype), meta_tokens.astype(x.dtype)], 0)

    w_main = w_in[0, :, :MAIN_COLS].astype(BF16)
    w_gates = jnp.pad(w_in[0, :, MAIN_COLS:], ((0, 0), (0, HEAD_W - GATE_COLS))).astype(BF16)
    proj, gates = _inproj(x2d, attn_norm_w, w_main, w_gates, min(INPROJ_ROWS, m), INPROJ_COLS)
    lead_proj, lead_gates = _inproj(lead, attn_norm_w, w_main, w_gates, LEAD, INPROJ_COLS)
    proj3 = proj.reshape(b, seq, MAIN_COLS)

    slopes = 2.0 ** (-8.0 * jnp.arange(1, HEADS + 1, dtype=F32) / HEADS)
    lvec = jnp.concatenate([lambda_q1, lambda_k1, lambda_q2, lambda_k2], 0).astype(F32)
    o_a = _attn(proj3, lead_proj, slopes, lvec, jnp.tile(q_norm_w, (1, 2)),
                jnp.tile(k_norm_w, (1, 2)), subln_w.reshape(HEAD_W, 1), ATTN_BLOCK, ATTN_NSUB)

    o_d = _gdn(
        proj3, lead_proj, gates.reshape(b, seq, HEAD_W), lead_gates, conv_w[0].T,
        a_log.astype(F32).reshape(HEADS, 1), dt_bias.astype(F32).reshape(HEADS, 1), o_norm_w)

    h1 = _outproj(x2d, o_a.reshape(m, GROUP_W), o_d.reshape(m, GROUP_W), w_out[0].astype(BF16),
                  min(OUTPROJ_ROWS, m))
    out = _ffn(h1, ffn_norm_w, w_gate[0].astype(BF16), w_up[0].astype(BF16),
               w_down[0].astype(BF16), min(FFN_ROWS, m), FFN_COLS)
    return out.reshape(b, seq, D_MODEL)
```
